```python
import math
import jax, jax.numpy as jnp
from jax import lax
import numpy as np

D_MODEL = 1024
BATCH = 4
SEQ = 4096
DEPTH = 1
DEC_BATCH = 128
DEC_SEQ = 1
PAST_LEN = 8192
PAGE_SIZE = 128

HEAD_DIM = 64
MIX_WIDTH = D_MODEL
RET_HEADS = (MIX_WIDTH // 2) // HEAD_DIM
ATT_HEADS = (MIX_WIDTH // 2) // HEAD_DIM
RET_WIDTH = RET_HEADS * HEAD_DIM
ATT_WIDTH = ATT_HEADS * HEAD_DIM
IN_COLS = 4 * RET_WIDTH + 3 * ATT_WIDTH
RET_CHUNK = 128
DIL_PATTERNS = ((128, 1), (512, 4), (2048, 16))
DIL_BLOCK = 128
MAX_WINDOW = 2048
N_MEM = 256
X_HEADS = 4
X_HEAD_DIM = 128
X_WIDTH = X_HEADS * X_HEAD_DIM
N_EXPERTS = 32
TOP_K = 4
D_FF = D_MODEL
SWIGLU_LIMIT = 7.0
SWIGLU_ALPHA = 1.702
MOE_BLOCK = 128
ROPE_THETA = 10000.0
EPS = 1e-6

kernel_name = "hybrid_retention_dilated_moe_step"

F32 = jnp.float32


def rms_norm(x, g):
    xf = x.astype(F32)
    y = xf * lax.rsqrt(jnp.mean(xf * xf, axis=-1, keepdims=True) + EPS) * g.astype(F32)
    return y.astype(x.dtype)


def rope(x, pos):
    half = x.shape[-1] // 2
    inv = jnp.exp(-math.log(ROPE_THETA) * jnp.arange(half, dtype=F32) / half)
    ang = pos.astype(F32)[:, None] * inv[None, :]
    cos = jnp.cos(ang)[:, None, :]
    sin = jnp.sin(ang)[:, None, :]
    x1, x2 = x[..., :half], x[..., half:]
    return jnp.concatenate([x1 * cos - x2 * sin, x2 * cos + x1 * sin], axis=-1)


def mixer_inputs(xn, pos, w_in, g_att_q, g_att_k):
    N, T, _ = xn.shape
    proj = xn @ w_in
    cuts = np.cumsum([RET_WIDTH] * 4 + [ATT_WIDTH] * 2).tolist()
    rq, rk, rv, rg, aq, ak, av = jnp.split(proj, cuts, axis=-1)
    rq = rope(rq.reshape(N, T, RET_HEADS, HEAD_DIM).astype(F32), pos)
    rk = rope(rk.reshape(N, T, RET_HEADS, HEAD_DIM).astype(F32), pos) * (HEAD_DIM ** -0.5)
    rv = rv.reshape(N, T, RET_HEADS, HEAD_DIM).astype(F32)
    aq = rope(rms_norm(aq.reshape(N, T, ATT_HEADS, HEAD_DIM), g_att_q).astype(F32), pos)
    ak = rope(rms_norm(ak.reshape(N, T, ATT_HEADS, HEAD_DIM), g_att_k).astype(F32), pos)
    av = av.reshape(N, T, ATT_HEADS, HEAD_DIM).astype(F32)
    return rq, rk, rv, rg, aq, ak, av


def mixer_output(ret_o, rg, att_o, g_ret_gn, w_out, dtype):
    N, T = ret_o.shape[:2]
    mu = jnp.mean(ret_o, axis=-1, keepdims=True)
    var = jnp.mean(jnp.square(ret_o - mu), axis=-1, keepdims=True)
    ret_n = ((ret_o - mu) * lax.rsqrt(var + EPS) * g_ret_gn.astype(F32)).reshape(N, T, RET_WIDTH)
    ret_n = ret_n * jax.nn.silu(rg.astype(F32))
    mixed = jnp.concatenate([ret_n, att_o.reshape(N, T, ATT_WIDTH)], axis=-1).astype(dtype)
    return mixed @ w_out


def retention_chunk(state, q, k, v, log_g):
    C = q.shape[1]
    idx = jnp.arange(C, dtype=F32)
    diff = idx[:, None] - idx[None, :]
    decay = jnp.where(diff[None] >= 0,
                      jnp.exp(jnp.maximum(diff, 0.0)[None] * log_g[:, None, None]), 0.0)
    scores = jnp.einsum('nihd,njhd->nhij', q, k) * decay[None]
    o = jnp.einsum('nhij,njhv->nihv', scores, v)
    q_dec = jnp.exp((idx + 1.0)[:, None] * log_g[None, :])
    o = o + jnp.einsum('nihd,nhdv->nihv', q, state) * q_dec[None, :, :, None]
    k_dec = jnp.exp((C - 1.0 - idx)[:, None] * log_g[None, :])
    new_state = (jnp.exp(C * log_g)[None, :, None, None] * state
                 + jnp.einsum('njhd,njhv->nhdv', k * k_dec[None, :, :, None], v))
    return new_state, o


def retention_prompt(q, k, v, log_g):
    N, S, H, Dk = q.shape
    n_chunks = S // RET_CHUNK

    def chunks(t):
        return jnp.moveaxis(t.reshape(N, n_chunks, RET_CHUNK, H, t.shape[-1]), 1, 0)

    def body(state, qkv):
        qc, kc, vc = qkv
        return retention_chunk(state, qc, kc, vc, log_g)

    init = jnp.zeros((N, H, Dk, v.shape[-1]), F32)
    final, o = lax.scan(body, init, (chunks(q), chunks(k), chunks(v)))
    return jnp.moveaxis(o, 0, 1).reshape(N, S, H, v.shape[-1]), final


def softmax_stats(s, mask):
    s = jnp.where(mask, s, -jnp.inf)
    m = jnp.max(s, axis=-1)
    p = jnp.exp(s - m[..., None])
    return p, m, jnp.sum(p, axis=-1)


def combine_patterns(o_s, m_s, l_s):
    w = l_s * jnp.exp(m_s - jnp.max(m_s, axis=0, keepdims=True))
    return jnp.einsum('pnth,pnthd->nthd', w, o_s) / jnp.sum(w, axis=0)[..., None]


def dilated_prompt(q, k, v):
    N, S, H, Dh = q.shape
    B = DIL_BLOCK
    scale = Dh ** -0.5
    outs, maxs, dens = [], [], []
    for window, dil in DIL_PATTERNS:
        steps = window // dil
        S_pad = -(-S // (dil * B)) * (dil * B)
        M = S_pad // dil
        nb = M // B

        def to_res(t):
            t = jnp.pad(t, ((0, 0), (0, S_pad - S), (0, 0), (0, 0))).reshape(N, M, dil, H, Dh)
            return jnp.moveaxis(t, 2, 1).reshape(N, dil, nb, B, H, Dh)

        def with_prev(t):
            prev = jnp.pad(t[:, :, :-1], ((0, 0), (0, 0), (1, 0), (0, 0), (0, 0), (0, 0)))
            return jnp.concatenate([prev, t], axis=3)

        def from_res(t):
            rest = t.shape[4:]
            t = t.reshape((N, dil, M) + rest)
            return jnp.moveaxis(t, 1, 2).reshape((N, S_pad) + rest)[:, :S]

        qr = to_res(q)
        kr = with_prev(to_res(k))
        vr = with_prev(to_res(v))
        s = jnp.einsum('nrbqhd,nrbkhd->nrbqhk', qr, kr) * scale
        dist = (jnp.arange(B) + B)[:, None] - jnp.arange(2 * B)[None, :]
        band = (dist >= 0) & (dist <= steps)
        has_prev = (jnp.arange(nb)[:, None, None] > 0) | (jnp.arange(2 * B)[None, None, :] >= B)
        mask = (band[None] & has_prev)[None, None, :, :, None, :]
        p, m, l = softmax_stats(s, mask)
        o = jnp.einsum('nrbqhk,nrbkhd->nrbqhd', p, vr) / l[..., None]
        outs.append(from_res(o))
        maxs.append(from_res(m))
        dens.append(from_res(l))
    return combine_patterns(jnp.stack(outs), jnp.stack(maxs), jnp.stack(dens))


def dilated_sample(q, k, v, k_buf, v_buf):
    N, T, H, Dh = q.shape
    W = k_buf.shape[1]
    scale = Dh ** -0.5
    kc = jnp.concatenate([k_buf.astype(F32), k], axis=1)
    vc = jnp.concatenate([v_buf.astype(F32), v], axis=1)
    t = jnp.arange(T)
    outs, maxs, dens = [], [], []
    for window, dil in DIL_PATTERNS:
        steps = window // dil
        j = jnp.arange(steps + 1)
        idx = W + t[:, None] - dil * j[None, :]
        valid = idx >= 0
        idx = jnp.maximum(idx, 0)
        kg = kc[:, idx]
        vg = vc[:, idx]
        s = jnp.einsum('nthd,ntjhd->nthj', q, kg) * scale
        p, m, l = softmax_stats(s, valid[None, :, None, :])
        outs.append(jnp.einsum('nthj,ntjhd->nthd', p, vg) / l[..., None])
        maxs.append(m)
        dens.append(l)
    out = combine_patterns(jnp.stack(outs), jnp.stack(maxs), jnp.stack(dens))
    return out, kc[:, T:], vc[:, T:]


def memory_kv(mem, g_mem, w_kv_x, g_x_k):
    N, Mm, _ = mem.shape
    kv = rms_norm(mem, g_mem) @ w_kv_x
    k, v = jnp.split(kv, 2, axis=-1)
    k = rms_norm(k.reshape(N, Mm, X_HEADS, X_HEAD_DIM), g_x_k)
    return k, v.reshape(N, Mm, X_HEADS, X_HEAD_DIM)


def cross_attn(xn, mem_k, mem_v, w_q_x, g_x_q, w_o_x):
    N, T, _ = xn.shape
    q = rms_norm((xn @ w_q_x).reshape(N, T, X_HEADS, X_HEAD_DIM), g_x_q).astype(F32)
    s = jnp.einsum('nthd,nmhd->nhtm', q, mem_k.astype(F32)) * (X_HEAD_DIM ** -0.5)
    p = jax.nn.softmax(s, axis=-1)
    o = jnp.einsum('nhtm,nmhd->nthd', p, mem_v.astype(F32)).reshape(N, T, X_WIDTH)
    return o.astype(xn.dtype) @ w_o_x


def moe_ffn(x, w_router, b_router, w_gu, b_gu, w_down, b_down):
    T, D = x.shape
    TK = T * TOP_K
    logits = x.astype(F32) @ w_router.astype(F32) + b_router.astype(F32)
    top_val, top_idx = lax.top_k(logits, TOP_K)
    gates = jax.nn.softmax(top_val, axis=-1)
    flat_e = top_idx.reshape(-1).astype(jnp.int32)
    order = jnp.argsort(flat_e)
    sorted_e = flat_e[order]
    counts = jnp.bincount(flat_e, length=N_EXPERTS).astype(jnp.int32)
    padded = (counts + MOE_BLOCK - 1) // MOE_BLOCK * MOE_BLOCK
    pad_end = jnp.cumsum(padded).astype(jnp.int32)
    pad_start = pad_end - padded
    start = jnp.cumsum(counts).astype(jnp.int32) - counts
    rank = jnp.arange(TK, dtype=jnp.int32) - start[sorted_e]
    dest = pad_start[sorted_e] + rank
    n_blocks = -(-TK // MOE_BLOCK) + N_EXPERTS
    cap = n_blocks * MOE_BLOCK
    row_tok = jnp.full((cap,), T, jnp.int32).at[dest].set((order // TOP_K).astype(jnp.int32))
    row_gate = jnp.zeros((cap,), F32).at[dest].set(gates.reshape(-1)[order])
    block_e = jnp.minimum(
        jnp.searchsorted(pad_end, jnp.arange(n_blocks, dtype=jnp.int32) * MOE_BLOCK, side='right'),
        N_EXPERTS - 1)
    x_pad = jnp.concatenate([x, jnp.zeros((1, D), x.dtype)], axis=0)
    xb = x_pad[row_tok].reshape(n_blocks, MOE_BLOCK, D)

    def expert_block(args):
        xblk, e = args
        h = xblk @ w_gu[e] + b_gu[e]
        glu, lin = h[:, :D_FF], h[:, D_FF:]
        glu = jnp.minimum(glu, SWIGLU_LIMIT)
        lin = jnp.clip(lin, -SWIGLU_LIMIT, SWIGLU_LIMIT)
        act = glu * jax.nn.sigmoid(SWIGLU_ALPHA * glu) * (lin + 1.0)
        return act @ w_down[e] + b_down[e]

    yb = lax.map(expert_block, (xb, block_e)).reshape(cap, D)
    y = jax.ops.segment_sum(yb * row_gate[:, None].astype(yb.dtype), row_tok, num_segments=T + 1)
    return y[:T]


def memory_and_ffn(h, mem_k, mem_v, g_norm2, w_q_x, g_x_q, w_o_x, g_norm3,
                   w_router, b_router, w_gu, b_gu, w_down, b_down):
    h = h + cross_attn(rms_norm(h, g_norm2), mem_k, mem_v, w_q_x, g_x_q, w_o_x)
    N, T, D = h.shape
    f = moe_ffn(rms_norm(h, g_norm3).reshape(N * T, D), w_router, b_router, w_gu, b_gu, w_down, b_down)
    return h + f.reshape(N, T, D)


def setup_inputs(seed: int = 0) -> dict:
    key = jax.random.key(seed)
    ks = jax.random.split(key, 32)
    W_BUF = min(MAX_WINDOW, PAST_LEN)
    nrm = lambda k, shape, s=1.0: jax.random.normal(k, shape, F32) * s
    gain = lambda k, shape: 1.0 + 0.02 * jax.random.normal(k, shape, F32)
    return {
        "x_prompt": nrm(ks[0], (BATCH, SEQ, D_MODEL)),
        "x_sample": nrm(ks[1], (DEC_BATCH, DEC_SEQ, D_MODEL)),
        "state_ret": nrm(ks[2], (DEPTH, DEC_BATCH, RET_HEADS, HEAD_DIM, HEAD_DIM), 0.3),
        "cache_win_k": nrm(ks[3], (DEPTH, DEC_BATCH, W_BUF, ATT_HEADS, HEAD_DIM)),
        "cache_win_v": nrm(ks[4], (DEPTH, DEC_BATCH, W_BUF, ATT_HEADS, HEAD_DIM)),
        "cache_mem_k": nrm(ks[5], (DEPTH, DEC_BATCH, N_MEM, X_HEADS, X_HEAD_DIM)),
        "cache_mem_v": nrm(ks[6], (DEPTH, DEC_BATCH, N_MEM, X_HEADS, X_HEAD_DIM)),
        "mem_prompt": nrm(ks[7], (BATCH, N_MEM, D_MODEL)),
        "g_norm1": gain(ks[8], (DEPTH, D_MODEL)),
        "w_in": nrm(ks[9], (DEPTH, D_MODEL, IN_COLS), D_MODEL ** -0.5),
        "g_att_q": gain(ks[10], (DEPTH, HEAD_DIM)),
        "g_att_k": gain(ks[11], (DEPTH, HEAD_DIM)),
        "g_ret_gn": gain(ks[12], (DEPTH, RET_HEADS, HEAD_DIM)),
        "w_out": nrm(ks[13], (DEPTH, MIX_WIDTH, D_MODEL), MIX_WIDTH ** -0.5),
        "g_norm2": gain(ks[14], (DEPTH, D_MODEL)),
        "g_mem": gain(ks[15], (DEPTH, D_MODEL)),
        "w_q_x": nrm(ks[16], (DEPTH, D_MODEL, X_WIDTH), D_MODEL ** -0.5),
        "w_kv_x": nrm(ks[17], (DEPTH, D_MODEL, 2 * X_WIDTH), D_MODEL ** -0.5),
        "g_x_q": gain(ks[18], (DEPTH, X_HEAD_DIM)),
        "g_x_k": gain(ks[19], (DEPTH, X_HEAD_DIM)),
        "w_o_x": nrm(ks[20], (DEPTH, X_WIDTH, D_MODEL), X_WIDTH ** -0.5),
        "g_norm3": gain(ks[21], (DEPTH, D_MODEL)),
        "w_router": nrm(ks[22], (DEPTH, D_MODEL, N_EXPERTS), D_MODEL ** -0.5),
        "b_router": nrm(ks[23], (DEPTH, N_EXPERTS), 0.01),
        "w_gu": nrm(ks[24], (DEPTH, N_EXPERTS, D_MODEL, 2 * D_FF), D_MODEL ** -0.5),
        "b_gu": nrm(ks[25], (DEPTH, N_EXPERTS, 2 * D_FF), 0.01),
        "w_down": nrm(ks[26], (DEPTH, N_EXPERTS, D_FF, D_MODEL), D_FF ** -0.5),
        "b_down": nrm(ks[27], (DEPTH, N_EXPERTS, D_MODEL), 0.01),
    }


def reference(x_prompt, x_sample, state_ret, cache_win_k, cache_win_v, cache_mem_k, cache_mem_v,
              mem_prompt, g_norm1, w_in, g_att_q, g_att_k, g_ret_gn, w_out, g_norm2, g_mem,
              w_q_x, w_kv_x, g_x_q, g_x_k, w_o_x, g_norm3, w_router, b_router, w_gu, b_gu,
              w_down, b_down):
    log_g = jnp.log1p(-jnp.exp2(-5.0 - jnp.arange(RET_HEADS, dtype=F32)))
    S = x_prompt.shape[1]
    T = x_sample.shape[1]
    pos_p = jnp.arange(S, dtype=jnp.int32)
    pos_s = PAST_LEN + jnp.arange(T, dtype=jnp.int32)
    wp = min(MAX_WINDOW, S)
    yp, ys = x_prompt, x_sample
    ret_p, ret_s, wk_p, wv_p, wk_s, wv_s, mk_p, mv_p = [], [], [], [], [], [], [], []
    for l in range(DEPTH):
        rq, rk, rv, rg, aq, ak, av = mixer_inputs(rms_norm(yp, g_norm1[l]), pos_p, w_in[l], g_att_q[l], g_att_k[l])
        ret_o, st_p = retention_prompt(rq, rk, rv, log_g)
        att_o = dilated_prompt(aq, ak, av)
        hp = yp + mixer_output(ret_o, rg, att_o, g_ret_gn[l], w_out[l], yp.dtype)
        mk, mv = memory_kv(mem_prompt, g_mem[l], w_kv_x[l], g_x_k[l])
        yp_next = memory_and_ffn(hp, mk, mv, g_norm2[l], w_q_x[l], g_x_q[l], w_o_x[l], g_norm3[l],
                                 w_router[l], b_router[l], w_gu[l], b_gu[l], w_down[l], b_down[l])
        ret_p.append(st_p)
        wk_p.append(ak[:, S - wp:].astype(yp.dtype))
        wv_p.append(av[:, S - wp:].astype(yp.dtype))
        mk_p.append(mk)
        mv_p.append(mv)
        rq, rk, rv, rg, aq, ak, av = mixer_inputs(rms_norm(ys, g_norm1[l]), pos_s, w_in[l], g_att_q[l], g_att_k[l])
        st_s, ret_o = retention_chunk(state_ret[l].astype(F32), rq, rk, rv, log_g)
        att_o, kb, vb = dilated_sample(aq, ak, av, cache_win_k[l], cache_win_v[l])
        hs = ys + mixer_output(ret_o, rg, att_o, g_ret_gn[l], w_out[l], ys.dtype)
        ys = memory_and_ffn(hs, cache_mem_k[l], cache_mem_v[l], g_norm2[l], w_q_x[l], g_x_q[l], w_o_x[l],
                            g_norm3[l], w_router[l], b_router[l], w_gu[l], b_gu[l], w_down[l], b_down[l])
        ret_s.append(st_s)
        wk_s.append(kb.astype(ys.dtype))
        wv_s.append(vb.astype(ys.dtype))
        yp = yp_next
    return (yp, ys, jnp.stack(ret_p), jnp.stack(ret_s), jnp.stack(wk_p), jnp.stack(wv_p),
            jnp.stack(wk_s), jnp.stack(wv_s), jnp.stack(mk_p), jnp.stack(mv_p))
```

```python
import functools
import math

import jax
import jax.numpy as jnp
from jax import lax
from jax.experimental import pallas as pl
from jax.experimental.pallas import tpu as pltpu

F32 = jnp.float32
BF16 = jnp.bfloat16
I32 = jnp.int32

D_MODEL = 1024
HEAD_DIM = 64
RET_HEADS = 8
ATT_HEADS = 8
RET_WIDTH = RET_HEADS * HEAD_DIM
ATT_WIDTH = ATT_HEADS * HEAD_DIM
IN_COLS = 4 * RET_WIDTH + 3 * ATT_WIDTH
RET_CHUNK = 128
DIL_PATTERNS = ((128, 1), (512, 4), (2048, 16))
DIL_BLOCK = 128
MAX_WINDOW = 2048
X_HEADS = 4
X_HEAD_DIM = 128
X_WIDTH = X_HEADS * X_HEAD_DIM
N_EXPERTS = 32
TOP_K = 4
D_FF = D_MODEL
SWIGLU_LIMIT = 7.0
SWIGLU_ALPHA = 1.702
ROPE_THETA = 10000.0
EPS = 1e-6
PAST_LEN = 8192

LANES = 128
SUBLANES = 8
HEADS_PER_TILE = LANES // HEAD_DIM

TOKEN_TILE = 384
ROW_TILE = 128
MOE_BLOCK_ROWS = 256
NEG_BIG = -1e30
VMEM_LIMIT = 48 * 1024 * 1024


def _cparams(n_axes):
    return pltpu.CompilerParams(
        dimension_semantics=("arbitrary",) * n_axes, vmem_limit_bytes=VMEM_LIMIT)


def _rms(x, g):
    return x * lax.rsqrt(jnp.mean(x * x, axis=-1, keepdims=True) + EPS) * g


def _dot(a, b):
    return jnp.dot(a, b, preferred_element_type=F32)


def _dot_nt(a, b):
    return lax.dot_general(a, b, (((1,), (1,)), ((), ())), preferred_element_type=F32)


def _dot_tn(a, b):
    return lax.dot_general(a, b, (((0,), (0,)), ((), ())), preferred_element_type=F32)


def _sigmoid(x):
    return 1.0 / (1.0 + jnp.exp(-x))


def _in_proj_kernel(x_ref, g1_ref, w_ref, cos_ref, sin_ref, gq_ref, gk_ref, seg_ref,
                    rq_ref, rk_ref, rv_ref, rg_ref, aq_ref, ak_ref, av_ref):
    xn = _rms(x_ref[...], g1_ref[...]).astype(BF16)
    proj = _dot(xn, w_ref[...])
    reps = RET_WIDTH // LANES
    cos = jnp.concatenate([cos_ref[...]] * reps, axis=-1)
    sin = jnp.concatenate([sin_ref[...]] * reps, axis=-1)
    lane = lax.broadcasted_iota(I32, (1, RET_WIDTH), 1)
    first_half = (lane % HEAD_DIM) < (HEAD_DIM // 2)
    seg = seg_ref[...]

    def rope(t):
        partner = jnp.where(first_half,
                            pltpu.roll(t, RET_WIDTH - HEAD_DIM // 2, 1),
                            pltpu.roll(t, HEAD_DIM // 2, 1))
        return t * cos + partner * sin

    def head_norm(t, g):
        sq = t * t
        hi = sq.astype(BF16)
        lo = (sq - hi.astype(F32)).astype(BF16)
        ssum = _dot(hi, seg) + _dot(lo, seg)
        return t * lax.rsqrt(ssum * (1.0 / HEAD_DIM) + EPS) * g

    w = RET_WIDTH
    rq_ref[...] = rope(proj[:, 0:w])
    rk_ref[...] = rope(proj[:, w:2 * w]) * (HEAD_DIM ** -0.5)
    rv_ref[...] = proj[:, 2 * w:3 * w]
    rg_ref[...] = proj[:, 3 * w:4 * w]
    aq_ref[...] = rope(head_norm(proj[:, 4 * w:5 * w], gq_ref[...]))
    ak_ref[...] = rope(head_norm(proj[:, 5 * w:6 * w], gk_ref[...]))
    av_ref[...] = proj[:, 6 * w:7 * w]


def _in_proj(x_all, g1, w_in, cos_t, sin_t, gq, gk, seg):
    t_all = x_all.shape[0]
    tm = TOKEN_TILE
    row = lambda i: (i, 0)
    fixed = lambda i: (0, 0)
    out = jax.ShapeDtypeStruct((t_all, RET_WIDTH), F32)
    return pl.pallas_call(
        _in_proj_kernel,
        grid=(t_all // tm,),
        in_specs=[pl.BlockSpec((tm, D_MODEL), row),
                  pl.BlockSpec((1, D_MODEL), fixed),
                  pl.BlockSpec((D_MODEL, IN_COLS), fixed),
                  pl.BlockSpec((tm, LANES), row),
                  pl.BlockSpec((tm, LANES), row),
                  pl.BlockSpec((1, ATT_WIDTH), fixed),
                  pl.BlockSpec((1, ATT_WIDTH), fixed),
                  pl.BlockSpec((ATT_WIDTH, ATT_WIDTH), fixed)],
        out_specs=[pl.BlockSpec((tm, RET_WIDTH), row)] * 7,
        out_shape=[out] * 7,
        compiler_params=_cparams(1),
        name="in_proj",
    )(x_all, g1, w_in, cos_t, sin_t, gq, gk, seg)


def _group_norm_gate(o, mask_a, gn, gate):
    inv = 1.0 / HEAD_DIM
    sa = jnp.sum(jnp.where(mask_a, o, 0.0), axis=-1, keepdims=True)
    sb = jnp.sum(jnp.where(mask_a, 0.0, o), axis=-1, keepdims=True)
    cen = o - jnp.where(mask_a, sa, sb) * inv
    c2 = cen * cen
    va = jnp.sum(jnp.where(mask_a, c2, 0.0), axis=-1, keepdims=True)
    vb = jnp.sum(jnp.where(mask_a, 0.0, c2), axis=-1, keepdims=True)
    var = jnp.where(mask_a, va, vb) * inv
    return cen * lax.rsqrt(var + EPS) * gn * (gate * _sigmoid(gate))


def _ret_prompt_kernel(q_ref, k_ref, v_ref, g_ref, gn_ref, lg_ref, o_ref, st_ref, state):
    c = pl.program_id(2)
    ch = RET_CHUNK

    @pl.when(c == 0)
    def _():
        state[...] = jnp.zeros_like(state)

    q = q_ref[...]
    k = k_ref[...]
    v = v_ref[...]
    lg = lg_ref[...]
    lga = lg[:, 0:1]
    lgb = lg[:, HEAD_DIM:HEAD_DIM + 1]
    lane = lax.broadcasted_iota(I32, (1, LANES), 1)
    mask_a = lane < HEAD_DIM
    row_i = lax.broadcasted_iota(I32, (ch, 1), 0)
    row = row_i.astype(F32)
    col = lax.broadcasted_iota(I32, (1, ch), 1).astype(F32)
    diff = row - col
    causal = diff >= 0.0
    dpos = jnp.maximum(diff, 0.0)
    dec_a = jnp.where(causal, jnp.exp(dpos * lga), 0.0)
    dec_b = jnp.where(causal, jnp.exp(dpos * lgb), 0.0)

    kb = k.astype(BF16)
    vb = v.astype(BF16)
    lhs = jnp.concatenate([jnp.where(mask_a, q, 0.0), jnp.where(mask_a, 0.0, q)], axis=0).astype(BF16)
    s = _dot_nt(lhs, kb)
    p = (s * jnp.concatenate([dec_a, dec_b], axis=0)).astype(BF16)
    o2 = _dot(p, vb)
    o_intra = jnp.where(mask_a, o2[:ch], o2[ch:])

    st = state[...]
    o_inter = _dot(q.astype(BF16), st.astype(BF16)) * jnp.exp((row + 1.0) * lg)
    o = o_intra + o_inter

    kdec = jnp.exp((ch - 1.0 - row) * lg)
    upd = _dot_tn((k * kdec).astype(BF16), vb)
    same_head = (row_i // HEAD_DIM) == (lane // HEAD_DIM)
    cdec = jnp.where(row_i < HEAD_DIM, jnp.exp(ch * lga), jnp.exp(ch * lgb))
    state[...] = cdec * st + jnp.where(same_head, upd, 0.0)

    o_ref[...] = _group_norm_gate(o, mask_a, gn_ref[...], g_ref[...])

    @pl.when(c == pl.num_programs(2) - 1)
    def _():
        st_ref[0, 0] = state[...]


def _ret_prompt(rq, rk, rv, rg, gn_row, lg_row, n_batch, seq):
    t_all = rq.shape[0]
    n_pairs = RET_WIDTH // LANES
    n_chunks = seq // RET_CHUNK
    blk = pl.BlockSpec((RET_CHUNK, LANES), lambda n, p, c: (n * n_chunks + c, p))
    lane_row = pl.BlockSpec((1, LANES), lambda n, p, c: (0, p))
    return pl.pallas_call(
        _ret_prompt_kernel,
        grid=(n_batch, n_pairs, n_chunks),
        in_specs=[blk, blk, blk, blk, lane_row, lane_row],
        out_specs=[blk, pl.BlockSpec((1, 1, LANES, LANES), lambda n, p, c: (n, p, 0, 0))],
        out_shape=[jax.ShapeDtypeStruct((t_all, RET_WIDTH), F32),
                   jax.ShapeDtypeStruct((n_batch, n_pairs, LANES, LANES), F32)],
        scratch_shapes=[pltpu.VMEM((LANES, LANES), F32)],
        input_output_aliases={3: 0},
        compiler_params=_cparams(3),
        name="ret_prompt",
    )(rq, rk, rv, rg, gn_row, lg_row)


def _dil_prompt_kernel(q_ref, k_ref, v_ref, o_ref, acc, m_s, l_s, *, seq):
    b = DIL_BLOCK
    lane = lax.broadcasted_iota(I32, (1, LANES), 1)
    mask_a = lane < HEAD_DIM
    acc[...] = jnp.zeros_like(acc)
    m_s[...] = jnp.full(m_s.shape, NEG_BIG, F32)
    l_s[...] = jnp.zeros_like(l_s)

    qi = lax.broadcasted_iota(I32, (2 * b, 1), 0) % b
    kk = lax.broadcasted_iota(I32, (1, 2 * b), 1)
    dist = qi + b - kk
    scale = HEAD_DIM ** -0.5

    for window, dil in DIL_PATTERNS:
        steps = window // dil
        band = (dist >= 0) & (dist <= steps)
        nb = seq // (b * dil)

        def body(idx, carry, dil=dil, nb=nb, band=band):
            r = idx // nb
            i = idx % nb
            rows_q = pl.ds(r + i * (b * dil), b, stride=dil)
            rows_p = pl.ds(r + jnp.maximum(i - 1, 0) * (b * dil), b, stride=dil)
            q = q_ref[rows_q, :]
            kc = jnp.concatenate([k_ref[rows_p, :], k_ref[rows_q, :]], axis=0).astype(BF16)
            vc = jnp.concatenate([v_ref[rows_p, :], v_ref[rows_q, :]], axis=0).astype(BF16)
            lhs = jnp.concatenate([jnp.where(mask_a, q, 0.0), jnp.where(mask_a, 0.0, q)],
                                  axis=0).astype(BF16)
            s = _dot_nt(lhs, kc) * scale
            valid = band & ((kk >= b) | (jnp.broadcast_to(i, kk.shape) > 0))
            s = jnp.where(valid, s, NEG_BIG)
            m_old = jnp.concatenate([m_s[0, rows_q, :], m_s[1, rows_q, :]], axis=0)
            l_old = jnp.concatenate([l_s[0, rows_q, :], l_s[1, rows_q, :]], axis=0)
            m_new = jnp.maximum(m_old, jnp.max(s, axis=-1, keepdims=True))
            alpha = jnp.exp(m_old - m_new)
            p = jnp.exp(s - m_new)
            l_new = alpha * l_old + jnp.sum(p, axis=-1, keepdims=True)
            pv = _dot(p.astype(BF16), vc)
            pv2 = jnp.where(mask_a, pv[:b], pv[b:])
            alpha2 = jnp.where(mask_a, alpha[:b], alpha[b:])
            acc[rows_q, :] = alpha2 * acc[rows_q, :] + pv2
            m_s[0, rows_q, :] = m_new[:b]
            m_s[1, rows_q, :] = m_new[b:]
            l_s[0, rows_q, :] = l_new[:b]
            l_s[1, rows_q, :] = l_new[b:]
            return carry

        lax.fori_loop(0, dil * nb, body, 0)

    o_ref[...] = acc[...] / jnp.where(mask_a, l_s[0], l_s[1])


def _dil_prompt(aq, ak, av, n_batch, seq):
    t_all = aq.shape[0]
    n_pairs = ATT_WIDTH // LANES
    blk = pl.BlockSpec((seq, LANES), lambda n, p: (n, p))
    return pl.pallas_call(
        functools.partial(_dil_prompt_kernel, seq=seq),
        grid=(n_batch, n_pairs),
        in_specs=[blk, blk, blk],
        out_specs=blk,
        out_shape=jax.ShapeDtypeStruct((t_all, ATT_WIDTH), F32),
        scratch_shapes=[pltpu.VMEM((seq, LANES), F32),
                        pltpu.VMEM((HEADS_PER_TILE, seq, 1), F32),
                        pltpu.VMEM((HEADS_PER_TILE, seq, 1), F32)],
        input_output_aliases={0: 0},
        compiler_params=_cparams(2),
        name="dil_prompt",
    )(aq, ak, av)


def _ret_sample_kernel(q_ref, k_ref, v_ref, g_ref, gn_ref, lg_ref, st_ref,
                       o_ref, nst_ref, qt, kt):
    qt[...] = q_ref[...].T
    kt[...] = k_ref[...].T
    vt = v_ref[...].T
    gt = g_ref[...].T
    lg = lg_ref[...]
    outs = []
    for hh in range(HEADS_PER_TILE):
        lo = hh * HEAD_DIM
        gdec = jnp.exp(lg[:, lo:lo + 1])
        vth = vt[lo:lo + HEAD_DIM, :]

        def body(d, o, hh=hh, lo=lo, gdec=gdec, vth=vth):
            new = gdec * st_ref[hh, d] + kt[pl.ds(lo + d, 1), :] * vth
            nst_ref[hh, d] = new
            return o + qt[pl.ds(lo + d, 1), :] * new

        o = lax.fori_loop(0, HEAD_DIM, body, jnp.zeros_like(vth))
        mu = jnp.mean(o, axis=0, keepdims=True)
        cen = o - mu
        var = jnp.mean(cen * cen, axis=0, keepdims=True)
        gate = gt[lo:lo + HEAD_DIM, :]
        outs.append(cen * lax.rsqrt(var + EPS) * gn_ref[lo:lo + HEAD_DIM, :] * (gate * _sigmoid(gate)))
    o_ref[...] = jnp.concatenate(outs, axis=0).T


def _ret_sample(rq, rk, rv, ret_n, gn_col, lg_row, state_t):
    t_all = rq.shape[0]
    n_s = state_t.shape[-1]
    n_pairs = RET_WIDTH // LANES
    last = t_all // n_s - 1
    blk = pl.BlockSpec((n_s, LANES), lambda p: (last, p))
    st_blk = pl.BlockSpec((HEADS_PER_TILE, HEAD_DIM, HEAD_DIM, n_s), lambda p: (p, 0, 0, 0))
    return pl.pallas_call(
        _ret_sample_kernel,
        grid=(n_pairs,),
        in_specs=[blk, blk, blk, blk,
                  pl.BlockSpec((LANES, 1), lambda p: (p, 0)),
                  pl.BlockSpec((1, LANES), lambda p: (0, p)),
                  st_blk],
        out_specs=[blk, st_blk],
        out_shape=[jax.ShapeDtypeStruct(ret_n.shape, F32),
                   jax.ShapeDtypeStruct(state_t.shape, F32)],
        scratch_shapes=[pltpu.VMEM((LANES, n_s), F32), pltpu.VMEM((LANES, n_s), F32)],
        input_output_aliases={3: 0},
        compiler_params=_cparams(1),
        name="ret_sample",
    )(rq, rk, rv, ret_n, gn_col, lg_row, state_t)


def _win_sample_kernel(aq_ref, akn_ref, avn_ref, kc_ref, vc_ref,
                       ko_ref, vo_ref, att_ref, qt, kt, vt, acct, *, win):
    n = pl.program_id(0)
    n_s = qt.shape[1]

    @pl.when(n == 0)
    def _():
        qt[...] = aq_ref[...].T
        kt[...] = akn_ref[...].T
        vt[...] = avn_ref[...].T
        acct[...] = jnp.zeros_like(acct)

    onehot = (lax.broadcasted_iota(I32, (n_s, LANES), 0) == n).astype(F32)
    hp = lax.Precision.HIGHEST
    qb = jnp.dot(qt[...], onehot, precision=hp, preferred_element_type=F32)
    kb = jnp.dot(kt[...], onehot, precision=hp, preferred_element_type=F32)
    vb = jnp.dot(vt[...], onehot, precision=hp, preferred_element_type=F32)

    w_pos = lax.broadcasted_iota(I32, (1, win), 1)
    back = win - w_pos
    mult = jnp.zeros((1, win), F32)
    for window, dil in DIL_PATTERNS:
        mult = mult + ((back <= window) & (back % dil == 0)).astype(F32)
    valid = mult > 0.0
    is_last = w_pos == win - 1
    reps = win // LANES
    scale = HEAD_DIM ** -0.5

    s_rows, s0_rows = [], []
    for h in range(ATT_HEADS):
        lo = h * HEAD_DIM
        k_t = kc_ref[0, h]
        qh = qb[lo:lo + HEAD_DIM, :]
        kh = kb[lo:lo + HEAD_DIM, :]
        s_rows.append(jnp.sum(k_t * jnp.concatenate([qh] * reps, axis=1), axis=0, keepdims=True))
        s0_rows.append(jnp.sum(qh * kh, axis=0, keepdims=True)[:, 0:1])
        ko_ref[0, h] = jnp.where(is_last, jnp.concatenate([kh] * reps, axis=1),
                                 pltpu.roll(k_t, win - 1, 1))
    s = jnp.concatenate(s_rows, axis=0) * scale
    s0 = jnp.concatenate(s0_rows, axis=0) * scale
    m = jnp.maximum(jnp.max(jnp.where(valid, s, NEG_BIG), axis=-1, keepdims=True), s0)
    e = jnp.where(valid, jnp.exp(s - m), 0.0) * mult
    e0 = len(DIL_PATTERNS) * jnp.exp(s0 - m)
    denom = jnp.sum(e, axis=-1, keepdims=True) + e0

    cols = []
    for h in range(ATT_HEADS):
        lo = h * HEAD_DIM
        v_t = vc_ref[0, h]
        vh = vb[lo:lo + HEAD_DIM, :]
        num = jnp.sum(v_t * e[h:h + 1, :], axis=1, keepdims=True) + e0[h:h + 1, :] * vh[:, 0:1]
        cols.append(num / denom[h:h + 1, :])
        vo_ref[0, h] = jnp.where(is_last, jnp.concatenate([vh] * reps, axis=1),
                                 pltpu.roll(v_t, win - 1, 1))
    o_col = jnp.concatenate(cols, axis=0)
    lane_n = lax.broadcasted_iota(I32, (1, n_s), 1) == n
    acct[...] = jnp.where(lane_n, o_col, acct[...])

    @pl.when(n == pl.num_programs(0) - 1)
    def _():
        att_ref[...] = acct[...].T


def _win_sample(att_o, ak, av, cache_k, cache_v):
    n_s, n_h, hd, win = cache_k.shape
    assert win >= max(w for w, _ in DIL_PATTERNS)
    t_all = att_o.shape[0]
    last = t_all // n_s - 1
    rows = pl.BlockSpec((n_s, ATT_WIDTH), lambda n: (last, 0))
    cblk = pl.BlockSpec((1, n_h, hd, win), lambda n: (n, 0, 0, 0))
    cshape = jax.ShapeDtypeStruct(cache_k.shape, F32)
    return pl.pallas_call(
        functools.partial(_win_sample_kernel, win=win),
        grid=(n_s,),
        in_specs=[rows, rows, rows, cblk, cblk],
        out_specs=[cblk, cblk, rows],
        out_shape=[cshape, cshape, jax.ShapeDtypeStruct(att_o.shape, F32)],
        scratch_shapes=[pltpu.VMEM((ATT_WIDTH, n_s), F32)] * 4,
        input_output_aliases={0: 2},
        compiler_params=_cparams(1),
        name="win_sample",
    )(att_o, ak, av, cache_k, cache_v)


def _lane_tile_norm(t, g, n_tiles):
    outs = []
    for h in range(n_tiles):
        outs.append(_rms(t[:, h * LANES:(h + 1) * LANES], g))
    return jnp.concatenate(outs, axis=-1)


def _mem_kv_kernel(mem_ref, gm_ref, w_ref, gk_ref, k_ref, v_ref):
    xn = _rms(mem_ref[0], gm_ref[...]).astype(BF16)
    kv = _dot(xn, w_ref[...])
    k_ref[0] = _lane_tile_norm(kv[:, :X_WIDTH], gk_ref[...], X_HEADS)
    v_ref[0] = kv[:, X_WIDTH:]


def _mem_kv(mem, g_mem, w_kv, g_xk):
    n, m, _ = mem.shape
    out = jax.ShapeDtypeStruct((n, m, X_WIDTH), F32)
    return pl.pallas_call(
        _mem_kv_kernel,
        grid=(n,),
        in_specs=[pl.BlockSpec((1, m, D_MODEL), lambda i: (i, 0, 0)),
                  pl.BlockSpec((1, D_MODEL), lambda i: (0, 0)),
                  pl.BlockSpec((D_MODEL, 2 * X_WIDTH), lambda i: (0, 0)),
                  pl.BlockSpec((1, X_HEAD_DIM), lambda i: (0, 0))],
        out_specs=[pl.BlockSpec((1, m, X_WIDTH), lambda i: (i, 0, 0))] * 2,
        out_shape=[out, out],
        compiler_params=_cparams(1),
        name="mem_kv",
    )(mem, g_mem, w_kv, g_xk)


def _mix_out_kernel(ret_ref, att_ref, x_ref, wo_ref, g2_ref, wq_ref, gq_ref, h_ref, q_ref):
    mixed = jnp.concatenate([ret_ref[...], att_ref[...]], axis=-1).astype(BF16)
    h = x_ref[...] + _dot(mixed, wo_ref[...])
    h_ref[...] = h
    q = _dot(_rms(h, g2_ref[...]).astype(BF16), wq_ref[...])
    q_ref[...] = _lane_tile_norm(q, gq_ref[...], X_HEADS)


def _mix_out(ret_n, att_o, x_all, w_out, g2, w_qx, g_xq):
    t_all = x_all.shape[0]
    tm = TOKEN_TILE
    row = lambda i: (i, 0)
    fixed = lambda i: (0, 0)
    return pl.pallas_call(
        _mix_out_kernel,
        grid=(t_all // tm,),
        in_specs=[pl.BlockSpec((tm, RET_WIDTH), row),
                  pl.BlockSpec((tm, ATT_WIDTH), row),
                  pl.BlockSpec((tm, D_MODEL), row),
                  pl.BlockSpec((D_MODEL, D_MODEL), fixed),
                  pl.BlockSpec((1, D_MODEL), fixed),
                  pl.BlockSpec((D_MODEL, X_WIDTH), fixed),
                  pl.BlockSpec((1, X_HEAD_DIM), fixed)],
        out_specs=[pl.BlockSpec((tm, D_MODEL), row), pl.BlockSpec((tm, X_WIDTH), row)],
        out_shape=[jax.ShapeDtypeStruct((t_all, D_MODEL), F32),
                   jax.ShapeDtypeStruct((t_all, X_WIDTH), F32)],
        compiler_params=_cparams(1),
        name="mix_out",
    )(ret_n, att_o, x_all, w_out, g2, w_qx, g_xq)


def _xattn_prompt_kernel(q_ref, k_ref, v_ref, o_ref):
    q = q_ref[...]
    k = k_ref[0]
    v = v_ref[0]
    scale = X_HEAD_DIM ** -0.5
    outs = []
    for h in range(X_HEADS):
        sl = slice(h * LANES, (h + 1) * LANES)
        s = _dot_nt(q[:, sl].astype(BF16), k[:, sl].astype(BF16)) * scale
        p = jnp.exp(s - jnp.max(s, axis=-1, keepdims=True))
        o = _dot(p.astype(BF16), v[:, sl].astype(BF16))
        outs.append(o / jnp.sum(p, axis=-1, keepdims=True))
    o_ref[...] = jnp.concatenate(outs, axis=-1)


def _xattn_prompt(qx, mk, mv, n_batch, seq, tq=512):
    t_all = qx.shape[0]
    per = seq // tq
    m = mk.shape[1]
    rows = pl.BlockSpec((tq, X_WIDTH), lambda n, i: (n * per + i, 0))
    mem = pl.BlockSpec((1, m, X_WIDTH), lambda n, i: (n, 0, 0))
    return pl.pallas_call(
        _xattn_prompt_kernel,
        grid=(n_batch, per),
        in_specs=[rows, mem, mem],
        out_specs=rows,
        out_shape=jax.ShapeDtypeStruct((t_all, X_WIDTH), F32),
        input_output_aliases={0: 0},
        compiler_params=_cparams(2),
        name="xattn_prompt",
    )(qx, mk, mv)


def _xattn_sample_kernel(q_ref, k_ref, v_ref, o_ref):
    n = pl.program_id(0)
    q = q_ref[pl.ds(n, 1), :]
    k = k_ref[0]
    v = v_ref[0]
    scale = X_HEAD_DIM ** -0.5
    outs = []
    for h in range(X_HEADS):
        sl = slice(h * LANES, (h + 1) * LANES)
        s = jnp.sum(k[:, sl] * q[:, sl], axis=-1, keepdims=True) * scale
        p = jnp.exp(s - jnp.max(s, axis=0, keepdims=True))
        o = jnp.sum(p * v[:, sl], axis=0, keepdims=True)
        outs.append(o / jnp.sum(p, axis=0, keepdims=True))
    o_ref[pl.ds(n, 1), :] = jnp.concatenate(outs, axis=-1)


def _xattn_sample(o_all, mk, mv):
    n_s, m, _ = mk.shape
    t_all = o_all.shape[0]
    last = t_all // n_s - 1
    rows = pl.BlockSpec((n_s, X_WIDTH), lambda n: (last, 0))
    mem = pl.BlockSpec((1, m, X_WIDTH), lambda n: (n, 0, 0))
    return pl.pallas_call(
        _xattn_sample_kernel,
        grid=(n_s,),
        in_specs=[rows, mem, mem],
        out_specs=rows,
        out_shape=jax.ShapeDtypeStruct(o_all.shape, F32),
        input_output_aliases={0: 0},
        compiler_params=_cparams(1),
        name="xattn_sample",
    )(o_all, mk, mv)


def _xout_router_kernel(o_ref, h_ref, wo_ref, g3_ref, wr_ref, br_ref,
                        h2_ref, xn_ref, idx_ref, gate_ref):
    h2 = h_ref[...] + _dot(o_ref[...].astype(BF16), wo_ref[...])
    h2_ref[...] = h2
    xn = _rms(h2, g3_ref[...])
    xn_ref[...] = xn
    logits = lax.dot_general(wr_ref[...], xn, (((1,), (1,)), ((), ())),
                             precision=lax.Precision.HIGHEST,
                             preferred_element_type=F32) + br_ref[...]
    eid = lax.broadcasted_iota(I32, logits.shape, 0)
    work = logits
    vals, idxs = [], []
    for _ in range(TOP_K):
        mx = jnp.max(work, axis=0, keepdims=True)
        ix = jnp.min(jnp.where(work == mx, eid, N_EXPERTS), axis=0, keepdims=True)
        vals.append(mx)
        idxs.append(ix)
        work = jnp.where(eid == ix, -jnp.inf, work)
    ex = [jnp.exp(v - vals[0]) for v in vals]
    tot = ex[0] + ex[1] + ex[2] + ex[3]
    idx_ref[...] = jnp.concatenate(idxs, axis=0)
    gate_ref[...] = jnp.concatenate([e / tot for e in ex], axis=0)


def _xout_router(o_all, h_all, w_ox, g3, w_rt, b_r):
    t_all = h_all.shape[0]
    tm = TOKEN_TILE
    row = lambda i: (i, 0)
    fixed = lambda i: (0, 0)
    colblk = lambda i: (0, i)
    return pl.pallas_call(
        _xout_router_kernel,
        grid=(t_all // tm,),
        in_specs=[pl.BlockSpec((tm, X_WIDTH), row),
                  pl.BlockSpec((tm, D_MODEL), row),
                  pl.BlockSpec((X_WIDTH, D_MODEL), fixed),
                  pl.BlockSpec((1, D_MODEL), fixed),
                  pl.BlockSpec((N_EXPERTS, D_MODEL), fixed),
                  pl.BlockSpec((N_EXPERTS, 1), fixed)],
        out_specs=[pl.BlockSpec((tm, D_MODEL), row), pl.BlockSpec((tm, D_MODEL), row),
                   pl.BlockSpec((TOP_K, tm), colblk), pl.BlockSpec((TOP_K, tm), colblk)],
        out_shape=[jax.ShapeDtypeStruct((t_all, D_MODEL), F32),
                   jax.ShapeDtypeStruct((t_all, D_MODEL), F32),
                   jax.ShapeDtypeStruct((TOP_K, t_all), I32),
                   jax.ShapeDtypeStruct((TOP_K, t_all), F32)],
        compiler_params=_cparams(1),
        name="xout_router",
    )(o_all, h_all, w_ox, g3, w_rt, b_r)


def _route_kernel(idx_ref, dest_ref, be_ref, nu_ref, *, t_all, n_blocks_pad):
    bm = MOE_BLOCK_ROWS
    nt = t_all // LANES
    e_col = lax.broadcasted_iota(I32, (N_EXPERTS, 1), 0)
    hp = lax.Precision.HIGHEST

    def multi_hot(j):
        blk = idx_ref[:, pl.ds(pl.multiple_of(j * LANES, LANES), LANES)]
        mh = jnp.zeros((N_EXPERTS, LANES), F32)
        for k in range(TOP_K):
            mh = mh + (e_col == blk[k:k + 1, :]).astype(F32)
        return blk, mh

    def count_body(j, c):
        _, mh = multi_hot(j)
        return c + jnp.sum(mh, axis=1, keepdims=True)

    counts = lax.fori_loop(0, nt, count_body, jnp.zeros((N_EXPERTS, 1), F32))
    padded = jnp.ceil(counts * (1.0 / bm)) * bm
    tri = (lax.broadcasted_iota(I32, (N_EXPERTS, N_EXPERTS), 1)
           <= lax.broadcasted_iota(I32, (N_EXPERTS, N_EXPERTS), 0)).astype(F32)
    pad_end = jnp.dot(tri, jnp.broadcast_to(padded, (N_EXPERTS, LANES)), precision=hp,
                      preferred_element_type=F32)
    pad_start = pad_end[:, 0:1] - padded
    upper = (lax.broadcasted_iota(I32, (LANES, LANES), 0)
             < lax.broadcasted_iota(I32, (LANES, LANES), 1)).astype(BF16)

    def dest_body(j, carry):
        blk, mh = multi_hot(j)
        rank = carry + _dot(mh.astype(BF16), upper)
        base = pad_start + rank
        for k in range(TOP_K):
            d = jnp.sum(jnp.where(e_col == blk[k:k + 1, :], base, 0.0), axis=0, keepdims=True)
            dest_ref[pl.ds(k, 1), pl.ds(pl.multiple_of(j * LANES, LANES), LANES)] = d.astype(I32)
        return carry + jnp.sum(mh, axis=1, keepdims=True)

    lax.fori_loop(0, nt, dest_body, jnp.zeros((N_EXPERTS, 1), F32))

    b_row = lax.broadcasted_iota(I32, (1, n_blocks_pad), 1).astype(F32) * bm
    be = jnp.sum((pad_end[:, 0:1] <= b_row).astype(F32), axis=0, keepdims=True)
    be_ref[...] = jnp.minimum(be, N_EXPERTS - 1.0).astype(I32)
    nu_ref[...] = (pad_end[N_EXPERTS - 1:N_EXPERTS, :] * (1.0 / bm)).astype(I32)


def _route(idx_t, n_blocks_pad):
    t_all = idx_t.shape[1]
    return pl.pallas_call(
        functools.partial(_route_kernel, t_all=t_all, n_blocks_pad=n_blocks_pad),
        out_shape=[jax.ShapeDtypeStruct((TOP_K, t_all), I32),
                   jax.ShapeDtypeStruct((1, n_blocks_pad), I32),
                   jax.ShapeDtypeStruct((1, LANES), I32)],
        compiler_params=pltpu.CompilerParams(vmem_limit_bytes=VMEM_LIMIT),
        name="moe_route",
    )(idx_t)


def _dispatch_kernel(dest_sm, x_ref, xs_in, xs_ref, sem, *, t_all):
    del xs_in
    i = pl.program_id(0)
    tm = x_ref.shape[0]

    def row_copy(j, d):
        return pltpu.make_async_copy(x_ref.at[pl.ds(j, 1)], xs_ref.at[pl.ds(d, 1)], sem)

    def start_body(j, c):
        for k in range(TOP_K):
            row_copy(j, dest_sm[k * t_all + i * tm + j]).start()
        return c

    lax.fori_loop(0, tm, start_body, 0)

    def wait_body(j, c):
        for k in range(TOP_K):
            row_copy(0, 0).wait()
        return c

    lax.fori_loop(0, tm, wait_body, 0)


def _dispatch(dest_flat, xn_all, xs_init):
    t_all = xn_all.shape[0]
    tm = ROW_TILE
    return pl.pallas_call(
        functools.partial(_dispatch_kernel, t_all=t_all),
        grid_spec=pltpu.PrefetchScalarGridSpec(
            num_scalar_prefetch=1,
            grid=(t_all // tm,),
            in_specs=[pl.BlockSpec((tm, D_MODEL), lambda i, d: (i, 0)),
                      pl.BlockSpec(memory_space=pl.ANY)],
            out_specs=pl.BlockSpec(memory_space=pl.ANY),
            scratch_shapes=[pltpu.SemaphoreType.DMA]),
        out_shape=jax.ShapeDtypeStruct(xs_init.shape, F32),
        input_output_aliases={2: 0},
        compiler_params=_cparams(1),
        name="moe_dispatch",
    )(dest_flat, xn_all, xs_init)


def _expert_kernel(be_sm, nu_sm, x_ref, wgu_ref, bgu_ref, wd_ref, bd_ref, y_ref, wgu_bf, wd_bf):
    b = pl.program_id(0)
    changed = jnp.logical_or(b == 0, be_sm[b] != be_sm[jnp.maximum(b - 1, 0)])

    @pl.when(changed)
    def _():
        wgu_bf[...] = wgu_ref[0].astype(BF16)
        wd_bf[...] = wd_ref[0].astype(BF16)

    @pl.when(b < nu_sm[0])
    def _():
        h = _dot(x_ref[...].astype(BF16), wgu_bf[...]) + bgu_ref[0]
        glu = jnp.minimum(h[:, :D_FF], SWIGLU_LIMIT)
        lin = jnp.clip(h[:, D_FF:], -SWIGLU_LIMIT, SWIGLU_LIMIT)
        act = glu * _sigmoid(SWIGLU_ALPHA * glu) * (lin + 1.0)
        y_ref[...] = _dot(act.astype(BF16), wd_bf[...]) + bd_ref[0]

    @pl.when(b >= nu_sm[0])
    def _():
        y_ref[...] = jnp.zeros_like(y_ref)


def _experts(block_e, n_used, xs, w_gu, b_gu, w_down, b_down):
    cap = xs.shape[0]
    bm = MOE_BLOCK_ROWS
    return pl.pallas_call(
        _expert_kernel,
        grid_spec=pltpu.PrefetchScalarGridSpec(
            num_scalar_prefetch=2,
            grid=(cap // bm,),
            in_specs=[pl.BlockSpec((bm, D_MODEL), lambda b, be, nu: (b, 0)),
                      pl.BlockSpec((1, D_MODEL, 2 * D_FF), lambda b, be, nu: (be[b], 0, 0)),
                      pl.BlockSpec((1, 1, 2 * D_FF), lambda b, be, nu: (be[b], 0, 0)),
                      pl.BlockSpec((1, D_FF, D_MODEL), lambda b, be, nu: (be[b], 0, 0)),
                      pl.BlockSpec((1, 1, D_MODEL), lambda b, be, nu: (be[b], 0, 0))],
            out_specs=pl.BlockSpec((bm, D_MODEL), lambda b, be, nu: (b, 0)),
            scratch_shapes=[pltpu.VMEM((D_MODEL, 2 * D_FF), BF16),
                            pltpu.VMEM((D_FF, D_MODEL), BF16)]),
        out_shape=jax.ShapeDtypeStruct((cap, D_MODEL), F32),
        compiler_params=_cparams(1),
        name="moe_experts",
    )(block_e, n_used, xs, w_gu, b_gu, w_down, b_down)


def _combine_kernel(dest_sm, yb_ref, h_ref, g_ref, op_ref, os_ref, buf, sem, *, t_all):
    i = pl.program_id(0)
    tm = h_ref.shape[0]

    def row_copy(k, j, d):
        return pltpu.make_async_copy(yb_ref.at[pl.ds(d, 1)], buf.at[k, pl.ds(j, 1)], sem)

    def start_body(j, c):
        for k in range(TOP_K):
            row_copy(k, j, dest_sm[k * t_all + i * tm + j]).start()
        return c

    lax.fori_loop(0, tm, start_body, 0)

    def wait_body(j, c):
        for k in range(TOP_K):
            row_copy(0, 0, 0).wait()
        return c

    lax.fori_loop(0, tm, wait_body, 0)

    g = g_ref[...]
    y = h_ref[...]
    for k in range(TOP_K):
        y = y + g[:, k:k + 1] * buf[k]
    last = pl.num_programs(0) - 1

    @pl.when(i < last)
    def _():
        op_ref[...] = y

    @pl.when(i == last)
    def _():
        os_ref[...] = y


def _combine(dest_flat, yb, h2_all, gates, n_prompt_rows):
    t_all = h2_all.shape[0]
    tm = ROW_TILE
    n_p_tiles = n_prompt_rows // tm
    assert t_all == n_prompt_rows + tm
    return pl.pallas_call(
        functools.partial(_combine_kernel, t_all=t_all),
        grid_spec=pltpu.PrefetchScalarGridSpec(
            num_scalar_prefetch=1,
            grid=(t_all // tm,),
            in_specs=[pl.BlockSpec(memory_space=pl.ANY),
                      pl.BlockSpec((tm, D_MODEL), lambda i, d: (i, 0)),
                      pl.BlockSpec((tm, TOP_K), lambda i, d: (i, 0))],
            out_specs=[pl.BlockSpec((tm, D_MODEL), lambda i, d: (jnp.minimum(i, n_p_tiles - 1), 0)),
                       pl.BlockSpec((tm, D_MODEL), lambda i, d: (0, 0))],
            scratch_shapes=[pltpu.VMEM((TOP_K, tm, D_MODEL), F32), pltpu.SemaphoreType.DMA]),
        out_shape=[jax.ShapeDtypeStruct((n_prompt_rows, D_MODEL), F32),
                   jax.ShapeDtypeStruct((tm, D_MODEL), F32)],
        compiler_params=_cparams(1),
        name="moe_combine",
    )(dest_flat, yb, h2_all, gates)


def _rope_tables(pos):
    half = HEAD_DIM // 2
    inv = jnp.exp(-math.log(ROPE_THETA) * jnp.arange(half, dtype=F32) / half)
    ang = pos.astype(F32)[:, None] * inv[None, :]
    cos, sin = jnp.cos(ang), jnp.sin(ang)
    cos_t = jnp.concatenate([cos, cos] * HEADS_PER_TILE, axis=-1)
    sin_t = jnp.concatenate([-sin, sin] * HEADS_PER_TILE, axis=-1)
    return cos_t, sin_t


def _block_diag_ones(n, blk):
    r = jnp.arange(n) // blk
    return (r[:, None] == r[None, :]).astype(BF16)


def _layer(x_prompt, x_sample, state_ret, cache_win_k, cache_win_v, cache_mem_k, cache_mem_v,
           mem_prompt, g_norm1, w_in, g_att_q, g_att_k, g_ret_gn, w_out, g_norm2, g_mem,
           w_q_x, w_kv_x, g_x_q, g_x_k, w_o_x, g_norm3, w_router, b_router, w_gu, b_gu,
           w_down, b_down):
    n_b, seq, _ = x_prompt.shape
    n_s = x_sample.shape[0]
    assert x_sample.shape[1] == 1 and n_s == ROW_TILE
    t_p = n_b * seq
    t_all = t_p + n_s
    assert t_all % TOKEN_TILE == 0 and t_p % ROW_TILE == 0

    x_all = jnp.concatenate([x_prompt.reshape(t_p, D_MODEL), x_sample.reshape(n_s, D_MODEL)], axis=0)
    pos = jnp.concatenate([jnp.tile(jnp.arange(seq, dtype=jnp.int32), n_b),
                           jnp.full((n_s,), PAST_LEN, jnp.int32)])
    cos_t, sin_t = _rope_tables(pos)
    log_g = jnp.log1p(-jnp.exp2(-5.0 - jnp.arange(RET_HEADS, dtype=F32)))
    lg_row = jnp.repeat(log_g, HEAD_DIM)[None, :]
    gn_row = g_ret_gn.reshape(1, RET_WIDTH)
    gn_col = g_ret_gn.reshape(RET_WIDTH, 1)
    gq = jnp.tile(g_att_q.reshape(1, HEAD_DIM), (1, ATT_HEADS))
    gk = jnp.tile(g_att_k.reshape(1, HEAD_DIM), (1, ATT_HEADS))
    seg = _block_diag_ones(ATT_WIDTH, HEAD_DIM)

    rq, rk, rv, rg, aq, ak, av = _in_proj(
        x_all, g_norm1.reshape(1, D_MODEL), w_in.astype(BF16), cos_t, sin_t, gq, gk, seg)

    ret_n, st_p = _ret_prompt(rq, rk, rv, rg, gn_row, lg_row, n_b, seq)
    att_o = _dil_prompt(aq, ak, av, n_b, seq)
    state_t = jnp.transpose(state_ret, (1, 2, 3, 0))
    ret_n, st_s = _ret_sample(rq, rk, rv, ret_n, gn_col, lg_row, state_t)
    ck = jnp.transpose(cache_win_k, (0, 2, 3, 1))
    cv = jnp.transpose(cache_win_v, (0, 2, 3, 1))
    wk_s, wv_s, att_o = _win_sample(att_o, ak, av, ck, cv)

    h_all, qx = _mix_out(ret_n, att_o, x_all, w_out.astype(BF16), g_norm2.reshape(1, D_MODEL),
                         w_q_x.astype(BF16), g_x_q.reshape(1, X_HEAD_DIM))

    mk_p, mv_p = _mem_kv(mem_prompt, g_mem.reshape(1, D_MODEL), w_kv_x.astype(BF16),
                         g_x_k.reshape(1, X_HEAD_DIM))
    o_all = _xattn_prompt(qx, mk_p, mv_p, n_b, seq)
    n_mem = cache_mem_k.shape[1]
    o_all = _xattn_sample(o_all, cache_mem_k.reshape(n_s, n_mem, X_WIDTH),
                          cache_mem_v.reshape(n_s, n_mem, X_WIDTH))

    h2_all, xn_all, idx_t, gate_t = _xout_router(
        o_all, h_all, w_o_x.astype(BF16), g_norm3.reshape(1, D_MODEL),
        jnp.transpose(w_router), b_router.reshape(N_EXPERTS, 1))

    bm = MOE_BLOCK_ROWS
    n_blocks = -(-(t_all * TOP_K) // bm) + N_EXPERTS
    n_blocks_pad = -(-n_blocks // LANES) * LANES
    dest_t, be, nu = _route(idx_t, n_blocks_pad)
    dest_flat = dest_t.reshape(TOP_K * t_all)
    xs = _dispatch(dest_flat, xn_all, jnp.zeros((n_blocks * bm, D_MODEL), F32))
    yb = _experts(be.reshape(n_blocks_pad), nu.reshape(LANES), xs, w_gu,
                  b_gu.reshape(N_EXPERTS, 1, 2 * D_FF), w_down, b_down.reshape(N_EXPERTS, 1, D_MODEL))
    y_p, y_s = _combine(dest_flat, yb, h2_all, jnp.transpose(gate_t), t_p)

    wp = min(MAX_WINDOW, seq)
    ak_p = ak[:t_p].reshape(n_b, seq, ATT_HEADS, HEAD_DIM)[:, seq - wp:]
    av_p = av[:t_p].reshape(n_b, seq, ATT_HEADS, HEAD_DIM)[:, seq - wp:]
    st_p = jnp.stack([st_p[:, :, :HEAD_DIM, :HEAD_DIM], st_p[:, :, HEAD_DIM:, HEAD_DIM:]], axis=2)
    st_p = st_p.reshape(n_b, RET_HEADS, HEAD_DIM, HEAD_DIM)
    return (y_p.reshape(n_b, seq, D_MODEL),
            y_s.reshape(n_s, 1, D_MODEL),
            st_p,
            jnp.transpose(st_s, (3, 0, 1, 2)),
            ak_p, av_p,
            jnp.transpose(wk_s, (0, 3, 1, 2)),
            jnp.transpose(wv_s, (0, 3, 1, 2)),
            mk_p.reshape(n_b, n_mem, X_HEADS, X_HEAD_DIM),
            mv_p.reshape(n_b, n_mem, X_HEADS, X_HEAD_DIM))


def kernel(x_prompt, x_sample, state_ret, cache_win_k, cache_win_v, cache_mem_k, cache_mem_v,
           mem_prompt, g_norm1, w_in, g_att_q, g_att_k, g_ret_gn, w_out, g_norm2, g_mem,
           w_q_x, w_kv_x, g_x_q, g_x_k, w_o_x, g_norm3, w_router, b_router, w_gu, b_gu,
           w_down, b_down):
    assert state_ret.shape[0] == 1, "single-layer trunk"
    outs = _layer(x_prompt, x_sample, state_ret[0], cache_win_k[0], cache_win_v[0],
                  cache_mem_k[0], cache_mem_v[0], mem_prompt, g_norm1[0], w_in[0], g_att_q[0],
                  g_att_k[0], g_ret_gn[0], w_out[0], g_norm2[0], g_mem[0], w_q_x[0], w_kv_x[0],
                  g_x_q[0], g_x_k[0], w_o_x[0], g_norm3[0], w_router[0], b_router[0], w_gu[0],
                  b_gu[0], w_down[0], b_down[0])
    y_p, y_s = outs[0], outs[1]
    return (y_p, y_s) + tuple(o[None] for o in outs[2:])
```

```python
import functools
import math

import jax
import jax.numpy as jnp
from jax import lax
from jax.experimental import pallas as pl
from jax.experimental.pallas import tpu as pltpu

F32 = jnp.float32
BF16 = jnp.bfloat16
I32 = jnp.int32

D_MODEL = 1024
HEAD_DIM = 64
RET_HEADS = 8
ATT_HEADS = 8
RET_WIDTH = RET_HEADS * HEAD_DIM
ATT_WIDTH = ATT_HEADS * HEAD_DIM
IN_COLS = 4 * RET_WIDTH + 3 * ATT_WIDTH
RET_CHUNK = 128
DIL_PATTERNS = ((128, 1), (512, 4), (2048, 16))
DIL_BLOCK = 128
MAX_WINDOW = 2048
X_HEADS = 4
X_HEAD_DIM = 128
X_WIDTH = X_HEADS * X_HEAD_DIM
N_EXPERTS = 32
TOP_K = 4
D_FF = D_MODEL
SWIGLU_LIMIT = 7.0
SWIGLU_ALPHA = 1.702
ROPE_THETA = 10000.0
EPS = 1e-6
PAST_LEN = 8192

LANES = 128
SUBLANES = 8
HEADS_PER_TILE = LANES // HEAD_DIM

TOKEN_TILE = 384
ROW_TILE = 128
MOE_BLOCK_ROWS = 256
NEG_BIG = -1e30
VMEM_LIMIT = 48 * 1024 * 1024


def _cparams(n_axes):
    return pltpu.CompilerParams(
        dimension_semantics=("arbitrary",) * n_axes, vmem_limit_bytes=VMEM_LIMIT)


def _rms(x, g):
    return x * lax.rsqrt(jnp.mean(x * x, axis=-1, keepdims=True) + EPS) * g


def _dot(a, b):
    return jnp.dot(a, b, preferred_element_type=F32)


def _dot_nt(a, b):
    return lax.dot_general(a, b, (((1,), (1,)), ((), ())), preferred_element_type=F32)


def _dot_tn(a, b):
    return lax.dot_general(a, b, (((0,), (0,)), ((), ())), preferred_element_type=F32)


def _sigmoid(x):
    return 1.0 / (1.0 + jnp.exp(-x))


def _in_proj_kernel(x_ref, g1_ref, w_ref, cos_ref, sin_ref, gq_ref, gk_ref, seg_ref,
                    rq_ref, rk_ref, rv_ref, rg_ref, aq_ref, ak_ref, av_ref):
    xn = _rms(x_ref[...], g1_ref[...]).astype(BF16)
    proj = _dot(xn, w_ref[...])
    reps = RET_WIDTH // LANES
    cos = jnp.concatenate([cos_ref[...]] * reps, axis=-1)
    sin = jnp.concatenate([sin_ref[...]] * reps, axis=-1)
    lane = lax.broadcasted_iota(I32, (1, RET_WIDTH), 1)
    first_half = (lane % HEAD_DIM) < (HEAD_DIM // 2)
    seg = seg_ref[...]

    def rope(t):
        partner = jnp.where(first_half,
                            pltpu.roll(t, RET_WIDTH - HEAD_DIM // 2, 1),
                            pltpu.roll(t, HEAD_DIM // 2, 1))
        return t * cos + partner * sin

    def head_norm(t, g):
        sq = t * t
        hi = sq.astype(BF16)
        lo = (sq - hi.astype(F32)).astype(BF16)
        ssum = _dot(hi, seg) + _dot(lo, seg)
        return t * lax.rsqrt(ssum * (1.0 / HEAD_DIM) + EPS) * g

    w = RET_WIDTH
    rq_ref[...] = rope(proj[:, 0:w])
    rk_ref[...] = rope(proj[:, w:2 * w]) * (HEAD_DIM ** -0.5)
    rv_ref[...] = proj[:, 2 * w:3 * w]
    rg_ref[...] = proj[:, 3 * w:4 * w]
    aq_ref[...] = rope(head_norm(proj[:, 4 * w:5 * w], gq_ref[...]))
    ak_ref[...] = rope(head_norm(proj[:, 5 * w:6 * w], gk_ref[...]))
    av_ref[...] = proj[:, 6 * w:7 * w]


def _in_proj(x_all, g1, w_in, cos_t, sin_t, gq, gk, seg):
    t_all = x_all.shape[0]
    tm = TOKEN_TILE
    row = lambda i: (i, 0)
    fixed = lambda i: (0, 0)
    out = jax.ShapeDtypeStruct((t_all, RET_WIDTH), F32)
    return pl.pallas_call(
        _in_proj_kernel,
        grid=(t_all // tm,),
        in_specs=[pl.BlockSpec((tm, D_MODEL), row),
                  pl.BlockSpec((1, D_MODEL), fixed),
                  pl.BlockSpec((D_MODEL, IN_COLS), fixed),
                  pl.BlockSpec((tm, LANES), row),
                  pl.BlockSpec((tm, LANES), row),
                  pl.BlockSpec((1, ATT_WIDTH), fixed),
                  pl.BlockSpec((1, ATT_WIDTH), fixed),
                  pl.BlockSpec((ATT_WIDTH, ATT_WIDTH), fixed)],
        out_specs=[pl.BlockSpec((tm, RET_WIDTH), row)] * 7,
        out_shape=[out] * 7,
        compiler_params=_cparams(1),
        name="in_proj",
    )(x_all, g1, w_in, cos_t, sin_t, gq, gk, seg)


def _group_norm_gate(o, mask_a, gn, gate):
    inv = 1.0 / HEAD_DIM
    sa = jnp.sum(jnp.where(mask_a, o, 0.0), axis=-1, keepdims=True)
    sb = jnp.sum(jnp.where(mask_a, 0.0, o), axis=-1, keepdims=True)
    cen = o - jnp.where(mask_a, sa, sb) * inv
    c2 = cen * cen
    va = jnp.sum(jnp.where(mask_a, c2, 0.0), axis=-1, keepdims=True)
    vb = jnp.sum(jnp.where(mask_a, 0.0, c2), axis=-1, keepdims=True)
    var = jnp.where(mask_a, va, vb) * inv
    return cen * lax.rsqrt(var + EPS) * gn * (gate * _sigmoid(gate))


def _ret_prompt_kernel(q_ref, k_ref, v_ref, g_ref, gn_ref, lg_ref, o_ref, st_ref,
                       state, dec, qdec, kdec):
    n = pl.program_id(0)
    c = pl.program_id(1)
    ch = RET_CHUNK
    n_pairs = RET_WIDTH // LANES
    lane = lax.broadcasted_iota(I32, (1, LANES), 1)
    mask_a = lane < HEAD_DIM
    row_i = lax.broadcasted_iota(I32, (ch, 1), 0)

    @pl.when(jnp.logical_and(n == 0, c == 0))
    def _():
        row = row_i.astype(F32)
        col = lax.broadcasted_iota(I32, (1, ch), 1).astype(F32)
        diff = row - col
        causal = diff >= 0.0
        dpos = jnp.maximum(diff, 0.0)
        lg = lg_ref[...]
        for h in range(RET_HEADS):
            dec[h] = jnp.where(causal, jnp.exp(dpos * lg[:, h * HEAD_DIM:h * HEAD_DIM + 1]), 0.0)
        qdec[...] = jnp.exp((row + 1.0) * lg)
        kdec[...] = jnp.exp((ch - 1.0 - row) * lg)

    @pl.when(c == 0)
    def _():
        state[...] = jnp.zeros_like(state)

    same_head = (row_i // HEAD_DIM) == (lane // HEAD_DIM)
    cdec = jnp.exp(ch * lg_ref[...])
    for p in range(n_pairs):
        sl = slice(p * LANES, (p + 1) * LANES)
        q = q_ref[:, sl]
        k = k_ref[:, sl]
        kb = k.astype(BF16)
        vb = v_ref[:, sl].astype(BF16)
        lhs = jnp.concatenate([jnp.where(mask_a, q, 0.0), jnp.where(mask_a, 0.0, q)],
                              axis=0).astype(BF16)
        s = _dot_nt(lhs, kb)
        pr = (s * jnp.concatenate([dec[2 * p], dec[2 * p + 1]], axis=0)).astype(BF16)
        o2 = _dot(pr, vb)
        o_intra = jnp.where(mask_a, o2[:ch], o2[ch:])
        st = state[p]
        o = o_intra + _dot(q.astype(BF16), st.astype(BF16)) * qdec[:, sl]
        upd = _dot_tn((k * kdec[:, sl]).astype(BF16), vb)
        state[p] = cdec[:, sl] * st + jnp.where(same_head, upd, 0.0)
        o_ref[:, sl] = _group_norm_gate(o, mask_a, gn_ref[:, sl], g_ref[:, sl])

    @pl.when(c == pl.num_programs(1) - 1)
    def _():
        st_ref[0] = state[...]


def _ret_prompt(rq, rk, rv, rg, gn_row, lg_row, n_batch, seq):
    t_all = rq.shape[0]
    n_pairs = RET_WIDTH // LANES
    n_chunks = seq // RET_CHUNK
    blk = pl.BlockSpec((RET_CHUNK, RET_WIDTH), lambda n, c: (n * n_chunks + c, 0))
    lane_row = pl.BlockSpec((1, RET_WIDTH), lambda n, c: (0, 0))
    return pl.pallas_call(
        _ret_prompt_kernel,
        grid=(n_batch, n_chunks),
        in_specs=[blk, blk, blk, blk, lane_row, lane_row],
        out_specs=[blk, pl.BlockSpec((1, n_pairs, LANES, LANES), lambda n, c: (n, 0, 0, 0))],
        out_shape=[jax.ShapeDtypeStruct((t_all, RET_WIDTH), F32),
                   jax.ShapeDtypeStruct((n_batch, n_pairs, LANES, LANES), F32)],
        scratch_shapes=[pltpu.VMEM((n_pairs, LANES, LANES), F32),
                        pltpu.VMEM((RET_HEADS, RET_CHUNK, RET_CHUNK), F32),
                        pltpu.VMEM((RET_CHUNK, RET_WIDTH), F32),
                        pltpu.VMEM((RET_CHUNK, RET_WIDTH), F32)],
        input_output_aliases={3: 0},
        compiler_params=_cparams(2),
        name="ret_prompt",
    )(rq, rk, rv, rg, gn_row, lg_row)


DIL_UNROLL = 2


def _dil_prompt_kernel(q_ref, k_ref, v_ref, o_ref, wk_ref, wv_ref, acc, m_s, l_s, *, seq, wp):
    b = DIL_BLOCK
    lane = lax.broadcasted_iota(I32, (1, LANES), 1)
    mask_a = lane < HEAD_DIM
    qi = lax.broadcasted_iota(I32, (2 * b, 1), 0) % b
    kk = lax.broadcasted_iota(I32, (1, 2 * b), 1)
    dist = qi + b - kk
    scale = HEAD_DIM ** -0.5

    for first, (window, dil) in zip((True, False, False), DIL_PATTERNS):
        steps = window // dil
        band = (dist >= 0) & (dist <= steps)
        nb = seq // (b * dil)

        def block(idx, dil=dil, nb=nb, band=band):
            r = idx // nb
            i = idx % nb
            rows_q = pl.ds(r + i * (b * dil), b, stride=dil)
            rows_p = pl.ds(r + jnp.maximum(i - 1, 0) * (b * dil), b, stride=dil)
            q = q_ref[rows_q, :]
            kc = jnp.concatenate([k_ref[rows_p, :], k_ref[rows_q, :]], axis=0).astype(BF16)
            vc = jnp.concatenate([v_ref[rows_p, :], v_ref[rows_q, :]], axis=0).astype(BF16)
            lhs = jnp.concatenate([jnp.where(mask_a, q, 0.0), jnp.where(mask_a, 0.0, q)],
                                  axis=0).astype(BF16)
            s = _dot_nt(lhs, kc) * scale
            valid = band & ((kk >= b) | (jnp.broadcast_to(i, kk.shape) > 0))
            s = jnp.where(valid, s, NEG_BIG)
            m_blk = jnp.max(s, axis=-1, keepdims=True)
            p = jnp.exp(s - m_blk)
            l_blk = jnp.sum(p, axis=-1, keepdims=True)
            pv = _dot(p.astype(BF16), vc)
            return rows_q, m_blk, l_blk, pv

        def merge(rows_q, m_blk, l_blk, pv, first=first):
            w_old, w_blk = [], []
            for hh in range(HEADS_PER_TILE):
                mb = jnp.broadcast_to(m_blk[hh * b:(hh + 1) * b], (b, LANES))
                lb = jnp.broadcast_to(l_blk[hh * b:(hh + 1) * b], (b, LANES))
                if first:
                    m_s[hh, rows_q, :] = mb
                    l_s[hh, rows_q, :] = lb
                else:
                    mo = m_s[hh, rows_q, :]
                    mn = jnp.maximum(mo, mb)
                    a_old = jnp.exp(mo - mn)
                    a_blk = jnp.exp(mb - mn)
                    m_s[hh, rows_q, :] = mn
                    l_s[hh, rows_q, :] = a_old * l_s[hh, rows_q, :] + a_blk * lb
                    w_old.append(a_old)
                    w_blk.append(a_blk)
            if first:
                acc[rows_q, :] = jnp.where(mask_a, pv[:b], pv[b:])
            else:
                acc[rows_q, :] = (jnp.where(mask_a, w_old[0], w_old[1]) * acc[rows_q, :]
                                  + jnp.where(mask_a, w_blk[0] * pv[:b], w_blk[1] * pv[b:]))

        def body(it, carry, block=block, merge=merge):
            parts = [block(it * DIL_UNROLL + u) for u in range(DIL_UNROLL)]
            for part in parts:
                merge(*part)
            return carry

        lax.fori_loop(0, (dil * nb) // DIL_UNROLL, body, 0)

    o_ref[...] = acc[...] / jnp.where(mask_a, l_s[0], l_s[1])

    tchunk = 4 * LANES
    for j in range(wp // tchunk):
        rows = pl.ds(seq - wp + j * tchunk, tchunk)
        cols = slice(j * tchunk, (j + 1) * tchunk)
        kt = k_ref[rows, :].T
        vt = v_ref[rows, :].T
        for hh in range(HEADS_PER_TILE):
            wk_ref[0, hh, :, cols] = kt[hh * HEAD_DIM:(hh + 1) * HEAD_DIM]
            wv_ref[0, hh, :, cols] = vt[hh * HEAD_DIM:(hh + 1) * HEAD_DIM]


def _dil_prompt(aq, ak, av, n_batch, seq):
    t_all = aq.shape[0]
    n_pairs = ATT_WIDTH // LANES
    wp = min(MAX_WINDOW, seq)
    blk = pl.BlockSpec((seq, LANES), lambda n, p: (n, p))
    wblk = pl.BlockSpec((1, HEADS_PER_TILE, HEAD_DIM, wp), lambda n, p: (n, p, 0, 0))
    wshape = jax.ShapeDtypeStruct((n_batch, ATT_HEADS, HEAD_DIM, wp), F32)
    return pl.pallas_call(
        functools.partial(_dil_prompt_kernel, seq=seq, wp=wp),
        grid=(n_batch, n_pairs),
        in_specs=[blk, blk, blk],
        out_specs=[blk, wblk, wblk],
        out_shape=[jax.ShapeDtypeStruct((t_all, ATT_WIDTH), F32), wshape, wshape],
        scratch_shapes=[pltpu.VMEM((seq, LANES), F32),
                        pltpu.VMEM((HEADS_PER_TILE, seq, LANES), F32),
                        pltpu.VMEM((HEADS_PER_TILE, seq, LANES), F32)],
        input_output_aliases={0: 0},
        compiler_params=_cparams(2),
        name="dil_prompt",
    )(aq, ak, av)


def _ret_sample_kernel(q_ref, k_ref, v_ref, g_ref, gn_ref, lg_ref, st_ref,
                       o_ref, nst_ref, qt, kt):
    qt[...] = q_ref[...].T
    kt[...] = k_ref[...].T
    vt = v_ref[...].T
    gt = g_ref[...].T
    lg = lg_ref[...]
    outs = []
    for hh in range(HEADS_PER_TILE):
        lo = hh * HEAD_DIM
        gdec = jnp.exp(lg[:, lo:lo + 1])
        vth = vt[lo:lo + HEAD_DIM, :]

        def body(d, o, hh=hh, lo=lo, gdec=gdec, vth=vth):
            new = gdec * st_ref[hh, d] + kt[pl.ds(lo + d, 1), :] * vth
            nst_ref[hh, d] = new
            return o + qt[pl.ds(lo + d, 1), :] * new

        o = lax.fori_loop(0, HEAD_DIM, body, jnp.zeros_like(vth))
        mu = jnp.mean(o, axis=0, keepdims=True)
        cen = o - mu
        var = jnp.mean(cen * cen, axis=0, keepdims=True)
        gate = gt[lo:lo + HEAD_DIM, :]
        outs.append(cen * lax.rsqrt(var + EPS) * gn_ref[lo:lo + HEAD_DIM, :] * (gate * _sigmoid(gate)))
    o_ref[...] = jnp.concatenate(outs, axis=0).T


def _ret_sample(rq, rk, rv, ret_n, gn_col, lg_row, state_t):
    t_all = rq.shape[0]
    n_s = state_t.shape[-1]
    n_pairs = RET_WIDTH // LANES
    last = t_all // n_s - 1
    blk = pl.BlockSpec((n_s, LANES), lambda p: (last, p))
    st_blk = pl.BlockSpec((HEADS_PER_TILE, HEAD_DIM, HEAD_DIM, n_s), lambda p: (p, 0, 0, 0))
    return pl.pallas_call(
        _ret_sample_kernel,
        grid=(n_pairs,),
        in_specs=[blk, blk, blk, blk,
                  pl.BlockSpec((LANES, 1), lambda p: (p, 0)),
                  pl.BlockSpec((1, LANES), lambda p: (0, p)),
                  st_blk],
        out_specs=[blk, st_blk],
        out_shape=[jax.ShapeDtypeStruct(ret_n.shape, F32),
                   jax.ShapeDtypeStruct(state_t.shape, F32)],
        scratch_shapes=[pltpu.VMEM((LANES, n_s), F32), pltpu.VMEM((LANES, n_s), F32)],
        input_output_aliases={3: 0},
        compiler_params=_cparams(1),
        name="ret_sample",
    )(rq, rk, rv, ret_n, gn_col, lg_row, state_t)


def _win_sample_kernel(aq_ref, akn_ref, avn_ref, kc_ref, vc_ref,
                       ko_ref, vo_ref, att_ref, qt, kt, vt, acct, *, win):
    n = pl.program_id(0)
    n_s = qt.shape[1]

    @pl.when(n == 0)
    def _():
        qt[...] = aq_ref[...].T
        kt[...] = akn_ref[...].T
        vt[...] = avn_ref[...].T
        acct[...] = jnp.zeros_like(acct)

    onehot = (lax.broadcasted_iota(I32, (n_s, LANES), 0) == n).astype(F32)
    hp = lax.Precision.HIGHEST
    qb = jnp.dot(qt[...], onehot, precision=hp, preferred_element_type=F32)
    kb = jnp.dot(kt[...], onehot, precision=hp, preferred_element_type=F32)
    vb = jnp.dot(vt[...], onehot, precision=hp, preferred_element_type=F32)

    w_pos = lax.broadcasted_iota(I32, (1, win), 1)
    back = win - w_pos
    mult = jnp.zeros((1, win), F32)
    for window, dil in DIL_PATTERNS:
        mult = mult + ((back <= window) & (back % dil == 0)).astype(F32)
    valid = mult > 0.0
    is_last = w_pos == win - 1
    reps = win // LANES
    scale = HEAD_DIM ** -0.5

    s_rows, s0_rows = [], []
    for h in range(ATT_HEADS):
        lo = h * HEAD_DIM
        k_t = kc_ref[0, h]
        qh = qb[lo:lo + HEAD_DIM, :]
        kh = kb[lo:lo + HEAD_DIM, :]
        s_rows.append(jnp.sum(k_t * jnp.concatenate([qh] * reps, axis=1), axis=0, keepdims=True))
        s0_rows.append(jnp.sum(qh * kh, axis=0, keepdims=True)[:, 0:1])
        ko_ref[0, h] = jnp.where(is_last, jnp.concatenate([kh] * reps, axis=1),
                                 pltpu.roll(k_t, win - 1, 1))
    s = jnp.concatenate(s_rows, axis=0) * scale
    s0 = jnp.concatenate(s0_rows, axis=0) * scale
    m = jnp.maximum(jnp.max(jnp.where(valid, s, NEG_BIG), axis=-1, keepdims=True), s0)
    e = jnp.where(valid, jnp.exp(s - m), 0.0) * mult
    e0 = len(DIL_PATTERNS) * jnp.exp(s0 - m)
    denom = jnp.sum(e, axis=-1, keepdims=True) + e0

    cols = []
    for h in range(ATT_HEADS):
        lo = h * HEAD_DIM
        v_t = vc_ref[0, h]
        vh = vb[lo:lo + HEAD_DIM, :]
        num = jnp.sum(v_t * e[h:h + 1, :], axis=1, keepdims=True) + e0[h:h + 1, :] * vh[:, 0:1]
        cols.append(num / denom[h:h + 1, :])
        vo_ref[0, h] = jnp.where(is_last, jnp.concatenate([vh] * reps, axis=1),
                                 pltpu.roll(v_t, win - 1, 1))
    o_col = jnp.concatenate(cols, axis=0)
    lane_n = lax.broadcasted_iota(I32, (1, n_s), 1) == n
    acct[...] = jnp.where(lane_n, o_col, acct[...])

    @pl.when(n == pl.num_programs(0) - 1)
    def _():
        att_ref[...] = acct[...].T


def _win_sample(att_o, ak, av, cache_k, cache_v):
    n_s, n_h, hd, win = cache_k.shape
    assert win >= max(w for w, _ in DIL_PATTERNS)
    t_all = att_o.shape[0]
    last = t_all // n_s - 1
    rows = pl.BlockSpec((n_s, ATT_WIDTH), lambda n: (last, 0))
    cblk = pl.BlockSpec((1, n_h, hd, win), lambda n: (n, 0, 0, 0))
    cshape = jax.ShapeDtypeStruct(cache_k.shape, F32)
    return pl.pallas_call(
        functools.partial(_win_sample_kernel, win=win),
        grid=(n_s,),
        in_specs=[rows, rows, rows, cblk, cblk],
        out_specs=[cblk, cblk, rows],
        out_shape=[cshape, cshape, jax.ShapeDtypeStruct(att_o.shape, F32)],
        scratch_shapes=[pltpu.VMEM((ATT_WIDTH, n_s), F32)] * 4,
        input_output_aliases={0: 2},
        compiler_params=_cparams(1),
        name="win_sample",
    )(att_o, ak, av, cache_k, cache_v)


def _lane_tile_norm(t, g, n_tiles):
    outs = []
    for h in range(n_tiles):
        outs.append(_rms(t[:, h * LANES:(h + 1) * LANES], g))
    return jnp.concatenate(outs, axis=-1)


def _mem_kv_kernel(mem_ref, gm_ref, w_ref, gk_ref, k_ref, v_ref):
    xn = _rms(mem_ref[0], gm_ref[...]).astype(BF16)
    kv = _dot(xn, w_ref[...])
    k_ref[0] = _lane_tile_norm(kv[:, :X_WIDTH], gk_ref[...], X_HEADS)
    v_ref[0] = kv[:, X_WIDTH:]


def _mem_kv(mem, g_mem, w_kv, g_xk):
    n, m, _ = mem.shape
    out = jax.ShapeDtypeStruct((n, m, X_WIDTH), F32)
    return pl.pallas_call(
        _mem_kv_kernel,
        grid=(n,),
        in_specs=[pl.BlockSpec((1, m, D_MODEL), lambda i: (i, 0, 0)),
                  pl.BlockSpec((1, D_MODEL), lambda i: (0, 0)),
                  pl.BlockSpec((D_MODEL, 2 * X_WIDTH), lambda i: (0, 0)),
                  pl.BlockSpec((1, X_HEAD_DIM), lambda i: (0, 0))],
        out_specs=[pl.BlockSpec((1, m, X_WIDTH), lambda i: (i, 0, 0))] * 2,
        out_shape=[out, out],
        compiler_params=_cparams(1),
        name="mem_kv",
    )(mem, g_mem, w_kv, g_xk)


def _mix_out_kernel(ret_ref, att_ref, x_ref, wo_ref, g2_ref, wq_ref, gq_ref, h_ref, q_ref):
    mixed = jnp.concatenate([ret_ref[...], att_ref[...]], axis=-1).astype(BF16)
    h = x_ref[...] + _dot(mixed, wo_ref[...])
    h_ref[...] = h
    q = _dot(_rms(h, g2_ref[...]).astype(BF16), wq_ref[...])
    q_ref[...] = _lane_tile_norm(q, gq_ref[...], X_HEADS)


def _mix_out(ret_n, att_o, x_all, w_out, g2, w_qx, g_xq):
    t_all = x_all.shape[0]
    tm = TOKEN_TILE
    row = lambda i: (i, 0)
    fixed = lambda i: (0, 0)
    return pl.pallas_call(
        _mix_out_kernel,
        grid=(t_all // tm,),
        in_specs=[pl.BlockSpec((tm, RET_WIDTH), row),
                  pl.BlockSpec((tm, ATT_WIDTH), row),
                  pl.BlockSpec((tm, D_MODEL), row),
                  pl.BlockSpec((D_MODEL, D_MODEL), fixed),
                  pl.BlockSpec((1, D_MODEL), fixed),
                  pl.BlockSpec((D_MODEL, X_WIDTH), fixed),
                  pl.BlockSpec((1, X_HEAD_DIM), fixed)],
        out_specs=[pl.BlockSpec((tm, D_MODEL), row), pl.BlockSpec((tm, X_WIDTH), row)],
        out_shape=[jax.ShapeDtypeStruct((t_all, D_MODEL), F32),
                   jax.ShapeDtypeStruct((t_all, X_WIDTH), F32)],
        compiler_params=_cparams(1),
        name="mix_out",
    )(ret_n, att_o, x_all, w_out, g2, w_qx, g_xq)


def _xattn_prompt_kernel(q_ref, k_ref, v_ref, o_ref):
    q = q_ref[...]
    k = k_ref[0]
    v = v_ref[0]
    scale = X_HEAD_DIM ** -0.5
    outs = []
    for h in range(X_HEADS):
        sl = slice(h * LANES, (h + 1) * LANES)
        s = _dot_nt(q[:, sl].astype(BF16), k[:, sl].astype(BF16)) * scale
        p = jnp.exp(s - jnp.max(s, axis=-1, keepdims=True))
        o = _dot(p.astype(BF16), v[:, sl].astype(BF16))
        outs.append(o / jnp.sum(p, axis=-1, keepdims=True))
    o_ref[...] = jnp.concatenate(outs, axis=-1)


def _xattn_prompt(qx, mk, mv, n_batch, seq, tq=512):
    t_all = qx.shape[0]
    per = seq // tq
    m = mk.shape[1]
    rows = pl.BlockSpec((tq, X_WIDTH), lambda n, i: (n * per + i, 0))
    mem = pl.BlockSpec((1, m, X_WIDTH), lambda n, i: (n, 0, 0))
    return pl.pallas_call(
        _xattn_prompt_kernel,
        grid=(n_batch, per),
        in_specs=[rows, mem, mem],
        out_specs=rows,
        out_shape=jax.ShapeDtypeStruct((t_all, X_WIDTH), F32),
        input_output_aliases={0: 0},
        compiler_params=_cparams(2),
        name="xattn_prompt",
    )(qx, mk, mv)


def _xattn_sample_kernel(q_ref, k_ref, v_ref, o_ref, *, group, n_mem):
    i = pl.program_id(0)
    scale = X_HEAD_DIM ** -0.5
    for j in range(group):
        n = i * group + j
        q = q_ref[pl.ds(n, 1), :]
        outs = []
        for h in range(X_HEADS):
            rows = pl.ds(h, n_mem, stride=X_HEADS)
            qh = q[:, h * LANES:(h + 1) * LANES]
            s = jnp.sum(k_ref[j, rows, :] * qh, axis=-1, keepdims=True) * scale
            p = jnp.exp(s - jnp.max(s, axis=0, keepdims=True))
            o = jnp.sum(p * v_ref[j, rows, :], axis=0, keepdims=True)
            outs.append(o / jnp.sum(p, axis=0, keepdims=True))
        o_ref[pl.ds(n, 1), :] = jnp.concatenate(outs, axis=-1)


XATTN_SAMPLE_GROUP = 8


def _xattn_sample(o_all, mk, mv):
    n_s, mh, _ = mk.shape
    t_all = o_all.shape[0]
    last = t_all // n_s - 1
    group = XATTN_SAMPLE_GROUP
    rows = pl.BlockSpec((n_s, X_WIDTH), lambda n: (last, 0))
    mem = pl.BlockSpec((group, mh, X_HEAD_DIM), lambda n: (n, 0, 0))
    return pl.pallas_call(
        functools.partial(_xattn_sample_kernel, group=group, n_mem=mh // X_HEADS),
        grid=(n_s // group,),
        in_specs=[rows, mem, mem],
        out_specs=rows,
        out_shape=jax.ShapeDtypeStruct(o_all.shape, F32),
        input_output_aliases={0: 0},
        compiler_params=_cparams(1),
        name="xattn_sample",
    )(o_all, mk, mv)


def _xout_router_kernel(o_ref, h_ref, wo_ref, g3_ref, wr_ref, br_ref,
                        h2_ref, xn_ref, idx_ref, gate_ref):
    h2 = h_ref[...] + _dot(o_ref[...].astype(BF16), wo_ref[...])
    h2_ref[...] = h2
    xn = _rms(h2, g3_ref[...])
    xn_ref[...] = xn
    logits = lax.dot_general(wr_ref[...], xn, (((1,), (1,)), ((), ())),
                             precision=lax.Precision.HIGHEST,
                             preferred_element_type=F32) + br_ref[...]
    eid = lax.broadcasted_iota(I32, logits.shape, 0)
    work = logits
    vals, idxs = [], []
    for _ in range(TOP_K):
        mx = jnp.max(work, axis=0, keepdims=True)
        ix = jnp.min(jnp.where(work == mx, eid, N_EXPERTS), axis=0, keepdims=True)
        vals.append(mx)
        idxs.append(ix)
        work = jnp.where(eid == ix, -jnp.inf, work)
    ex = [jnp.exp(v - vals[0]) for v in vals]
    tot = ex[0] + ex[1] + ex[2] + ex[3]
    idx_ref[...] = jnp.concatenate(idxs, axis=0)
    gate_ref[...] = jnp.concatenate([e / tot for e in ex], axis=0)


def _xout_router(o_all, h_all, w_ox, g3, w_rt, b_r):
    t_all = h_all.shape[0]
    tm = TOKEN_TILE
    row = lambda i: (i, 0)
    fixed = lambda i: (0, 0)
    colblk = lambda i: (0, i)
    return pl.pallas_call(
        _xout_router_kernel,
        grid=(t_all // tm,),
        in_specs=[pl.BlockSpec((tm, X_WIDTH), row),
                  pl.BlockSpec((tm, D_MODEL), row),
                  pl.BlockSpec((X_WIDTH, D_MODEL), fixed),
                  pl.BlockSpec((1, D_MODEL), fixed),
                  pl.BlockSpec((N_EXPERTS, D_MODEL), fixed),
                  pl.BlockSpec((N_EXPERTS, 1), fixed)],
        out_specs=[pl.BlockSpec((tm, D_MODEL), row), pl.BlockSpec((tm, D_MODEL), row),
                   pl.BlockSpec((TOP_K, tm), colblk), pl.BlockSpec((TOP_K, tm), colblk)],
        out_shape=[jax.ShapeDtypeStruct((t_all, D_MODEL), F32),
                   jax.ShapeDtypeStruct((t_all, D_MODEL), F32),
                   jax.ShapeDtypeStruct((TOP_K, t_all), I32),
                   jax.ShapeDtypeStruct((TOP_K, t_all), F32)],
        compiler_params=_cparams(1),
        name="xout_router",
    )(o_all, h_all, w_ox, g3, w_rt, b_r)


def _route_kernel(idx_ref, dest_ref, be_ref, nu_ref, *, t_all, n_blocks_pad):
    bm = MOE_BLOCK_ROWS
    nt = t_all // LANES
    e_col = lax.broadcasted_iota(I32, (N_EXPERTS, 1), 0)
    hp = lax.Precision.HIGHEST

    def multi_hot(j):
        blk = idx_ref[:, pl.ds(pl.multiple_of(j * LANES, LANES), LANES)]
        mh = jnp.zeros((N_EXPERTS, LANES), F32)
        for k in range(TOP_K):
            mh = mh + (e_col == blk[k:k + 1, :]).astype(F32)
        return blk, mh

    def count_body(j, c):
        _, mh = multi_hot(j)
        return c + jnp.sum(mh, axis=1, keepdims=True)

    counts = lax.fori_loop(0, nt, count_body, jnp.zeros((N_EXPERTS, 1), F32))
    padded = jnp.ceil(counts * (1.0 / bm)) * bm
    tri = (lax.broadcasted_iota(I32, (N_EXPERTS, N_EXPERTS), 1)
           <= lax.broadcasted_iota(I32, (N_EXPERTS, N_EXPERTS), 0)).astype(F32)
    pad_end = jnp.dot(tri, jnp.broadcast_to(padded, (N_EXPERTS, LANES)), precision=hp,
                      preferred_element_type=F32)
    pad_start = pad_end[:, 0:1] - padded
    upper = (lax.broadcasted_iota(I32, (LANES, LANES), 0)
             < lax.broadcasted_iota(I32, (LANES, LANES), 1)).astype(BF16)

    def dest_body(j, carry):
        blk, mh = multi_hot(j)
        rank = carry + _dot(mh.astype(BF16), upper)
        base = pad_start + rank
        for k in range(TOP_K):
            d = jnp.sum(jnp.where(e_col == blk[k:k + 1, :], base, 0.0), axis=0, keepdims=True)
            dest_ref[pl.ds(k, 1), pl.ds(pl.multiple_of(j * LANES, LANES), LANES)] = d.astype(I32)
        return carry + jnp.sum(mh, axis=1, keepdims=True)

    lax.fori_loop(0, nt, dest_body, jnp.zeros((N_EXPERTS, 1), F32))

    b_row = lax.broadcasted_iota(I32, (1, n_blocks_pad), 1).astype(F32) * bm
    be = jnp.sum((pad_end[:, 0:1] <= b_row).astype(F32), axis=0, keepdims=True)
    be_ref[...] = jnp.minimum(be, N_EXPERTS - 1.0).astype(I32)
    nu_ref[...] = (pad_end[N_EXPERTS - 1:N_EXPERTS, :] * (1.0 / bm)).astype(I32)


def _route(idx_t, n_blocks_pad):
    t_all = idx_t.shape[1]
    return pl.pallas_call(
        functools.partial(_route_kernel, t_all=t_all, n_blocks_pad=n_blocks_pad),
        out_shape=[jax.ShapeDtypeStruct((TOP_K, t_all), I32),
                   jax.ShapeDtypeStruct((1, n_blocks_pad), I32),
                   jax.ShapeDtypeStruct((1, LANES), I32)],
        compiler_params=pltpu.CompilerParams(vmem_limit_bytes=VMEM_LIMIT),
        name="moe_route",
    )(idx_t)


def _dispatch_kernel(dest_sm, x_ref, xs_in, xs_ref, sem, *, t_all):
    del xs_in
    i = pl.program_id(0)
    tm = x_ref.shape[0]

    def row_copy(j, d):
        return pltpu.make_async_copy(x_ref.at[pl.ds(j, 1)], xs_ref.at[pl.ds(d, 1)], sem)

    def start_body(j, c):
        for k in range(TOP_K):
            row_copy(j, dest_sm[k * t_all + i * tm + j]).start()
        return c

    lax.fori_loop(0, tm, start_body, 0)

    def wait_body(j, c):
        for k in range(TOP_K):
            row_copy(0, 0).wait()
        return c

    lax.fori_loop(0, tm, wait_body, 0)


def _dispatch(dest_flat, xn_all, xs_init):
    t_all = xn_all.shape[0]
    tm = ROW_TILE
    return pl.pallas_call(
        functools.partial(_dispatch_kernel, t_all=t_all),
        grid_spec=pltpu.PrefetchScalarGridSpec(
            num_scalar_prefetch=1,
            grid=(t_all // tm,),
            in_specs=[pl.BlockSpec((tm, D_MODEL), lambda i, d: (i, 0)),
                      pl.BlockSpec(memory_space=pl.ANY)],
            out_specs=pl.BlockSpec(memory_space=pl.ANY),
            scratch_shapes=[pltpu.SemaphoreType.DMA]),
        out_shape=jax.ShapeDtypeStruct(xs_init.shape, F32),
        input_output_aliases={2: 0},
        compiler_params=_cparams(1),
        name="moe_dispatch",
    )(dest_flat, xn_all, xs_init)


def _expert_kernel(be_sm, nu_sm, x_ref, wgu_ref, bgu_ref, wd_ref, bd_ref, y_ref, wgu_bf, wd_bf):
    b = pl.program_id(0)
    changed = jnp.logical_or(b == 0, be_sm[b] != be_sm[jnp.maximum(b - 1, 0)])

    @pl.when(changed)
    def _():
        wgu_bf[...] = wgu_ref[0].astype(BF16)
        wd_bf[...] = wd_ref[0].astype(BF16)

    @pl.when(b < nu_sm[0])
    def _():
        h = _dot(x_ref[...].astype(BF16), wgu_bf[...]) + bgu_ref[0]
        glu = jnp.minimum(h[:, :D_FF], SWIGLU_LIMIT)
        lin = jnp.clip(h[:, D_FF:], -SWIGLU_LIMIT, SWIGLU_LIMIT)
        act = glu * _sigmoid(SWIGLU_ALPHA * glu) * (lin + 1.0)
        y_ref[...] = _dot(act.astype(BF16), wd_bf[...]) + bd_ref[0]

    @pl.when(b >= nu_sm[0])
    def _():
        y_ref[...] = jnp.zeros_like(y_ref)


def _experts(block_e, n_used, xs, w_gu, b_gu, w_down, b_down):
    cap = xs.shape[0]
    bm = MOE_BLOCK_ROWS
    return pl.pallas_call(
        _expert_kernel,
        grid_spec=pltpu.PrefetchScalarGridSpec(
            num_scalar_prefetch=2,
            grid=(cap // bm,),
            in_specs=[pl.BlockSpec((bm, D_MODEL), lambda b, be, nu: (b, 0)),
                      pl.BlockSpec((1, D_MODEL, 2 * D_FF), lambda b, be, nu: (be[b], 0, 0)),
                      pl.BlockSpec((1, 1, 2 * D_FF), lambda b, be, nu: (be[b], 0, 0)),
                      pl.BlockSpec((1, D_FF, D_MODEL), lambda b, be, nu: (be[b], 0, 0)),
                      pl.BlockSpec((1, 1, D_MODEL), lambda b, be, nu: (be[b], 0, 0))],
            out_specs=pl.BlockSpec((bm, D_MODEL), lambda b, be, nu: (b, 0)),
            scratch_shapes=[pltpu.VMEM((D_MODEL, 2 * D_FF), BF16),
                            pltpu.VMEM((D_FF, D_MODEL), BF16)]),
        out_shape=jax.ShapeDtypeStruct((cap, D_MODEL), F32),
        compiler_params=_cparams(1),
        name="moe_experts",
    )(block_e, n_used, xs, w_gu, b_gu, w_down, b_down)


def _combine_kernel(dest_sm, yb_ref, h_ref, g_ref, op_ref, os_ref, buf, sem, *, t_all):
    i = pl.program_id(0)
    tm = h_ref.shape[0]

    def row_copy(k, j, d):
        return pltpu.make_async_copy(yb_ref.at[pl.ds(d, 1)], buf.at[k, pl.ds(j, 1)], sem)

    def start_body(j, c):
        for k in range(TOP_K):
            row_copy(k, j, dest_sm[k * t_all + i * tm + j]).start()
        return c

    lax.fori_loop(0, tm, start_body, 0)

    def wait_body(j, c):
        for k in range(TOP_K):
            row_copy(0, 0, 0).wait()
        return c

    lax.fori_loop(0, tm, wait_body, 0)

    g = g_ref[...]
    y = h_ref[...]
    for k in range(TOP_K):
        y = y + g[:, k:k + 1] * buf[k]
    last = pl.num_programs(0) - 1

    @pl.when(i < last)
    def _():
        op_ref[...] = y

    @pl.when(i == last)
    def _():
        os_ref[...] = y


def _combine(dest_flat, yb, h2_all, gates, n_prompt_rows):
    t_all = h2_all.shape[0]
    tm = ROW_TILE
    n_p_tiles = n_prompt_rows // tm
    assert t_all == n_prompt_rows + tm
    return pl.pallas_call(
        functools.partial(_combine_kernel, t_all=t_all),
        grid_spec=pltpu.PrefetchScalarGridSpec(
            num_scalar_prefetch=1,
            grid=(t_all // tm,),
            in_specs=[pl.BlockSpec(memory_space=pl.ANY),
                      pl.BlockSpec((tm, D_MODEL), lambda i, d: (i, 0)),
                      pl.BlockSpec((tm, TOP_K), lambda i, d: (i, 0))],
            out_specs=[pl.BlockSpec((tm, D_MODEL), lambda i, d: (jnp.minimum(i, n_p_tiles - 1), 0)),
                       pl.BlockSpec((tm, D_MODEL), lambda i, d: (0, 0))],
            scratch_shapes=[pltpu.VMEM((TOP_K, tm, D_MODEL), F32), pltpu.SemaphoreType.DMA]),
        out_shape=[jax.ShapeDtypeStruct((n_prompt_rows, D_MODEL), F32),
                   jax.ShapeDtypeStruct((tm, D_MODEL), F32)],
        compiler_params=_cparams(1),
        name="moe_combine",
    )(dest_flat, yb, h2_all, gates)


def _rope_tables(pos):
    half = HEAD_DIM // 2
    inv = jnp.exp(-math.log(ROPE_THETA) * jnp.arange(half, dtype=F32) / half)
    ang = pos.astype(F32)[:, None] * inv[None, :]
    cos, sin = jnp.cos(ang), jnp.sin(ang)
    cos_t = jnp.concatenate([cos, cos] * HEADS_PER_TILE, axis=-1)
    sin_t = jnp.concatenate([-sin, sin] * HEADS_PER_TILE, axis=-1)
    return cos_t, sin_t


def _block_diag_ones(n, blk):
    r = jnp.arange(n) // blk
    return (r[:, None] == r[None, :]).astype(BF16)


def _layer(x_prompt, x_sample, state_ret, cache_win_k, cache_win_v, cache_mem_k, cache_mem_v,
           mem_prompt, g_norm1, w_in, g_att_q, g_att_k, g_ret_gn, w_out, g_norm2, g_mem,
           w_q_x, w_kv_x, g_x_q, g_x_k, w_o_x, g_norm3, w_router, b_router, w_gu, b_gu,
           w_down, b_down):
    n_b, seq, _ = x_prompt.shape
    n_s = x_sample.shape[0]
    assert x_sample.shape[1] == 1 and n_s == ROW_TILE
    t_p = n_b * seq
    t_all = t_p + n_s
    assert t_all % TOKEN_TILE == 0 and t_p % ROW_TILE == 0

    x_all = jnp.concatenate([x_prompt.reshape(t_p, D_MODEL), x_sample.reshape(n_s, D_MODEL)], axis=0)
    pos = jnp.concatenate([jnp.tile(jnp.arange(seq, dtype=jnp.int32), n_b),
                           jnp.full((n_s,), PAST_LEN, jnp.int32)])
    cos_t, sin_t = _rope_tables(pos)
    log_g = jnp.log1p(-jnp.exp2(-5.0 - jnp.arange(RET_HEADS, dtype=F32)))
    lg_row = jnp.repeat(log_g, HEAD_DIM)[None, :]
    gn_row = g_ret_gn.reshape(1, RET_WIDTH)
    gn_col = g_ret_gn.reshape(RET_WIDTH, 1)
    gq = jnp.tile(g_att_q.reshape(1, HEAD_DIM), (1, ATT_HEADS))
    gk = jnp.tile(g_att_k.reshape(1, HEAD_DIM), (1, ATT_HEADS))
    seg = _block_diag_ones(ATT_WIDTH, HEAD_DIM)

    rq, rk, rv, rg, aq, ak, av = _in_proj(
        x_all, g_norm1.reshape(1, D_MODEL), w_in.astype(BF16), cos_t, sin_t, gq, gk, seg)

    ret_n, st_p = _ret_prompt(rq, rk, rv, rg, gn_row, lg_row, n_b, seq)
    att_o, wk_p, wv_p = _dil_prompt(aq, ak, av, n_b, seq)
    state_t = jnp.transpose(state_ret, (1, 2, 3, 0))
    ret_n, st_s = _ret_sample(rq, rk, rv, ret_n, gn_col, lg_row, state_t)
    ck = jnp.transpose(cache_win_k, (0, 2, 3, 1))
    cv = jnp.transpose(cache_win_v, (0, 2, 3, 1))
    wk_s, wv_s, att_o = _win_sample(att_o, ak, av, ck, cv)

    h_all, qx = _mix_out(ret_n, att_o, x_all, w_out.astype(BF16), g_norm2.reshape(1, D_MODEL),
                         w_q_x.astype(BF16), g_x_q.reshape(1, X_HEAD_DIM))

    mk_p, mv_p = _mem_kv(mem_prompt, g_mem.reshape(1, D_MODEL), w_kv_x.astype(BF16),
                         g_x_k.reshape(1, X_HEAD_DIM))
    o_all = _xattn_prompt(qx, mk_p, mv_p, n_b, seq)
    n_mem = cache_mem_k.shape[1]
    o_all = _xattn_sample(o_all, cache_mem_k.reshape(n_s, n_mem * X_HEADS, X_HEAD_DIM),
                          cache_mem_v.reshape(n_s, n_mem * X_HEADS, X_HEAD_DIM))

    h2_all, xn_all, idx_t, gate_t = _xout_router(
        o_all, h_all, w_o_x.astype(BF16), g_norm3.reshape(1, D_MODEL),
        jnp.transpose(w_router), b_router.reshape(N_EXPERTS, 1))

    bm = MOE_BLOCK_ROWS
    n_blocks = -(-(t_all * TOP_K) // bm) + N_EXPERTS
    n_blocks_pad = -(-n_blocks // LANES) * LANES
    dest_t, be, nu = _route(idx_t, n_blocks_pad)
    dest_flat = dest_t.reshape(TOP_K * t_all)
    xs = _dispatch(dest_flat, xn_all, jnp.zeros((n_blocks * bm, D_MODEL), F32))
    yb = _experts(be.reshape(n_blocks_pad), nu.reshape(LANES), xs, w_gu,
                  b_gu.reshape(N_EXPERTS, 1, 2 * D_FF), w_down, b_down.reshape(N_EXPERTS, 1, D_MODEL))
    y_p, y_s = _combine(dest_flat, yb, h2_all, jnp.transpose(gate_t), t_p)

    st_p = jnp.stack([st_p[:, :, :HEAD_DIM, :HEAD_DIM], st_p[:, :, HEAD_DIM:, HEAD_DIM:]], axis=2)
    st_p = st_p.reshape(n_b, RET_HEADS, HEAD_DIM, HEAD_DIM)
    return (y_p.reshape(n_b, seq, D_MODEL),
            y_s.reshape(n_s, 1, D_MODEL),
            st_p,
            jnp.transpose(st_s, (3, 0, 1, 2)),
            jnp.transpose(wk_p, (0, 3, 1, 2)),
            jnp.transpose(wv_p, (0, 3, 1, 2)),
            jnp.transpose(wk_s, (0, 3, 1, 2)),
            jnp.transpose(wv_s, (0, 3, 1, 2)),
            mk_p.reshape(n_b, n_mem, X_HEADS, X_HEAD_DIM),
            mv_p.reshape(n_b, n_mem, X_HEADS, X_HEAD_DIM))


def kernel(x_prompt, x_sample, state_ret, cache_win_k, cache_win_v, cache_mem_k, cache_mem_v,
           mem_prompt, g_norm1, w_in, g_att_q, g_att_k, g_ret_gn, w_out, g_norm2, g_mem,
           w_q_x, w_kv_x, g_x_q, g_x_k, w_o_x, g_norm3, w_router, b_router, w_gu, b_gu,
           w_down, b_down):
    assert state_ret.shape[0] == 1, "single-layer trunk"
    outs = _layer(x_prompt, x_sample, state_ret[0], cache_win_k[0], cache_win_v[0],
                  cache_mem_k[0], cache_mem_v[0], mem_prompt, g_norm1[0], w_in[0], g_att_q[0],
                  g_att_k[0], g_ret_gn[0], w_out[0], g_norm2[0], g_mem[0], w_q_x[0], w_kv_x[0],
                  g_x_q[0], g_x_k[0], w_o_x[0], g_norm3[0], w_router[0], b_router[0], w_gu[0],
                  b_gu[0], w_down[0], b_down[0])
    y_p, y_s = outs[0], outs[1]
    return (y_p, y_s) + tuple(o[None] for o in outs[2:])
```

```python
import functools
import math

import jax
import jax.numpy as jnp
from jax import lax
from jax.experimental import pallas as pl
from jax.experimental.pallas import tpu as pltpu

F32 = jnp.float32
BF16 = jnp.bfloat16
I32 = jnp.int32

D_MODEL = 1024
HEAD_DIM = 64
RET_HEADS = 8
ATT_HEADS = 8
RET_WIDTH = RET_HEADS * HEAD_DIM
ATT_WIDTH = ATT_HEADS * HEAD_DIM
IN_COLS = 4 * RET_WIDTH + 3 * ATT_WIDTH
RET_CHUNK = 128
DIL_PATTERNS = ((128, 1), (512, 4), (2048, 16))
DIL_BLOCK = 128
MAX_WINDOW = 2048
X_HEADS = 4
X_HEAD_DIM = 128
X_WIDTH = X_HEADS * X_HEAD_DIM
N_EXPERTS = 32
TOP_K = 4
D_FF = D_MODEL
SWIGLU_LIMIT = 7.0
SWIGLU_ALPHA = 1.702
ROPE_THETA = 10000.0
EPS = 1e-6
PAST_LEN = 8192

LANES = 128
SUBLANES = 8
HEADS_PER_TILE = LANES // HEAD_DIM

TOKEN_TILE = 384
ROW_TILE = 128
MOE_BLOCK_ROWS = 256
NEG_BIG = -1e30
VMEM_LIMIT = 48 * 1024 * 1024


def _cparams(n_axes):
    return pltpu.CompilerParams(
        dimension_semantics=("arbitrary",) * n_axes, vmem_limit_bytes=VMEM_LIMIT)


def _rms(x, g):
    return x * lax.rsqrt(jnp.mean(x * x, axis=-1, keepdims=True) + EPS) * g


def _dot(a, b):
    return jnp.dot(a, b, preferred_element_type=F32)


def _dot_nt(a, b):
    return lax.dot_general(a, b, (((1,), (1,)), ((), ())), preferred_element_type=F32)


def _dot_tn(a, b):
    return lax.dot_general(a, b, (((0,), (0,)), ((), ())), preferred_element_type=F32)


def _sigmoid(x):
    return 1.0 / (1.0 + jnp.exp(-x))


def _in_proj_kernel(x_ref, g1_ref, w_ref, cos_ref, sin_ref, gq_ref, gk_ref, seg_ref,
                    rq_ref, rk_ref, rv_ref, rg_ref, aq_ref, ak_ref, av_ref):
    xn = _rms(x_ref[...], g1_ref[...]).astype(BF16)
    proj = _dot(xn, w_ref[...])
    reps = RET_WIDTH // LANES
    cos = jnp.concatenate([cos_ref[...]] * reps, axis=-1)
    sin = jnp.concatenate([sin_ref[...]] * reps, axis=-1)
    lane = lax.broadcasted_iota(I32, (1, RET_WIDTH), 1)
    first_half = (lane % HEAD_DIM) < (HEAD_DIM // 2)
    seg = seg_ref[...]

    def rope(t):
        partner = jnp.where(first_half,
                            pltpu.roll(t, RET_WIDTH - HEAD_DIM // 2, 1),
                            pltpu.roll(t, HEAD_DIM // 2, 1))
        return t * cos + partner * sin

    def head_norm(t, g):
        sq = t * t
        hi = sq.astype(BF16)
        lo = (sq - hi.astype(F32)).astype(BF16)
        ssum = _dot(hi, seg) + _dot(lo, seg)
        return t * lax.rsqrt(ssum * (1.0 / HEAD_DIM) + EPS) * g

    w = RET_WIDTH
    rq_ref[...] = rope(proj[:, 0:w])
    rk_ref[...] = rope(proj[:, w:2 * w]) * (HEAD_DIM ** -0.5)
    rv_ref[...] = proj[:, 2 * w:3 * w]
    rg_ref[...] = proj[:, 3 * w:4 * w]
    aq_ref[...] = rope(head_norm(proj[:, 4 * w:5 * w], gq_ref[...]))
    ak_ref[...] = rope(head_norm(proj[:, 5 * w:6 * w], gk_ref[...]))
    av_ref[...] = proj[:, 6 * w:7 * w]


def _in_proj(x_all, g1, w_in, cos_t, sin_t, gq, gk, seg):
    t_all = x_all.shape[0]
    tm = TOKEN_TILE
    row = lambda i: (i, 0)
    fixed = lambda i: (0, 0)
    out = jax.ShapeDtypeStruct((t_all, RET_WIDTH), F32)
    return pl.pallas_call(
        _in_proj_kernel,
        grid=(t_all // tm,),
        in_specs=[pl.BlockSpec((tm, D_MODEL), row),
                  pl.BlockSpec((1, D_MODEL), fixed),
                  pl.BlockSpec((D_MODEL, IN_COLS), fixed),
                  pl.BlockSpec((tm, LANES), row),
                  pl.BlockSpec((tm, LANES), row),
                  pl.BlockSpec((1, ATT_WIDTH), fixed),
                  pl.BlockSpec((1, ATT_WIDTH), fixed),
                  pl.BlockSpec((ATT_WIDTH, ATT_WIDTH), fixed)],
        out_specs=[pl.BlockSpec((tm, RET_WIDTH), row)] * 7,
        out_shape=[out] * 7,
        compiler_params=_cparams(1),
        name="in_proj",
    )(x_all, g1, w_in, cos_t, sin_t, gq, gk, seg)


def _group_norm_gate(o, mask_a, gn, gate):
    inv = 1.0 / HEAD_DIM
    sa = jnp.sum(jnp.where(mask_a, o, 0.0), axis=-1, keepdims=True)
    sb = jnp.sum(jnp.where(mask_a, 0.0, o), axis=-1, keepdims=True)
    cen = o - jnp.where(mask_a, sa, sb) * inv
    c2 = cen * cen
    va = jnp.sum(jnp.where(mask_a, c2, 0.0), axis=-1, keepdims=True)
    vb = jnp.sum(jnp.where(mask_a, 0.0, c2), axis=-1, keepdims=True)
    var = jnp.where(mask_a, va, vb) * inv
    return cen * lax.rsqrt(var + EPS) * gn * (gate * _sigmoid(gate))


def _ret_prompt_kernel(q_ref, k_ref, v_ref, g_ref, gn_ref, lg_ref, o_ref, st_ref,
                       state, dec, qdec, kdec):
    n = pl.program_id(0)
    c = pl.program_id(1)
    ch = RET_CHUNK
    n_pairs = RET_WIDTH // LANES
    lane = lax.broadcasted_iota(I32, (1, LANES), 1)
    mask_a = lane < HEAD_DIM
    row_i = lax.broadcasted_iota(I32, (ch, 1), 0)

    @pl.when(jnp.logical_and(n == 0, c == 0))
    def _():
        row = row_i.astype(F32)
        col = lax.broadcasted_iota(I32, (1, ch), 1).astype(F32)
        diff = row - col
        causal = diff >= 0.0
        dpos = jnp.maximum(diff, 0.0)
        lg = lg_ref[...]
        for h in range(RET_HEADS):
            dec[h] = jnp.where(causal, jnp.exp(dpos * lg[:, h * HEAD_DIM:h * HEAD_DIM + 1]), 0.0)
        qdec[...] = jnp.exp((row + 1.0) * lg)
        kdec[...] = jnp.exp((ch - 1.0 - row) * lg)

    @pl.when(c == 0)
    def _():
        state[...] = jnp.zeros_like(state)

    same_head = (row_i // HEAD_DIM) == (lane // HEAD_DIM)
    cdec = jnp.exp(ch * lg_ref[...])
    for p in range(n_pairs):
        sl = slice(p * LANES, (p + 1) * LANES)
        q = q_ref[:, sl]
        k = k_ref[:, sl]
        kb = k.astype(BF16)
        vb = v_ref[:, sl].astype(BF16)
        lhs = jnp.concatenate([jnp.where(mask_a, q, 0.0), jnp.where(mask_a, 0.0, q)],
                              axis=0).astype(BF16)
        s = _dot_nt(lhs, kb)
        pr = (s * jnp.concatenate([dec[2 * p], dec[2 * p + 1]], axis=0)).astype(BF16)
        o2 = _dot(pr, vb)
        o_intra = jnp.where(mask_a, o2[:ch], o2[ch:])
        st = state[p]
        o = o_intra + _dot(q.astype(BF16), st.astype(BF16)) * qdec[:, sl]
        upd = _dot_tn((k * kdec[:, sl]).astype(BF16), vb)
        state[p] = cdec[:, sl] * st + jnp.where(same_head, upd, 0.0)
        o_ref[:, sl] = _group_norm_gate(o, mask_a, gn_ref[:, sl], g_ref[:, sl])

    @pl.when(c == pl.num_programs(1) - 1)
    def _():
        st_ref[0] = state[...]


def _ret_prompt(rq, rk, rv, rg, gn_row, lg_row, n_batch, seq):
    t_all = rq.shape[0]
    n_pairs = RET_WIDTH // LANES
    n_chunks = seq // RET_CHUNK
    blk = pl.BlockSpec((RET_CHUNK, RET_WIDTH), lambda n, c: (n * n_chunks + c, 0))
    lane_row = pl.BlockSpec((1, RET_WIDTH), lambda n, c: (0, 0))
    return pl.pallas_call(
        _ret_prompt_kernel,
        grid=(n_batch, n_chunks),
        in_specs=[blk, blk, blk, blk, lane_row, lane_row],
        out_specs=[blk, pl.BlockSpec((1, n_pairs, LANES, LANES), lambda n, c: (n, 0, 0, 0))],
        out_shape=[jax.ShapeDtypeStruct((t_all, RET_WIDTH), F32),
                   jax.ShapeDtypeStruct((n_batch, n_pairs, LANES, LANES), F32)],
        scratch_shapes=[pltpu.VMEM((n_pairs, LANES, LANES), F32),
                        pltpu.VMEM((RET_HEADS, RET_CHUNK, RET_CHUNK), F32),
                        pltpu.VMEM((RET_CHUNK, RET_WIDTH), F32),
                        pltpu.VMEM((RET_CHUNK, RET_WIDTH), F32)],
        input_output_aliases={3: 0},
        compiler_params=_cparams(2),
        name="ret_prompt",
    )(rq, rk, rv, rg, gn_row, lg_row)


DIL_UNROLL = 4


def _dil_prompt_kernel(q_ref, k_ref, v_ref, o_ref, wk_ref, wv_ref, acc, m_s, l_s, *, seq, wp):
    b = DIL_BLOCK
    lane = lax.broadcasted_iota(I32, (1, LANES), 1)
    mask_a = lane < HEAD_DIM
    qi = lax.broadcasted_iota(I32, (2 * b, 1), 0) % b
    kk = lax.broadcasted_iota(I32, (1, 2 * b), 1)
    dist = qi + b - kk
    scale = HEAD_DIM ** -0.5

    for first, (window, dil) in zip((True, False, False), DIL_PATTERNS):
        steps = window // dil
        band = (dist >= 0) & (dist <= steps)
        nb = seq // (b * dil)

        def block(idx, dil=dil, nb=nb, band=band):
            r = idx // nb
            i = idx % nb
            rows_q = pl.ds(r + i * (b * dil), b, stride=dil)
            rows_p = pl.ds(r + jnp.maximum(i - 1, 0) * (b * dil), b, stride=dil)
            q = q_ref[rows_q, :]
            kc = jnp.concatenate([k_ref[rows_p, :], k_ref[rows_q, :]], axis=0).astype(BF16)
            vc = jnp.concatenate([v_ref[rows_p, :], v_ref[rows_q, :]], axis=0).astype(BF16)
            lhs = jnp.concatenate([jnp.where(mask_a, q, 0.0), jnp.where(mask_a, 0.0, q)],
                                  axis=0).astype(BF16)
            s = _dot_nt(lhs, kc) * scale
            valid = band & ((kk >= b) | (jnp.broadcast_to(i, kk.shape) > 0))
            s = jnp.where(valid, s, NEG_BIG)
            m_blk = jnp.max(s, axis=-1, keepdims=True)
            p = jnp.exp(s - m_blk)
            l_blk = jnp.sum(p, axis=-1, keepdims=True)
            pv = _dot(p.astype(BF16), vc)
            return rows_q, m_blk, l_blk, pv

        def merge(rows_q, m_blk, l_blk, pv, first=first):
            w_old, w_blk = [], []
            for hh in range(HEADS_PER_TILE):
                mb = jnp.broadcast_to(m_blk[hh * b:(hh + 1) * b], (b, LANES))
                lb = jnp.broadcast_to(l_blk[hh * b:(hh + 1) * b], (b, LANES))
                if first:
                    m_s[hh, rows_q, :] = mb
                    l_s[hh, rows_q, :] = lb
                else:
                    mo = m_s[hh, rows_q, :]
                    mn = jnp.maximum(mo, mb)
                    a_old = jnp.exp(mo - mn)
                    a_blk = jnp.exp(mb - mn)
                    m_s[hh, rows_q, :] = mn
                    l_s[hh, rows_q, :] = a_old * l_s[hh, rows_q, :] + a_blk * lb
                    w_old.append(a_old)
                    w_blk.append(a_blk)
            if first:
                acc[rows_q, :] = jnp.where(mask_a, pv[:b], pv[b:])
            else:
                acc[rows_q, :] = (jnp.where(mask_a, w_old[0], w_old[1]) * acc[rows_q, :]
                                  + jnp.where(mask_a, w_blk[0] * pv[:b], w_blk[1] * pv[b:]))

        def body(it, carry, block=block, merge=merge):
            parts = [block(it * DIL_UNROLL + u) for u in range(DIL_UNROLL)]
            for part in parts:
                merge(*part)
            return carry

        lax.fori_loop(0, (dil * nb) // DIL_UNROLL, body, 0)

    o_ref[...] = acc[...] / jnp.where(mask_a, l_s[0], l_s[1])

    tchunk = 4 * LANES
    for j in range(wp // tchunk):
        rows = pl.ds(seq - wp + j * tchunk, tchunk)
        cols = slice(j * tchunk, (j + 1) * tchunk)
        kt = k_ref[rows, :].T
        vt = v_ref[rows, :].T
        for hh in range(HEADS_PER_TILE):
            wk_ref[0, hh, :, cols] = kt[hh * HEAD_DIM:(hh + 1) * HEAD_DIM]
            wv_ref[0, hh, :, cols] = vt[hh * HEAD_DIM:(hh + 1) * HEAD_DIM]


def _dil_prompt(aq, ak, av, n_batch, seq):
    t_all = aq.shape[0]
    n_pairs = ATT_WIDTH // LANES
    wp = min(MAX_WINDOW, seq)
    blk = pl.BlockSpec((seq, LANES), lambda n, p: (n, p))
    wblk = pl.BlockSpec((1, HEADS_PER_TILE, HEAD_DIM, wp), lambda n, p: (n, p, 0, 0))
    wshape = jax.ShapeDtypeStruct((n_batch, ATT_HEADS, HEAD_DIM, wp), F32)
    return pl.pallas_call(
        functools.partial(_dil_prompt_kernel, seq=seq, wp=wp),
        grid=(n_batch, n_pairs),
        in_specs=[blk, blk, blk],
        out_specs=[blk, wblk, wblk],
        out_shape=[jax.ShapeDtypeStruct((t_all, ATT_WIDTH), F32), wshape, wshape],
        scratch_shapes=[pltpu.VMEM((seq, LANES), F32),
                        pltpu.VMEM((HEADS_PER_TILE, seq, LANES), F32),
                        pltpu.VMEM((HEADS_PER_TILE, seq, LANES), F32)],
        input_output_aliases={0: 0},
        compiler_params=_cparams(2),
        name="dil_prompt",
    )(aq, ak, av)


def _ret_sample_kernel(q_ref, k_ref, v_ref, g_ref, gn_ref, lg_ref, st_ref,
                       o_ref, nst_ref, qt, kt):
    qt[...] = q_ref[...].T
    kt[...] = k_ref[...].T
    vt = v_ref[...].T
    gt = g_ref[...].T
    lg = lg_ref[...]
    outs = []
    for hh in range(HEADS_PER_TILE):
        lo = hh * HEAD_DIM
        gdec = jnp.exp(lg[:, lo:lo + 1])
        vth = vt[lo:lo + HEAD_DIM, :]

        def body(d, o, hh=hh, lo=lo, gdec=gdec, vth=vth):
            new = gdec * st_ref[hh, d] + kt[pl.ds(lo + d, 1), :] * vth
            nst_ref[hh, d] = new
            return o + qt[pl.ds(lo + d, 1), :] * new

        o = lax.fori_loop(0, HEAD_DIM, body, jnp.zeros_like(vth))
        mu = jnp.mean(o, axis=0, keepdims=True)
        cen = o - mu
        var = jnp.mean(cen * cen, axis=0, keepdims=True)
        gate = gt[lo:lo + HEAD_DIM, :]
        outs.append(cen * lax.rsqrt(var + EPS) * gn_ref[lo:lo + HEAD_DIM, :] * (gate * _sigmoid(gate)))
    o_ref[...] = jnp.concatenate(outs, axis=0).T


def _ret_sample(rq, rk, rv, ret_n, gn_col, lg_row, state_t):
    t_all = rq.shape[0]
    n_s = state_t.shape[-1]
    n_pairs = RET_WIDTH // LANES
    last = t_all // n_s - 1
    blk = pl.BlockSpec((n_s, LANES), lambda p: (last, p))
    st_blk = pl.BlockSpec((HEADS_PER_TILE, HEAD_DIM, HEAD_DIM, n_s), lambda p: (p, 0, 0, 0))
    return pl.pallas_call(
        _ret_sample_kernel,
        grid=(n_pairs,),
        in_specs=[blk, blk, blk, blk,
                  pl.BlockSpec((LANES, 1), lambda p: (p, 0)),
                  pl.BlockSpec((1, LANES), lambda p: (0, p)),
                  st_blk],
        out_specs=[blk, st_blk],
        out_shape=[jax.ShapeDtypeStruct(ret_n.shape, F32),
                   jax.ShapeDtypeStruct(state_t.shape, F32)],
        scratch_shapes=[pltpu.VMEM((LANES, n_s), F32), pltpu.VMEM((LANES, n_s), F32)],
        input_output_aliases={3: 0},
        compiler_params=_cparams(1),
        name="ret_sample",
    )(rq, rk, rv, ret_n, gn_col, lg_row, state_t)


def _win_sample_kernel(aq_ref, akn_ref, avn_ref, kc_ref, vc_ref,
                       ko_ref, vo_ref, att_ref, qt, kt, vt, acct, *, win):
    n = pl.program_id(0)
    n_s = qt.shape[1]

    @pl.when(n == 0)
    def _():
        qt[...] = aq_ref[...].T
        kt[...] = akn_ref[...].T
        vt[...] = avn_ref[...].T
        acct[...] = jnp.zeros_like(acct)

    onehot = (lax.broadcasted_iota(I32, (n_s, LANES), 0) == n).astype(F32)
    hp = lax.Precision.HIGHEST
    qb = jnp.dot(qt[...], onehot, precision=hp, preferred_element_type=F32)
    kb = jnp.dot(kt[...], onehot, precision=hp, preferred_element_type=F32)
    vb = jnp.dot(vt[...], onehot, precision=hp, preferred_element_type=F32)

    w_pos = lax.broadcasted_iota(I32, (1, win), 1)
    back = win - w_pos
    mult = jnp.zeros((1, win), F32)
    for window, dil in DIL_PATTERNS:
        mult = mult + ((back <= window) & (back % dil == 0)).astype(F32)
    valid = mult > 0.0
    is_last = w_pos == win - 1
    reps = win // LANES
    scale = HEAD_DIM ** -0.5

    s_rows, s0_rows = [], []
    for h in range(ATT_HEADS):
        lo = h * HEAD_DIM
        k_t = kc_ref[0, h]
        qh = qb[lo:lo + HEAD_DIM, :]
        kh = kb[lo:lo + HEAD_DIM, :]
        s_rows.append(jnp.sum(k_t * jnp.concatenate([qh] * reps, axis=1), axis=0, keepdims=True))
        s0_rows.append(jnp.sum(qh * kh, axis=0, keepdims=True)[:, 0:1])
        ko_ref[0, h] = jnp.where(is_last, jnp.concatenate([kh] * reps, axis=1),
                                 pltpu.roll(k_t, win - 1, 1))
    s = jnp.concatenate(s_rows, axis=0) * scale
    s0 = jnp.concatenate(s0_rows, axis=0) * scale
    m = jnp.maximum(jnp.max(jnp.where(valid, s, NEG_BIG), axis=-1, keepdims=True), s0)
    e = jnp.where(valid, jnp.exp(s - m), 0.0) * mult
    e0 = len(DIL_PATTERNS) * jnp.exp(s0 - m)
    denom = jnp.sum(e, axis=-1, keepdims=True) + e0

    cols = []
    for h in range(ATT_HEADS):
        lo = h * HEAD_DIM
        v_t = vc_ref[0, h]
        vh = vb[lo:lo + HEAD_DIM, :]
        num = jnp.sum(v_t * e[h:h + 1, :], axis=1, keepdims=True) + e0[h:h + 1, :] * vh[:, 0:1]
        cols.append(num / denom[h:h + 1, :])
        vo_ref[0, h] = jnp.where(is_last, jnp.concatenate([vh] * reps, axis=1),
                                 pltpu.roll(v_t, win - 1, 1))
    o_col = jnp.concatenate(cols, axis=0)
    lane_n = lax.broadcasted_iota(I32, (1, n_s), 1) == n
    acct[...] = jnp.where(lane_n, o_col, acct[...])

    @pl.when(n == pl.num_programs(0) - 1)
    def _():
        att_ref[...] = acct[...].T


def _win_sample(att_o, ak, av, cache_k, cache_v):
    n_s, n_h, hd, win = cache_k.shape
    assert win >= max(w for w, _ in DIL_PATTERNS)
    t_all = att_o.shape[0]
    last = t_all // n_s - 1
    rows = pl.BlockSpec((n_s, ATT_WIDTH), lambda n: (last, 0))
    cblk = pl.BlockSpec((1, n_h, hd, win), lambda n: (n, 0, 0, 0))
    cshape = jax.ShapeDtypeStruct(cache_k.shape, F32)
    return pl.pallas_call(
        functools.partial(_win_sample_kernel, win=win),
        grid=(n_s,),
        in_specs=[rows, rows, rows, cblk, cblk],
        out_specs=[cblk, cblk, rows],
        out_shape=[cshape, cshape, jax.ShapeDtypeStruct(att_o.shape, F32)],
        scratch_shapes=[pltpu.VMEM((ATT_WIDTH, n_s), F32)] * 4,
        input_output_aliases={0: 2},
        compiler_params=_cparams(1),
        name="win_sample",
    )(att_o, ak, av, cache_k, cache_v)


def _lane_tile_norm(t, g, n_tiles):
    outs = []
    for h in range(n_tiles):
        outs.append(_rms(t[:, h * LANES:(h + 1) * LANES], g))
    return jnp.concatenate(outs, axis=-1)


def _mem_kv_kernel(mem_ref, gm_ref, w_ref, gk_ref, k_ref, v_ref):
    xn = _rms(mem_ref[0], gm_ref[...]).astype(BF16)
    kv = _dot(xn, w_ref[...])
    k_ref[0] = _lane_tile_norm(kv[:, :X_WIDTH], gk_ref[...], X_HEADS)
    v_ref[0] = kv[:, X_WIDTH:]


def _mem_kv(mem, g_mem, w_kv, g_xk):
    n, m, _ = mem.shape
    out = jax.ShapeDtypeStruct((n, m, X_WIDTH), F32)
    return pl.pallas_call(
        _mem_kv_kernel,
        grid=(n,),
        in_specs=[pl.BlockSpec((1, m, D_MODEL), lambda i: (i, 0, 0)),
                  pl.BlockSpec((1, D_MODEL), lambda i: (0, 0)),
                  pl.BlockSpec((D_MODEL, 2 * X_WIDTH), lambda i: (0, 0)),
                  pl.BlockSpec((1, X_HEAD_DIM), lambda i: (0, 0))],
        out_specs=[pl.BlockSpec((1, m, X_WIDTH), lambda i: (i, 0, 0))] * 2,
        out_shape=[out, out],
        compiler_params=_cparams(1),
        name="mem_kv",
    )(mem, g_mem, w_kv, g_xk)


def _mix_out_kernel(ret_ref, att_ref, x_ref, wo_ref, g2_ref, wq_ref, gq_ref, h_ref, q_ref):
    mixed = jnp.concatenate([ret_ref[...], att_ref[...]], axis=-1).astype(BF16)
    h = x_ref[...] + _dot(mixed, wo_ref[...])
    h_ref[...] = h
    q = _dot(_rms(h, g2_ref[...]).astype(BF16), wq_ref[...])
    q_ref[...] = _lane_tile_norm(q, gq_ref[...], X_HEADS)


def _mix_out(ret_n, att_o, x_all, w_out, g2, w_qx, g_xq):
    t_all = x_all.shape[0]
    tm = TOKEN_TILE
    row = lambda i: (i, 0)
    fixed = lambda i: (0, 0)
    return pl.pallas_call(
        _mix_out_kernel,
        grid=(t_all // tm,),
        in_specs=[pl.BlockSpec((tm, RET_WIDTH), row),
                  pl.BlockSpec((tm, ATT_WIDTH), row),
                  pl.BlockSpec((tm, D_MODEL), row),
                  pl.BlockSpec((D_MODEL, D_MODEL), fixed),
                  pl.BlockSpec((1, D_MODEL), fixed),
                  pl.BlockSpec((D_MODEL, X_WIDTH), fixed),
                  pl.BlockSpec((1, X_HEAD_DIM), fixed)],
        out_specs=[pl.BlockSpec((tm, D_MODEL), row), pl.BlockSpec((tm, X_WIDTH), row)],
        out_shape=[jax.ShapeDtypeStruct((t_all, D_MODEL), F32),
                   jax.ShapeDtypeStruct((t_all, X_WIDTH), F32)],
        compiler_params=_cparams(1),
        name="mix_out",
    )(ret_n, att_o, x_all, w_out, g2, w_qx, g_xq)


def _xattn_prompt_kernel(q_ref, k_ref, v_ref, o_ref):
    q = q_ref[...]
    k = k_ref[0]
    v = v_ref[0]
    scale = X_HEAD_DIM ** -0.5
    outs = []
    for h in range(X_HEADS):
        sl = slice(h * LANES, (h + 1) * LANES)
        s = _dot_nt(q[:, sl].astype(BF16), k[:, sl].astype(BF16)) * scale
        p = jnp.exp(s - jnp.max(s, axis=-1, keepdims=True))
        o = _dot(p.astype(BF16), v[:, sl].astype(BF16))
        outs.append(o / jnp.sum(p, axis=-1, keepdims=True))
    o_ref[...] = jnp.concatenate(outs, axis=-1)


def _xattn_prompt(qx, mk, mv, n_batch, seq, tq=512):
    t_all = qx.shape[0]
    per = seq // tq
    m = mk.shape[1]
    rows = pl.BlockSpec((tq, X_WIDTH), lambda n, i: (n * per + i, 0))
    mem = pl.BlockSpec((1, m, X_WIDTH), lambda n, i: (n, 0, 0))
    return pl.pallas_call(
        _xattn_prompt_kernel,
        grid=(n_batch, per),
        in_specs=[rows, mem, mem],
        out_specs=rows,
        out_shape=jax.ShapeDtypeStruct((t_all, X_WIDTH), F32),
        input_output_aliases={0: 0},
        compiler_params=_cparams(2),
        name="xattn_prompt",
    )(qx, mk, mv)


def _xattn_sample_kernel(q_ref, k_ref, v_ref, o_ref, *, group, n_mem):
    i = pl.program_id(0)
    scale = X_HEAD_DIM ** -0.5
    for j in range(group):
        n = i * group + j
        q = q_ref[pl.ds(n, 1), :]
        outs = []
        for h in range(X_HEADS):
            rows = pl.ds(h, n_mem, stride=X_HEADS)
            qh = q[:, h * LANES:(h + 1) * LANES]
            s = jnp.sum(k_ref[j, rows, :] * qh, axis=-1, keepdims=True) * scale
            p = jnp.exp(s - jnp.max(s, axis=0, keepdims=True))
            o = jnp.sum(p * v_ref[j, rows, :], axis=0, keepdims=True)
            outs.append(o / jnp.sum(p, axis=0, keepdims=True))
        o_ref[pl.ds(n, 1), :] = jnp.concatenate(outs, axis=-1)


XATTN_SAMPLE_GROUP = 8


def _xattn_sample(o_all, mk, mv):
    n_s, mh, _ = mk.shape
    t_all = o_all.shape[0]
    last = t_all // n_s - 1
    group = XATTN_SAMPLE_GROUP
    rows = pl.BlockSpec((n_s, X_WIDTH), lambda n: (last, 0))
    mem = pl.BlockSpec((group, mh, X_HEAD_DIM), lambda n: (n, 0, 0))
    return pl.pallas_call(
        functools.partial(_xattn_sample_kernel, group=group, n_mem=mh // X_HEADS),
        grid=(n_s // group,),
        in_specs=[rows, mem, mem],
        out_specs=rows,
        out_shape=jax.ShapeDtypeStruct(o_all.shape, F32),
        input_output_aliases={0: 0},
        compiler_params=_cparams(1),
        name="xattn_sample",
    )(o_all, mk, mv)


ROW_TILE_ROWS = D_MODEL // LANES


def _store_row_tiles(ref, x):
    rows = x.shape[0]
    for c in range(ROW_TILE_ROWS):
        ref[pl.ds(c, rows, stride=ROW_TILE_ROWS), :] = x[:, c * LANES:(c + 1) * LANES]


def _load_row_tiles(ref, rows):
    return jnp.concatenate([ref[pl.ds(c, rows, stride=ROW_TILE_ROWS), :]
                            for c in range(ROW_TILE_ROWS)], axis=1)


def _xout_router_kernel(o_ref, h_ref, wo_ref, g3_ref, wr_ref, br_ref,
                        h2_ref, xn_ref, idx_ref, gate_ref):
    h2 = h_ref[...] + _dot(o_ref[...].astype(BF16), wo_ref[...])
    h2_ref[...] = h2
    xn = _rms(h2, g3_ref[...])
    _store_row_tiles(xn_ref, xn)
    logits = lax.dot_general(wr_ref[...], xn, (((1,), (1,)), ((), ())),
                             precision=lax.Precision.HIGHEST,
                             preferred_element_type=F32) + br_ref[...]
    eid = lax.broadcasted_iota(I32, logits.shape, 0)
    work = logits
    vals, idxs = [], []
    for _ in range(TOP_K):
        mx = jnp.max(work, axis=0, keepdims=True)
        ix = jnp.min(jnp.where(work == mx, eid, N_EXPERTS), axis=0, keepdims=True)
        vals.append(mx)
        idxs.append(ix)
        work = jnp.where(eid == ix, -jnp.inf, work)
    ex = [jnp.exp(v - vals[0]) for v in vals]
    tot = ex[0] + ex[1] + ex[2] + ex[3]
    idx_ref[...] = jnp.concatenate(idxs, axis=0)
    gate_ref[...] = jnp.concatenate([e / tot for e in ex], axis=0)


def _xout_router(o_all, h_all, w_ox, g3, w_rt, b_r):
    t_all = h_all.shape[0]
    tm = TOKEN_TILE
    row = lambda i: (i, 0)
    fixed = lambda i: (0, 0)
    colblk = lambda i: (0, i)
    return pl.pallas_call(
        _xout_router_kernel,
        grid=(t_all // tm,),
        in_specs=[pl.BlockSpec((tm, X_WIDTH), row),
                  pl.BlockSpec((tm, D_MODEL), row),
                  pl.BlockSpec((X_WIDTH, D_MODEL), fixed),
                  pl.BlockSpec((1, D_MODEL), fixed),
                  pl.BlockSpec((N_EXPERTS, D_MODEL), fixed),
                  pl.BlockSpec((N_EXPERTS, 1), fixed)],
        out_specs=[pl.BlockSpec((tm, D_MODEL), row),
                   pl.BlockSpec((tm * ROW_TILE_ROWS, LANES), row),
                   pl.BlockSpec((TOP_K, tm), colblk), pl.BlockSpec((TOP_K, tm), colblk)],
        out_shape=[jax.ShapeDtypeStruct((t_all, D_MODEL), F32),
                   jax.ShapeDtypeStruct((t_all * ROW_TILE_ROWS, LANES), F32),
                   jax.ShapeDtypeStruct((TOP_K, t_all), I32),
                   jax.ShapeDtypeStruct((TOP_K, t_all), F32)],
        compiler_params=_cparams(1),
        name="xout_router",
    )(o_all, h_all, w_ox, g3, w_rt, b_r)


def _route_kernel(idx_ref, dest_ref, be_ref, nu_ref, *, t_all, n_blocks_pad):
    bm = MOE_BLOCK_ROWS
    nt = t_all // LANES
    e_col = lax.broadcasted_iota(I32, (N_EXPERTS, 1), 0)
    hp = lax.Precision.HIGHEST

    def multi_hot(j):
        blk = idx_ref[:, pl.ds(pl.multiple_of(j * LANES, LANES), LANES)]
        mh = jnp.zeros((N_EXPERTS, LANES), F32)
        for k in range(TOP_K):
            mh = mh + (e_col == blk[k:k + 1, :]).astype(F32)
        return blk, mh

    def count_body(j, c):
        _, mh = multi_hot(j)
        return c + jnp.sum(mh, axis=1, keepdims=True)

    counts = lax.fori_loop(0, nt, count_body, jnp.zeros((N_EXPERTS, 1), F32))
    padded = jnp.ceil(counts * (1.0 / bm)) * bm
    tri = (lax.broadcasted_iota(I32, (N_EXPERTS, N_EXPERTS), 1)
           <= lax.broadcasted_iota(I32, (N_EXPERTS, N_EXPERTS), 0)).astype(F32)
    pad_end = jnp.dot(tri, jnp.broadcast_to(padded, (N_EXPERTS, LANES)), precision=hp,
                      preferred_element_type=F32)
    pad_start = pad_end[:, 0:1] - padded
    upper = (lax.broadcasted_iota(I32, (LANES, LANES), 0)
             < lax.broadcasted_iota(I32, (LANES, LANES), 1)).astype(BF16)

    def dest_body(j, carry):
        blk, mh = multi_hot(j)
        rank = carry + _dot(mh.astype(BF16), upper)
        base = pad_start + rank
        for k in range(TOP_K):
            d = jnp.sum(jnp.where(e_col == blk[k:k + 1, :], base, 0.0), axis=0, keepdims=True)
            dest_ref[pl.ds(k, 1), pl.ds(pl.multiple_of(j * LANES, LANES), LANES)] = d.astype(I32)
        return carry + jnp.sum(mh, axis=1, keepdims=True)

    lax.fori_loop(0, nt, dest_body, jnp.zeros((N_EXPERTS, 1), F32))

    b_row = lax.broadcasted_iota(I32, (1, n_blocks_pad), 1).astype(F32) * bm
    be = jnp.sum((pad_end[:, 0:1] <= b_row).astype(F32), axis=0, keepdims=True)
    be_ref[...] = jnp.minimum(be, N_EXPERTS - 1.0).astype(I32)
    nu_ref[...] = (pad_end[N_EXPERTS - 1:N_EXPERTS, :] * (1.0 / bm)).astype(I32)


def _route(idx_t, n_blocks_pad):
    t_all = idx_t.shape[1]
    return pl.pallas_call(
        functools.partial(_route_kernel, t_all=t_all, n_blocks_pad=n_blocks_pad),
        out_shape=[jax.ShapeDtypeStruct((TOP_K, t_all), I32),
                   jax.ShapeDtypeStruct((1, n_blocks_pad), I32),
                   jax.ShapeDtypeStruct((1, LANES), I32)],
        compiler_params=pltpu.CompilerParams(vmem_limit_bytes=VMEM_LIMIT),
        name="moe_route",
    )(idx_t)


def _dispatch_kernel(dest_sm, x_ref, xs_in, xs_ref, sem, *, t_all):
    del xs_in
    i = pl.program_id(0)
    tm = x_ref.shape[0]

    def start_body(j, c):
        for k in range(TOP_K):
            d = dest_sm[k * t_all + i * tm + j]
            pltpu.make_async_copy(x_ref.at[j], xs_ref.at[d], sem).start()
        return c

    lax.fori_loop(0, tm, start_body, 0, unroll=ROW_DMA_UNROLL)
    for k in range(TOP_K):
        pltpu.make_async_copy(x_ref, xs_ref.at[pl.ds(0, tm)], sem).wait()


ROW_DMA_UNROLL = 4


def _dispatch(dest_flat, xn_all, xs_init):
    t_all = xn_all.shape[0]
    tm = ROW_TILE
    return pl.pallas_call(
        functools.partial(_dispatch_kernel, t_all=t_all),
        grid_spec=pltpu.PrefetchScalarGridSpec(
            num_scalar_prefetch=1,
            grid=(t_all // tm,),
            in_specs=[pl.BlockSpec((tm, ROW_TILE_ROWS, LANES), lambda i, d: (i, 0, 0)),
                      pl.BlockSpec(memory_space=pl.ANY)],
            out_specs=pl.BlockSpec(memory_space=pl.ANY),
            scratch_shapes=[pltpu.SemaphoreType.DMA]),
        out_shape=jax.ShapeDtypeStruct(xs_init.shape, F32),
        input_output_aliases={2: 0},
        compiler_params=_cparams(1),
        name="moe_dispatch",
    )(dest_flat, xn_all, xs_init)


def _expert_kernel(be_sm, nu_sm, x_ref, wgu_ref, bgu_ref, wd_ref, bd_ref, y_ref, wgu_bf, wd_bf):
    b = pl.program_id(0)
    changed = jnp.logical_or(b == 0, be_sm[b] != be_sm[jnp.maximum(b - 1, 0)])

    @pl.when(changed)
    def _():
        wgu_bf[...] = wgu_ref[0].astype(BF16)
        wd_bf[...] = wd_ref[0].astype(BF16)

    @pl.when(b < nu_sm[0])
    def _():
        x = _load_row_tiles(x_ref, MOE_BLOCK_ROWS).astype(BF16)
        h = _dot(x, wgu_bf[...]) + bgu_ref[0]
        glu = jnp.minimum(h[:, :D_FF], SWIGLU_LIMIT)
        lin = jnp.clip(h[:, D_FF:], -SWIGLU_LIMIT, SWIGLU_LIMIT)
        act = glu * _sigmoid(SWIGLU_ALPHA * glu) * (lin + 1.0)
        _store_row_tiles(y_ref, _dot(act.astype(BF16), wd_bf[...]) + bd_ref[0])

    @pl.when(b >= nu_sm[0])
    def _():
        y_ref[...] = jnp.zeros_like(y_ref)


def _experts(block_e, n_used, xs, w_gu, b_gu, w_down, b_down):
    cap = xs.shape[0] // ROW_TILE_ROWS
    bm = MOE_BLOCK_ROWS
    return pl.pallas_call(
        _expert_kernel,
        grid_spec=pltpu.PrefetchScalarGridSpec(
            num_scalar_prefetch=2,
            grid=(cap // bm,),
            in_specs=[pl.BlockSpec((bm * ROW_TILE_ROWS, LANES), lambda b, be, nu: (b, 0)),
                      pl.BlockSpec((1, D_MODEL, 2 * D_FF), lambda b, be, nu: (be[b], 0, 0)),
                      pl.BlockSpec((1, 1, 2 * D_FF), lambda b, be, nu: (be[b], 0, 0)),
                      pl.BlockSpec((1, D_FF, D_MODEL), lambda b, be, nu: (be[b], 0, 0)),
                      pl.BlockSpec((1, 1, D_MODEL), lambda b, be, nu: (be[b], 0, 0))],
            out_specs=pl.BlockSpec((bm * ROW_TILE_ROWS, LANES), lambda b, be, nu: (b, 0)),
            scratch_shapes=[pltpu.VMEM((D_MODEL, 2 * D_FF), BF16),
                            pltpu.VMEM((D_FF, D_MODEL), BF16)]),
        out_shape=jax.ShapeDtypeStruct((cap * ROW_TILE_ROWS, LANES), F32),
        compiler_params=_cparams(1),
        name="moe_experts",
    )(block_e, n_used, xs, w_gu, b_gu, w_down, b_down)


def _combine_kernel(dest_sm, yb_ref, h_ref, g_ref, op_ref, os_ref, buf, sem, *, t_all):
    i = pl.program_id(0)
    tm = h_ref.shape[0]

    def start_body(j, c):
        for k in range(TOP_K):
            d = dest_sm[k * t_all + i * tm + j]
            tile = pl.ds(pl.multiple_of(j * ROW_TILE_ROWS, ROW_TILE_ROWS), ROW_TILE_ROWS)
            pltpu.make_async_copy(yb_ref.at[d], buf.at[k, tile], sem).start()
        return c

    lax.fori_loop(0, tm, start_body, 0, unroll=ROW_DMA_UNROLL)
    for k in range(TOP_K):
        pltpu.make_async_copy(buf.at[k], buf.at[k], sem).wait()

    g = g_ref[...]
    y = h_ref[...]
    for k in range(TOP_K):
        y = y + g[:, k:k + 1] * _load_row_tiles(buf.at[k], tm)
    last = pl.num_programs(0) - 1

    @pl.when(i < last)
    def _():
        op_ref[...] = y

    @pl.when(i == last)
    def _():
        os_ref[...] = y


def _combine(dest_flat, yb, h2_all, gates, n_prompt_rows):
    t_all = h2_all.shape[0]
    tm = ROW_TILE
    n_p_tiles = n_prompt_rows // tm
    assert t_all == n_prompt_rows + tm
    return pl.pallas_call(
        functools.partial(_combine_kernel, t_all=t_all),
        grid_spec=pltpu.PrefetchScalarGridSpec(
            num_scalar_prefetch=1,
            grid=(t_all // tm,),
            in_specs=[pl.BlockSpec(memory_space=pl.ANY),
                      pl.BlockSpec((tm, D_MODEL), lambda i, d: (i, 0)),
                      pl.BlockSpec((tm, TOP_K), lambda i, d: (i, 0))],
            out_specs=[pl.BlockSpec((tm, D_MODEL), lambda i, d: (jnp.minimum(i, n_p_tiles - 1), 0)),
                       pl.BlockSpec((tm, D_MODEL), lambda i, d: (0, 0))],
            scratch_shapes=[pltpu.VMEM((TOP_K, tm * ROW_TILE_ROWS, LANES), F32),
                            pltpu.SemaphoreType.DMA]),
        out_shape=[jax.ShapeDtypeStruct((n_prompt_rows, D_MODEL), F32),
                   jax.ShapeDtypeStruct((tm, D_MODEL), F32)],
        compiler_params=_cparams(1),
        name="moe_combine",
    )(dest_flat, yb, h2_all, gates)


def _rope_tables(pos):
    half = HEAD_DIM // 2
    inv = jnp.exp(-math.log(ROPE_THETA) * jnp.arange(half, dtype=F32) / half)
    ang = pos.astype(F32)[:, None] * inv[None, :]
    cos, sin = jnp.cos(ang), jnp.sin(ang)
    cos_t = jnp.concatenate([cos, cos] * HEADS_PER_TILE, axis=-1)
    sin_t = jnp.concatenate([-sin, sin] * HEADS_PER_TILE, axis=-1)
    return cos_t, sin_t


def _block_diag_ones(n, blk):
    r = jnp.arange(n) // blk
    return (r[:, None] == r[None, :]).astype(BF16)


def _layer(x_prompt, x_sample, state_ret, cache_win_k, cache_win_v, cache_mem_k, cache_mem_v,
           mem_prompt, g_norm1, w_in, g_att_q, g_att_k, g_ret_gn, w_out, g_norm2, g_mem,
           w_q_x, w_kv_x, g_x_q, g_x_k, w_o_x, g_norm3, w_router, b_router, w_gu, b_gu,
           w_down, b_down):
    n_b, seq, _ = x_prompt.shape
    n_s = x_sample.shape[0]
    assert x_sample.shape[1] == 1 and n_s == ROW_TILE
    t_p = n_b * seq
    t_all = t_p + n_s
    assert t_all % TOKEN_TILE == 0 and t_p % ROW_TILE == 0

    x_all = jnp.concatenate([x_prompt.reshape(t_p, D_MODEL), x_sample.reshape(n_s, D_MODEL)], axis=0)
    pos = jnp.concatenate([jnp.tile(jnp.arange(seq, dtype=jnp.int32), n_b),
                           jnp.full((n_s,), PAST_LEN, jnp.int32)])
    cos_t, sin_t = _rope_tables(pos)
    log_g = jnp.log1p(-jnp.exp2(-5.0 - jnp.arange(RET_HEADS, dtype=F32)))
    lg_row = jnp.repeat(log_g, HEAD_DIM)[None, :]
    gn_row = g_ret_gn.reshape(1, RET_WIDTH)
    gn_col = g_ret_gn.reshape(RET_WIDTH, 1)
    gq = jnp.tile(g_att_q.reshape(1, HEAD_DIM), (1, ATT_HEADS))
    gk = jnp.tile(g_att_k.reshape(1, HEAD_DIM), (1, ATT_HEADS))
    seg = _block_diag_ones(ATT_WIDTH, HEAD_DIM)

    rq, rk, rv, rg, aq, ak, av = _in_proj(
        x_all, g_norm1.reshape(1, D_MODEL), w_in.astype(BF16), cos_t, sin_t, gq, gk, seg)

    ret_n, st_p = _ret_prompt(rq, rk, rv, rg, gn_row, lg_row, n_b, seq)
    att_o, wk_p, wv_p = _dil_prompt(aq, ak, av, n_b, seq)
    state_t = jnp.transpose(state_ret, (1, 2, 3, 0))
    ret_n, st_s = _ret_sample(rq, rk, rv, ret_n, gn_col, lg_row, state_t)
    ck = jnp.transpose(cache_win_k, (0, 2, 3, 1))
    cv = jnp.transpose(cache_win_v, (0, 2, 3, 1))
    wk_s, wv_s, att_o = _win_sample(att_o, ak, av, ck, cv)

    h_all, qx = _mix_out(ret_n, att_o, x_all, w_out.astype(BF16), g_norm2.reshape(1, D_MODEL),
                         w_q_x.astype(BF16), g_x_q.reshape(1, X_HEAD_DIM))

    mk_p, mv_p = _mem_kv(mem_prompt, g_mem.reshape(1, D_MODEL), w_kv_x.astype(BF16),
                         g_x_k.reshape(1, X_HEAD_DIM))
    o_all = _xattn_prompt(qx, mk_p, mv_p, n_b, seq)
    n_mem = cache_mem_k.shape[1]
    o_all = _xattn_sample(o_all, cache_mem_k.reshape(n_s, n_mem * X_HEADS, X_HEAD_DIM),
                          cache_mem_v.reshape(n_s, n_mem * X_HEADS, X_HEAD_DIM))

    h2_all, xn_all, idx_t, gate_t = _xout_router(
        o_all, h_all, w_o_x.astype(BF16), g_norm3.reshape(1, D_MODEL),
        jnp.transpose(w_router), b_router.reshape(N_EXPERTS, 1))

    bm = MOE_BLOCK_ROWS
    n_blocks = -(-(t_all * TOP_K) // bm) + N_EXPERTS
    n_blocks_pad = -(-n_blocks // LANES) * LANES
    dest_t, be, nu = _route(idx_t, n_blocks_pad)
    dest_flat = dest_t.reshape(TOP_K * t_all)
    cap = n_blocks * bm
    xs = _dispatch(dest_flat, xn_all.reshape(t_all, ROW_TILE_ROWS, LANES),
                   jnp.zeros((cap, ROW_TILE_ROWS, LANES), F32))
    yb = _experts(be.reshape(n_blocks_pad), nu.reshape(LANES), xs.reshape(cap * ROW_TILE_ROWS, LANES),
                  w_gu, b_gu.reshape(N_EXPERTS, 1, 2 * D_FF), w_down,
                  b_down.reshape(N_EXPERTS, 1, D_MODEL))
    y_p, y_s = _combine(dest_flat, yb.reshape(cap, ROW_TILE_ROWS, LANES), h2_all,
                        jnp.transpose(gate_t), t_p)

    st_p = jnp.stack([st_p[:, :, :HEAD_DIM, :HEAD_DIM], st_p[:, :, HEAD_DIM:, HEAD_DIM:]], axis=2)
    st_p = st_p.reshape(n_b, RET_HEADS, HEAD_DIM, HEAD_DIM)
    return (y_p.reshape(n_b, seq, D_MODEL),
            y_s.reshape(n_s, 1, D_MODEL),
            st_p,
            jnp.transpose(st_s, (3, 0, 1, 2)),
            jnp.transpose(wk_p, (0, 3, 1, 2)),
            jnp.transpose(wv_p, (0, 3, 1, 2)),
            jnp.transpose(wk_s, (0, 3, 1, 2)),
            jnp.transpose(wv_s, (0, 3, 1, 2)),
            mk_p.reshape(n_b, n_mem, X_HEADS, X_HEAD_DIM),
            mv_p.reshape(n_b, n_mem, X_HEADS, X_HEAD_DIM))


def kernel(x_prompt, x_sample, state_ret, cache_win_k, cache_win_v, cache_mem_k, cache_mem_v,
           mem_prompt, g_norm1, w_in, g_att_q, g_att_k, g_ret_gn, w_out, g_norm2, g_mem,
           w_q_x, w_kv_x, g_x_q, g_x_k, w_o_x, g_norm3, w_router, b_router, w_gu, b_gu,
           w_down, b_down):
    assert state_ret.shape[0] == 1, "single-layer trunk"
    outs = _layer(x_prompt, x_sample, state_ret[0], cache_win_k[0], cache_win_v[0],
                  cache_mem_k[0], cache_mem_v[0], mem_prompt, g_norm1[0], w_in[0], g_att_q[0],
                  g_att_k[0], g_ret_gn[0], w_out[0], g_norm2[0], g_mem[0], w_q_x[0], w_kv_x[0],
                  g_x_q[0], g_x_k[0], w_o_x[0], g_norm3[0], w_router[0], b_router[0], w_gu[0],
                  b_gu[0], w_down[0], b_down[0])
    y_p, y_s = outs[0], outs[1]
    return (y_p, y_s) + tuple(o[None] for o in outs[2:])
```

```python
import functools
import math

import jax
import jax.numpy as jnp
from jax import lax
from jax.experimental import pallas as pl
from jax.experimental.pallas import tpu as pltpu

F32 = jnp.float32
BF16 = jnp.bfloat16
I32 = jnp.int32

D_MODEL = 1024
HEAD_DIM = 64
RET_HEADS = 8
ATT_HEADS = 8
RET_WIDTH = RET_HEADS * HEAD_DIM
ATT_WIDTH = ATT_HEADS * HEAD_DIM
IN_COLS = 4 * RET_WIDTH + 3 * ATT_WIDTH
RET_CHUNK = 128
DIL_PATTERNS = ((128, 1), (512, 4), (2048, 16))
DIL_BLOCK = 128
MAX_WINDOW = 2048
X_HEADS = 4
X_HEAD_DIM = 128
X_WIDTH = X_HEADS * X_HEAD_DIM
N_EXPERTS = 32
TOP_K = 4
D_FF = D_MODEL
SWIGLU_LIMIT = 7.0
SWIGLU_ALPHA = 1.702
ROPE_THETA = 10000.0
EPS = 1e-6
PAST_LEN = 8192

LANES = 128
SUBLANES = 8
HEADS_PER_TILE = LANES // HEAD_DIM

TOKEN_TILE = 384
ROW_TILE = 128
MOE_BLOCK_ROWS = 512
NEG_BIG = -1e30
VMEM_LIMIT = 48 * 1024 * 1024
VMEM_LIMIT_EXPERTS = 58 * 1024 * 1024


def _cparams(n_axes, vmem_limit=VMEM_LIMIT):
    return pltpu.CompilerParams(
        dimension_semantics=("arbitrary",) * n_axes, vmem_limit_bytes=vmem_limit)


def _rms(x, g):
    return x * lax.rsqrt(jnp.mean(x * x, axis=-1, keepdims=True) + EPS) * g


def _dot(a, b):
    return jnp.dot(a, b, preferred_element_type=F32)


def _dot_nt(a, b):
    return lax.dot_general(a, b, (((1,), (1,)), ((), ())), preferred_element_type=F32)


def _dot_tn(a, b):
    return lax.dot_general(a, b, (((0,), (0,)), ((), ())), preferred_element_type=F32)


def _sigmoid(x):
    return 1.0 / (1.0 + jnp.exp(-x))


def _in_proj_kernel(x_ref, g1_ref, w_ref, cos_ref, sin_ref, gq_ref, gk_ref, seg_ref,
                    rq_ref, rk_ref, rv_ref, rg_ref, aq_ref, ak_ref, av_ref):
    xn = _rms(x_ref[...], g1_ref[...]).astype(BF16)
    proj = _dot(xn, w_ref[...])
    reps = RET_WIDTH // LANES
    cos = jnp.concatenate([cos_ref[...]] * reps, axis=-1)
    sin = jnp.concatenate([sin_ref[...]] * reps, axis=-1)
    lane = lax.broadcasted_iota(I32, (1, RET_WIDTH), 1)
    first_half = (lane % HEAD_DIM) < (HEAD_DIM // 2)
    seg = seg_ref[...]

    def rope(t):
        partner = jnp.where(first_half,
                            pltpu.roll(t, RET_WIDTH - HEAD_DIM // 2, 1),
                            pltpu.roll(t, HEAD_DIM // 2, 1))
        return t * cos + partner * sin

    def head_norm(t, g):
        sq = t * t
        hi = sq.astype(BF16)
        lo = (sq - hi.astype(F32)).astype(BF16)
        ssum = _dot(hi, seg) + _dot(lo, seg)
        return t * lax.rsqrt(ssum * (1.0 / HEAD_DIM) + EPS) * g

    w = RET_WIDTH
    rq_ref[...] = rope(proj[:, 0:w])
    rk_ref[...] = rope(proj[:, w:2 * w]) * (HEAD_DIM ** -0.5)
    rv_ref[...] = proj[:, 2 * w:3 * w]
    rg_ref[...] = proj[:, 3 * w:4 * w]
    aq_ref[...] = rope(head_norm(proj[:, 4 * w:5 * w], gq_ref[...]))
    ak_ref[...] = rope(head_norm(proj[:, 5 * w:6 * w], gk_ref[...]))
    av_ref[...] = proj[:, 6 * w:7 * w]


def _in_proj(x_all, g1, w_in, cos_t, sin_t, gq, gk, seg):
    t_all = x_all.shape[0]
    tm = TOKEN_TILE
    row = lambda i: (i, 0)
    fixed = lambda i: (0, 0)
    out = jax.ShapeDtypeStruct((t_all, RET_WIDTH), F32)
    return pl.pallas_call(
        _in_proj_kernel,
        grid=(t_all // tm,),
        in_specs=[pl.BlockSpec((tm, D_MODEL), row),
                  pl.BlockSpec((1, D_MODEL), fixed),
                  pl.BlockSpec((D_MODEL, IN_COLS), fixed),
                  pl.BlockSpec((tm, LANES), row),
                  pl.BlockSpec((tm, LANES), row),
                  pl.BlockSpec((1, ATT_WIDTH), fixed),
                  pl.BlockSpec((1, ATT_WIDTH), fixed),
                  pl.BlockSpec((ATT_WIDTH, ATT_WIDTH), fixed)],
        out_specs=[pl.BlockSpec((tm, RET_WIDTH), row)] * 7,
        out_shape=[out] * 7,
        compiler_params=_cparams(1),
        name="in_proj",
    )(x_all, g1, w_in, cos_t, sin_t, gq, gk, seg)


def _group_norm_gate(o, mask_a, gn, gate):
    inv = 1.0 / HEAD_DIM
    sa = jnp.sum(jnp.where(mask_a, o, 0.0), axis=-1, keepdims=True)
    sb = jnp.sum(jnp.where(mask_a, 0.0, o), axis=-1, keepdims=True)
    cen = o - jnp.where(mask_a, sa, sb) * inv
    c2 = cen * cen
    va = jnp.sum(jnp.where(mask_a, c2, 0.0), axis=-1, keepdims=True)
    vb = jnp.sum(jnp.where(mask_a, 0.0, c2), axis=-1, keepdims=True)
    var = jnp.where(mask_a, va, vb) * inv
    return cen * lax.rsqrt(var + EPS) * gn * (gate * _sigmoid(gate))


def _ret_prompt_kernel(q_ref, k_ref, v_ref, g_ref, gn_ref, lg_ref, o_ref, st_ref,
                       state, dec, qdec, kdec):
    n = pl.program_id(0)
    c = pl.program_id(1)
    ch = RET_CHUNK
    n_pairs = RET_WIDTH // LANES
    lane = lax.broadcasted_iota(I32, (1, LANES), 1)
    mask_a = lane < HEAD_DIM
    row_i = lax.broadcasted_iota(I32, (ch, 1), 0)

    @pl.when(jnp.logical_and(n == 0, c == 0))
    def _():
        row = row_i.astype(F32)
        col = lax.broadcasted_iota(I32, (1, ch), 1).astype(F32)
        diff = row - col
        causal = diff >= 0.0
        dpos = jnp.maximum(diff, 0.0)
        lg = lg_ref[...]
        for h in range(RET_HEADS):
            dec[h] = jnp.where(causal, jnp.exp(dpos * lg[:, h * HEAD_DIM:h * HEAD_DIM + 1]), 0.0)
        qdec[...] = jnp.exp((row + 1.0) * lg)
        kdec[...] = jnp.exp((ch - 1.0 - row) * lg)

    @pl.when(c == 0)
    def _():
        state[...] = jnp.zeros_like(state)

    same_head = (row_i // HEAD_DIM) == (lane // HEAD_DIM)
    cdec = jnp.exp(ch * lg_ref[...])
    for p in range(n_pairs):
        sl = slice(p * LANES, (p + 1) * LANES)
        q = q_ref[:, sl]
        k = k_ref[:, sl]
        kb = k.astype(BF16)
        vb = v_ref[:, sl].astype(BF16)
        lhs = jnp.concatenate([jnp.where(mask_a, q, 0.0), jnp.where(mask_a, 0.0, q)],
                              axis=0).astype(BF16)
        s = _dot_nt(lhs, kb)
        pr = (s * jnp.concatenate([dec[2 * p], dec[2 * p + 1]], axis=0)).astype(BF16)
        o2 = _dot(pr, vb)
        o_intra = jnp.where(mask_a, o2[:ch], o2[ch:])
        st = state[p]
        o = o_intra + _dot(q.astype(BF16), st.astype(BF16)) * qdec[:, sl]
        upd = _dot_tn((k * kdec[:, sl]).astype(BF16), vb)
        state[p] = cdec[:, sl] * st + jnp.where(same_head, upd, 0.0)
        o_ref[:, sl] = _group_norm_gate(o, mask_a, gn_ref[:, sl], g_ref[:, sl])

    @pl.when(c == pl.num_programs(1) - 1)
    def _():
        st_ref[0] = state[...]


def _ret_prompt(rq, rk, rv, rg, gn_row, lg_row, n_batch, seq):
    t_all = rq.shape[0]
    n_pairs = RET_WIDTH // LANES
    n_chunks = seq // RET_CHUNK
    blk = pl.BlockSpec((RET_CHUNK, RET_WIDTH), lambda n, c: (n * n_chunks + c, 0))
    lane_row = pl.BlockSpec((1, RET_WIDTH), lambda n, c: (0, 0))
    return pl.pallas_call(
        _ret_prompt_kernel,
        grid=(n_batch, n_chunks),
        in_specs=[blk, blk, blk, blk, lane_row, lane_row],
        out_specs=[blk, pl.BlockSpec((1, n_pairs, LANES, LANES), lambda n, c: (n, 0, 0, 0))],
        out_shape=[jax.ShapeDtypeStruct((t_all, RET_WIDTH), F32),
                   jax.ShapeDtypeStruct((n_batch, n_pairs, LANES, LANES), F32)],
        scratch_shapes=[pltpu.VMEM((n_pairs, LANES, LANES), F32),
                        pltpu.VMEM((RET_HEADS, RET_CHUNK, RET_CHUNK), F32),
                        pltpu.VMEM((RET_CHUNK, RET_WIDTH), F32),
                        pltpu.VMEM((RET_CHUNK, RET_WIDTH), F32)],
        input_output_aliases={3: 0},
        compiler_params=_cparams(2),
        name="ret_prompt",
    )(rq, rk, rv, rg, gn_row, lg_row)


DIL_UNROLL = 4


def _dil_prompt_kernel(q_ref, k_ref, v_ref, o_ref, wk_ref, wv_ref, acc, m_s, l_s, *, seq, wp):
    b = DIL_BLOCK
    lane = lax.broadcasted_iota(I32, (1, LANES), 1)
    mask_a = lane < HEAD_DIM
    qi = lax.broadcasted_iota(I32, (2 * b, 1), 0) % b
    kk = lax.broadcasted_iota(I32, (1, 2 * b), 1)
    dist = qi + b - kk
    scale = HEAD_DIM ** -0.5

    for first, (window, dil) in zip((True, False, False), DIL_PATTERNS):
        steps = window // dil
        band = (dist >= 0) & (dist <= steps)
        nb = seq // (b * dil)

        def block(idx, dil=dil, nb=nb, band=band):
            r = idx // nb
            i = idx % nb
            rows_q = pl.ds(r + i * (b * dil), b, stride=dil)
            rows_p = pl.ds(r + jnp.maximum(i - 1, 0) * (b * dil), b, stride=dil)
            q = q_ref[rows_q, :]
            kc = jnp.concatenate([k_ref[rows_p, :], k_ref[rows_q, :]], axis=0).astype(BF16)
            vc = jnp.concatenate([v_ref[rows_p, :], v_ref[rows_q, :]], axis=0).astype(BF16)
            lhs = jnp.concatenate([jnp.where(mask_a, q, 0.0), jnp.where(mask_a, 0.0, q)],
                                  axis=0).astype(BF16)
            s = _dot_nt(lhs, kc) * scale
            valid = band & ((kk >= b) | (jnp.broadcast_to(i, kk.shape) > 0))
            s = jnp.where(valid, s, NEG_BIG)
            m_blk = jnp.max(s, axis=-1, keepdims=True)
            p = jnp.exp(s - m_blk)
            l_blk = jnp.sum(p, axis=-1, keepdims=True)
            pv = _dot(p.astype(BF16), vc)
            return rows_q, m_blk, l_blk, pv

        def merge(rows_q, m_blk, l_blk, pv, first=first):
            w_old, w_blk = [], []
            for hh in range(HEADS_PER_TILE):
                mb = jnp.broadcast_to(m_blk[hh * b:(hh + 1) * b], (b, LANES))
                lb = jnp.broadcast_to(l_blk[hh * b:(hh + 1) * b], (b, LANES))
                if first:
                    m_s[hh, rows_q, :] = mb
                    l_s[hh, rows_q, :] = lb
                else:
                    mo = m_s[hh, rows_q, :]
                    mn = jnp.maximum(mo, mb)
                    a_old = jnp.exp(mo - mn)
                    a_blk = jnp.exp(mb - mn)
                    m_s[hh, rows_q, :] = mn
                    l_s[hh, rows_q, :] = a_old * l_s[hh, rows_q, :] + a_blk * lb
                    w_old.append(a_old)
                    w_blk.append(a_blk)
            if first:
                acc[rows_q, :] = jnp.where(mask_a, pv[:b], pv[b:])
            else:
                acc[rows_q, :] = (jnp.where(mask_a, w_old[0], w_old[1]) * acc[rows_q, :]
                                  + jnp.where(mask_a, w_blk[0] * pv[:b], w_blk[1] * pv[b:]))

        def body(it, carry, block=block, merge=merge):
            parts = [block(it * DIL_UNROLL + u) for u in range(DIL_UNROLL)]
            for part in parts:
                merge(*part)
            return carry

        lax.fori_loop(0, (dil * nb) // DIL_UNROLL, body, 0)

    o_ref[...] = acc[...] / jnp.where(mask_a, l_s[0], l_s[1])

    tchunk = 4 * LANES
    for j in range(wp // tchunk):
        rows = pl.ds(seq - wp + j * tchunk, tchunk)
        cols = slice(j * tchunk, (j + 1) * tchunk)
        kt = k_ref[rows, :].T
        vt = v_ref[rows, :].T
        for hh in range(HEADS_PER_TILE):
            wk_ref[0, hh, :, cols] = kt[hh * HEAD_DIM:(hh + 1) * HEAD_DIM]
            wv_ref[0, hh, :, cols] = vt[hh * HEAD_DIM:(hh + 1) * HEAD_DIM]


def _dil_prompt(aq, ak, av, n_batch, seq):
    t_all = aq.shape[0]
    n_pairs = ATT_WIDTH // LANES
    wp = min(MAX_WINDOW, seq)
    blk = pl.BlockSpec((seq, LANES), lambda n, p: (n, p))
    wblk = pl.BlockSpec((1, HEADS_PER_TILE, HEAD_DIM, wp), lambda n, p: (n, p, 0, 0))
    wshape = jax.ShapeDtypeStruct((n_batch, ATT_HEADS, HEAD_DIM, wp), F32)
    return pl.pallas_call(
        functools.partial(_dil_prompt_kernel, seq=seq, wp=wp),
        grid=(n_batch, n_pairs),
        in_specs=[blk, blk, blk],
        out_specs=[blk, wblk, wblk],
        out_shape=[jax.ShapeDtypeStruct((t_all, ATT_WIDTH), F32), wshape, wshape],
        scratch_shapes=[pltpu.VMEM((seq, LANES), F32),
                        pltpu.VMEM((HEADS_PER_TILE, seq, LANES), F32),
                        pltpu.VMEM((HEADS_PER_TILE, seq, LANES), F32)],
        input_output_aliases={0: 0},
        compiler_params=_cparams(2),
        name="dil_prompt",
    )(aq, ak, av)


def _ret_sample_kernel(q_ref, k_ref, v_ref, g_ref, gn_ref, lg_ref, st_ref,
                       o_ref, nst_ref, qt, kt):
    qt[...] = q_ref[...].T
    kt[...] = k_ref[...].T
    vt = v_ref[...].T
    gt = g_ref[...].T
    lg = lg_ref[...]
    outs = []
    for hh in range(HEADS_PER_TILE):
        lo = hh * HEAD_DIM
        gdec = jnp.exp(lg[:, lo:lo + 1])
        vth = vt[lo:lo + HEAD_DIM, :]

        def body(d, o, hh=hh, lo=lo, gdec=gdec, vth=vth):
            new = gdec * st_ref[hh, d] + kt[pl.ds(lo + d, 1), :] * vth
            nst_ref[hh, d] = new
            return o + qt[pl.ds(lo + d, 1), :] * new

        o = lax.fori_loop(0, HEAD_DIM, body, jnp.zeros_like(vth))
        mu = jnp.mean(o, axis=0, keepdims=True)
        cen = o - mu
        var = jnp.mean(cen * cen, axis=0, keepdims=True)
        gate = gt[lo:lo + HEAD_DIM, :]
        outs.append(cen * lax.rsqrt(var + EPS) * gn_ref[lo:lo + HEAD_DIM, :] * (gate * _sigmoid(gate)))
    o_ref[...] = jnp.concatenate(outs, axis=0).T


def _ret_sample(rq, rk, rv, ret_n, gn_col, lg_row, state_t):
    t_all = rq.shape[0]
    n_s = state_t.shape[-1]
    n_pairs = RET_WIDTH // LANES
    last = t_all // n_s - 1
    blk = pl.BlockSpec((n_s, LANES), lambda p: (last, p))
    st_blk = pl.BlockSpec((HEADS_PER_TILE, HEAD_DIM, HEAD_DIM, n_s), lambda p: (p, 0, 0, 0))
    return pl.pallas_call(
        _ret_sample_kernel,
        grid=(n_pairs,),
        in_specs=[blk, blk, blk, blk,
                  pl.BlockSpec((LANES, 1), lambda p: (p, 0)),
                  pl.BlockSpec((1, LANES), lambda p: (0, p)),
                  st_blk],
        out_specs=[blk, st_blk],
        out_shape=[jax.ShapeDtypeStruct(ret_n.shape, F32),
                   jax.ShapeDtypeStruct(state_t.shape, F32)],
        scratch_shapes=[pltpu.VMEM((LANES, n_s), F32), pltpu.VMEM((LANES, n_s), F32)],
        input_output_aliases={3: 0},
        compiler_params=_cparams(1),
        name="ret_sample",
    )(rq, rk, rv, ret_n, gn_col, lg_row, state_t)


def _win_sample_kernel(aq_ref, akn_ref, avn_ref, kc_ref, vc_ref,
                       ko_ref, vo_ref, att_ref, qt, kt, vt, acct, *, win):
    n = pl.program_id(0)
    n_s = qt.shape[1]

    @pl.when(n == 0)
    def _():
        qt[...] = aq_ref[...].T
        kt[...] = akn_ref[...].T
        vt[...] = avn_ref[...].T
        acct[...] = jnp.zeros_like(acct)

    onehot = (lax.broadcasted_iota(I32, (n_s, LANES), 0) == n).astype(F32)
    hp = lax.Precision.HIGHEST
    qb = jnp.dot(qt[...], onehot, precision=hp, preferred_element_type=F32)
    kb = jnp.dot(kt[...], onehot, precision=hp, preferred_element_type=F32)
    vb = jnp.dot(vt[...], onehot, precision=hp, preferred_element_type=F32)

    w_pos = lax.broadcasted_iota(I32, (1, win), 1)
    back = win - w_pos
    mult = jnp.zeros((1, win), F32)
    for window, dil in DIL_PATTERNS:
        mult = mult + ((back <= window) & (back % dil == 0)).astype(F32)
    valid = mult > 0.0
    is_last = w_pos == win - 1
    reps = win // LANES
    scale = HEAD_DIM ** -0.5

    s_rows, s0_rows = [], []
    for h in range(ATT_HEADS):
        lo = h * HEAD_DIM
        k_t = kc_ref[0, h]
        qh = qb[lo:lo + HEAD_DIM, :]
        kh = kb[lo:lo + HEAD_DIM, :]
        s_rows.append(jnp.sum(k_t * jnp.concatenate([qh] * reps, axis=1), axis=0, keepdims=True))
        s0_rows.append(jnp.sum(qh * kh, axis=0, keepdims=True)[:, 0:1])
        ko_ref[0, h] = jnp.where(is_last, jnp.concatenate([kh] * reps, axis=1),
                                 pltpu.roll(k_t, win - 1, 1))
    s = jnp.concatenate(s_rows, axis=0) * scale
    s0 = jnp.concatenate(s0_rows, axis=0) * scale
    m = jnp.maximum(jnp.max(jnp.where(valid, s, NEG_BIG), axis=-1, keepdims=True), s0)
    e = jnp.where(valid, jnp.exp(s - m), 0.0) * mult
    e0 = len(DIL_PATTERNS) * jnp.exp(s0 - m)
    denom = jnp.sum(e, axis=-1, keepdims=True) + e0

    cols = []
    for h in range(ATT_HEADS):
        lo = h * HEAD_DIM
        v_t = vc_ref[0, h]
        vh = vb[lo:lo + HEAD_DIM, :]
        num = jnp.sum(v_t * e[h:h + 1, :], axis=1, keepdims=True) + e0[h:h + 1, :] * vh[:, 0:1]
        cols.append(num / denom[h:h + 1, :])
        vo_ref[0, h] = jnp.where(is_last, jnp.concatenate([vh] * reps, axis=1),
                                 pltpu.roll(v_t, win - 1, 1))
    o_col = jnp.concatenate(cols, axis=0)
    lane_n = lax.broadcasted_iota(I32, (1, n_s), 1) == n
    acct[...] = jnp.where(lane_n, o_col, acct[...])

    @pl.when(n == pl.num_programs(0) - 1)
    def _():
        att_ref[...] = acct[...].T


def _win_sample(att_o, ak, av, cache_k, cache_v):
    n_s, n_h, hd, win = cache_k.shape
    assert win >= max(w for w, _ in DIL_PATTERNS)
    t_all = att_o.shape[0]
    last = t_all // n_s - 1
    rows = pl.BlockSpec((n_s, ATT_WIDTH), lambda n: (last, 0))
    cblk = pl.BlockSpec((1, n_h, hd, win), lambda n: (n, 0, 0, 0))
    cshape = jax.ShapeDtypeStruct(cache_k.shape, F32)
    return pl.pallas_call(
        functools.partial(_win_sample_kernel, win=win),
        grid=(n_s,),
        in_specs=[rows, rows, rows, cblk, cblk],
        out_specs=[cblk, cblk, rows],
        out_shape=[cshape, cshape, jax.ShapeDtypeStruct(att_o.shape, F32)],
        scratch_shapes=[pltpu.VMEM((ATT_WIDTH, n_s), F32)] * 4,
        input_output_aliases={0: 2},
        compiler_params=_cparams(1),
        name="win_sample",
    )(att_o, ak, av, cache_k, cache_v)


def _lane_tile_norm(t, g, n_tiles):
    outs = []
    for h in range(n_tiles):
        outs.append(_rms(t[:, h * LANES:(h + 1) * LANES], g))
    return jnp.concatenate(outs, axis=-1)


def _mem_kv_kernel(mem_ref, gm_ref, w_ref, gk_ref, k_ref, v_ref):
    xn = _rms(mem_ref[0], gm_ref[...]).astype(BF16)
    kv = _dot(xn, w_ref[...])
    k_ref[0] = _lane_tile_norm(kv[:, :X_WIDTH], gk_ref[...], X_HEADS)
    v_ref[0] = kv[:, X_WIDTH:]


def _mem_kv(mem, g_mem, w_kv, g_xk):
    n, m, _ = mem.shape
    out = jax.ShapeDtypeStruct((n, m, X_WIDTH), F32)
    return pl.pallas_call(
        _mem_kv_kernel,
        grid=(n,),
        in_specs=[pl.BlockSpec((1, m, D_MODEL), lambda i: (i, 0, 0)),
                  pl.BlockSpec((1, D_MODEL), lambda i: (0, 0)),
                  pl.BlockSpec((D_MODEL, 2 * X_WIDTH), lambda i: (0, 0)),
                  pl.BlockSpec((1, X_HEAD_DIM), lambda i: (0, 0))],
        out_specs=[pl.BlockSpec((1, m, X_WIDTH), lambda i: (i, 0, 0))] * 2,
        out_shape=[out, out],
        compiler_params=_cparams(1),
        name="mem_kv",
    )(mem, g_mem, w_kv, g_xk)


def _mix_out_kernel(ret_ref, att_ref, x_ref, wo_ref, g2_ref, wq_ref, gq_ref, h_ref, q_ref):
    mixed = jnp.concatenate([ret_ref[...], att_ref[...]], axis=-1).astype(BF16)
    h = x_ref[...] + _dot(mixed, wo_ref[...])
    h_ref[...] = h
    q = _dot(_rms(h, g2_ref[...]).astype(BF16), wq_ref[...])
    q_ref[...] = _lane_tile_norm(q, gq_ref[...], X_HEADS)


def _mix_out(ret_n, att_o, x_all, w_out, g2, w_qx, g_xq):
    t_all = x_all.shape[0]
    tm = TOKEN_TILE
    row = lambda i: (i, 0)
    fixed = lambda i: (0, 0)
    return pl.pallas_call(
        _mix_out_kernel,
        grid=(t_all // tm,),
        in_specs=[pl.BlockSpec((tm, RET_WIDTH), row),
                  pl.BlockSpec((tm, ATT_WIDTH), row),
                  pl.BlockSpec((tm, D_MODEL), row),
                  pl.BlockSpec((D_MODEL, D_MODEL), fixed),
                  pl.BlockSpec((1, D_MODEL), fixed),
                  pl.BlockSpec((D_MODEL, X_WIDTH), fixed),
                  pl.BlockSpec((1, X_HEAD_DIM), fixed)],
        out_specs=[pl.BlockSpec((tm, D_MODEL), row), pl.BlockSpec((tm, X_WIDTH), row)],
        out_shape=[jax.ShapeDtypeStruct((t_all, D_MODEL), F32),
                   jax.ShapeDtypeStruct((t_all, X_WIDTH), F32)],
        compiler_params=_cparams(1),
        name="mix_out",
    )(ret_n, att_o, x_all, w_out, g2, w_qx, g_xq)


def _xattn_prompt_kernel(q_ref, k_ref, v_ref, o_ref):
    q = q_ref[...]
    k = k_ref[0]
    v = v_ref[0]
    scale = X_HEAD_DIM ** -0.5
    outs = []
    for h in range(X_HEADS):
        sl = slice(h * LANES, (h + 1) * LANES)
        s = _dot_nt(q[:, sl].astype(BF16), k[:, sl].astype(BF16)) * scale
        p = jnp.exp(s - jnp.max(s, axis=-1, keepdims=True))
        o = _dot(p.astype(BF16), v[:, sl].astype(BF16))
        outs.append(o / jnp.sum(p, axis=-1, keepdims=True))
    o_ref[...] = jnp.concatenate(outs, axis=-1)


def _xattn_prompt(qx, mk, mv, n_batch, seq, tq=512):
    t_all = qx.shape[0]
    per = seq // tq
    m = mk.shape[1]
    rows = pl.BlockSpec((tq, X_WIDTH), lambda n, i: (n * per + i, 0))
    mem = pl.BlockSpec((1, m, X_WIDTH), lambda n, i: (n, 0, 0))
    return pl.pallas_call(
        _xattn_prompt_kernel,
        grid=(n_batch, per),
        in_specs=[rows, mem, mem],
        out_specs=rows,
        out_shape=jax.ShapeDtypeStruct((t_all, X_WIDTH), F32),
        input_output_aliases={0: 0},
        compiler_params=_cparams(2),
        name="xattn_prompt",
    )(qx, mk, mv)


def _xattn_sample_kernel(q_ref, k_ref, v_ref, o_ref, *, group, n_mem):
    i = pl.program_id(0)
    scale = X_HEAD_DIM ** -0.5
    for j in range(group):
        n = i * group + j
        q = q_ref[pl.ds(n, 1), :]
        outs = []
        for h in range(X_HEADS):
            rows = pl.ds(h, n_mem, stride=X_HEADS)
            qh = q[:, h * LANES:(h + 1) * LANES]
            s = jnp.sum(k_ref[j, rows, :] * qh, axis=-1, keepdims=True) * scale
            p = jnp.exp(s - jnp.max(s, axis=0, keepdims=True))
            o = jnp.sum(p * v_ref[j, rows, :], axis=0, keepdims=True)
            outs.append(o / jnp.sum(p, axis=0, keepdims=True))
        o_ref[pl.ds(n, 1), :] = jnp.concatenate(outs, axis=-1)


XATTN_SAMPLE_GROUP = 8


def _xattn_sample(o_all, mk, mv):
    n_s, mh, _ = mk.shape
    t_all = o_all.shape[0]
    last = t_all // n_s - 1
    group = XATTN_SAMPLE_GROUP
    rows = pl.BlockSpec((n_s, X_WIDTH), lambda n: (last, 0))
    mem = pl.BlockSpec((group, mh, X_HEAD_DIM), lambda n: (n, 0, 0))
    return pl.pallas_call(
        functools.partial(_xattn_sample_kernel, group=group, n_mem=mh // X_HEADS),
        grid=(n_s // group,),
        in_specs=[rows, mem, mem],
        out_specs=rows,
        out_shape=jax.ShapeDtypeStruct(o_all.shape, F32),
        input_output_aliases={0: 0},
        compiler_params=_cparams(1),
        name="xattn_sample",
    )(o_all, mk, mv)


ROW_TILE_ROWS = D_MODEL // LANES


def _store_row_tiles(ref, x):
    rows = x.shape[0]
    for c in range(ROW_TILE_ROWS):
        ref[pl.ds(c, rows, stride=ROW_TILE_ROWS), :] = x[:, c * LANES:(c + 1) * LANES]


def _load_row_tiles(ref, rows):
    return jnp.concatenate([ref[pl.ds(c, rows, stride=ROW_TILE_ROWS), :]
                            for c in range(ROW_TILE_ROWS)], axis=1)


def _xout_router_kernel(o_ref, h_ref, wo_ref, g3_ref, wr_ref, br_ref,
                        h2_ref, xn_ref, idx_ref, gate_ref):
    h2 = h_ref[...] + _dot(o_ref[...].astype(BF16), wo_ref[...])
    h2_ref[...] = h2
    xn = _rms(h2, g3_ref[...])
    _store_row_tiles(xn_ref, xn)
    logits = lax.dot_general(wr_ref[...], xn, (((1,), (1,)), ((), ())),
                             precision=lax.Precision.HIGHEST,
                             preferred_element_type=F32) + br_ref[...]
    eid = lax.broadcasted_iota(I32, logits.shape, 0)
    work = logits
    vals, idxs = [], []
    for _ in range(TOP_K):
        mx = jnp.max(work, axis=0, keepdims=True)
        ix = jnp.min(jnp.where(work == mx, eid, N_EXPERTS), axis=0, keepdims=True)
        vals.append(mx)
        idxs.append(ix)
        work = jnp.where(eid == ix, -jnp.inf, work)
    ex = [jnp.exp(v - vals[0]) for v in vals]
    tot = ex[0] + ex[1] + ex[2] + ex[3]
    idx_ref[...] = jnp.concatenate(idxs, axis=0)
    gate_ref[...] = jnp.concatenate([e / tot for e in ex], axis=0)


def _xout_router(o_all, h_all, w_ox, g3, w_rt, b_r):
    t_all = h_all.shape[0]
    tm = TOKEN_TILE
    row = lambda i: (i, 0)
    fixed = lambda i: (0, 0)
    colblk = lambda i: (0, i)
    return pl.pallas_call(
        _xout_router_kernel,
        grid=(t_all // tm,),
        in_specs=[pl.BlockSpec((tm, X_WIDTH), row),
                  pl.BlockSpec((tm, D_MODEL), row),
                  pl.BlockSpec((X_WIDTH, D_MODEL), fixed),
                  pl.BlockSpec((1, D_MODEL), fixed),
                  pl.BlockSpec((N_EXPERTS, D_MODEL), fixed),
                  pl.BlockSpec((N_EXPERTS, 1), fixed)],
        out_specs=[pl.BlockSpec((tm, D_MODEL), row),
                   pl.BlockSpec((tm * ROW_TILE_ROWS, LANES), row),
                   pl.BlockSpec((TOP_K, tm), colblk), pl.BlockSpec((TOP_K, tm), colblk)],
        out_shape=[jax.ShapeDtypeStruct((t_all, D_MODEL), F32),
                   jax.ShapeDtypeStruct((t_all * ROW_TILE_ROWS, LANES), F32),
                   jax.ShapeDtypeStruct((TOP_K, t_all), I32),
                   jax.ShapeDtypeStruct((TOP_K, t_all), F32)],
        compiler_params=_cparams(1),
        name="xout_router",
    )(o_all, h_all, w_ox, g3, w_rt, b_r)


def _route_kernel(idx_ref, dest_ref, be_ref, nu_ref, *, t_all, n_blocks_pad):
    bm = MOE_BLOCK_ROWS
    nt = t_all // LANES
    e_col = lax.broadcasted_iota(I32, (N_EXPERTS, 1), 0)
    hp = lax.Precision.HIGHEST

    def multi_hot(j):
        blk = idx_ref[:, pl.ds(pl.multiple_of(j * LANES, LANES), LANES)]
        mh = jnp.zeros((N_EXPERTS, LANES), F32)
        for k in range(TOP_K):
            mh = mh + (e_col == blk[k:k + 1, :]).astype(F32)
        return blk, mh

    def count_body(j, c):
        _, mh = multi_hot(j)
        return c + jnp.sum(mh, axis=1, keepdims=True)

    counts = lax.fori_loop(0, nt, count_body, jnp.zeros((N_EXPERTS, 1), F32))
    padded = jnp.ceil(counts * (1.0 / bm)) * bm
    tri = (lax.broadcasted_iota(I32, (N_EXPERTS, N_EXPERTS), 1)
           <= lax.broadcasted_iota(I32, (N_EXPERTS, N_EXPERTS), 0)).astype(F32)
    pad_end = jnp.dot(tri, jnp.broadcast_to(padded, (N_EXPERTS, LANES)), precision=hp,
                      preferred_element_type=F32)
    pad_start = pad_end[:, 0:1] - padded
    upper = (lax.broadcasted_iota(I32, (LANES, LANES), 0)
             < lax.broadcasted_iota(I32, (LANES, LANES), 1)).astype(BF16)

    def dest_body(j, carry):
        blk, mh = multi_hot(j)
        rank = carry + _dot(mh.astype(BF16), upper)
        base = pad_start + rank
        for k in range(TOP_K):
            d = jnp.sum(jnp.where(e_col == blk[k:k + 1, :], base, 0.0), axis=0, keepdims=True)
            dest_ref[pl.ds(k, 1), pl.ds(pl.multiple_of(j * LANES, LANES), LANES)] = d.astype(I32)
        return carry + jnp.sum(mh, axis=1, keepdims=True)

    lax.fori_loop(0, nt, dest_body, jnp.zeros((N_EXPERTS, 1), F32))

    b_row = lax.broadcasted_iota(I32, (1, n_blocks_pad), 1).astype(F32) * bm
    be = jnp.sum((pad_end[:, 0:1] <= b_row).astype(F32), axis=0, keepdims=True)
    be_ref[...] = jnp.minimum(be, N_EXPERTS - 1.0).astype(I32)
    nu_ref[...] = (pad_end[N_EXPERTS - 1:N_EXPERTS, :] * (1.0 / bm)).astype(I32)


def _route(idx_t, n_blocks_pad):
    t_all = idx_t.shape[1]
    return pl.pallas_call(
        functools.partial(_route_kernel, t_all=t_all, n_blocks_pad=n_blocks_pad),
        out_shape=[jax.ShapeDtypeStruct((TOP_K, t_all), I32),
                   jax.ShapeDtypeStruct((1, n_blocks_pad), I32),
                   jax.ShapeDtypeStruct((1, LANES), I32)],
        compiler_params=pltpu.CompilerParams(vmem_limit_bytes=VMEM_LIMIT),
        name="moe_route",
    )(idx_t)


def _dispatch_kernel(dest_sm, x_ref, xs_in, xs_ref, sem, *, t_all):
    del xs_in
    i = pl.program_id(0)
    tm = x_ref.shape[0]

    def start_body(j, c):
        for k in range(TOP_K):
            d = dest_sm[k * t_all + i * tm + j]
            pltpu.make_async_copy(x_ref.at[j], xs_ref.at[d], sem).start(priority=k % 2)
        return c

    lax.fori_loop(0, tm, start_body, 0, unroll=ROW_DMA_UNROLL)
    for k in range(TOP_K):
        pltpu.make_async_copy(x_ref, xs_ref.at[pl.ds(0, tm)], sem).wait()


ROW_DMA_UNROLL = 4


def _dispatch(dest_flat, xn_all, xs_init):
    t_all = xn_all.shape[0]
    tm = ROW_TILE
    return pl.pallas_call(
        functools.partial(_dispatch_kernel, t_all=t_all),
        grid_spec=pltpu.PrefetchScalarGridSpec(
            num_scalar_prefetch=1,
            grid=(t_all // tm,),
            in_specs=[pl.BlockSpec((tm, ROW_TILE_ROWS, LANES), lambda i, d: (i, 0, 0)),
                      pl.BlockSpec(memory_space=pl.ANY)],
            out_specs=pl.BlockSpec(memory_space=pl.ANY),
            scratch_shapes=[pltpu.SemaphoreType.DMA]),
        out_shape=jax.ShapeDtypeStruct(xs_init.shape, F32),
        input_output_aliases={2: 0},
        compiler_params=_cparams(1),
        name="moe_dispatch",
    )(dest_flat, xn_all, xs_init)


def _expert_kernel(be_sm, nu_sm, x_ref, wgu_ref, bgu_ref, wd_ref, bd_ref, y_ref, wgu_bf, wd_bf):
    b = pl.program_id(0)
    changed = jnp.logical_or(b == 0, be_sm[b] != be_sm[jnp.maximum(b - 1, 0)])

    @pl.when(changed)
    def _():
        wgu_bf[...] = wgu_ref[0].astype(BF16)
        wd_bf[...] = wd_ref[0].astype(BF16)

    @pl.when(b < nu_sm[0])
    def _():
        x = _load_row_tiles(x_ref, MOE_BLOCK_ROWS).astype(BF16)
        h = _dot(x, wgu_bf[...]) + bgu_ref[0]
        glu = jnp.minimum(h[:, :D_FF], SWIGLU_LIMIT)
        lin = jnp.clip(h[:, D_FF:], -SWIGLU_LIMIT, SWIGLU_LIMIT)
        act = glu * _sigmoid(SWIGLU_ALPHA * glu) * (lin + 1.0)
        _store_row_tiles(y_ref, _dot(act.astype(BF16), wd_bf[...]) + bd_ref[0])

    @pl.when(b >= nu_sm[0])
    def _():
        y_ref[...] = jnp.zeros_like(y_ref)


def _experts(block_e, n_used, xs, w_gu, b_gu, w_down, b_down):
    cap = xs.shape[0] // ROW_TILE_ROWS
    bm = MOE_BLOCK_ROWS
    return pl.pallas_call(
        _expert_kernel,
        grid_spec=pltpu.PrefetchScalarGridSpec(
            num_scalar_prefetch=2,
            grid=(cap // bm,),
            in_specs=[pl.BlockSpec((bm * ROW_TILE_ROWS, LANES), lambda b, be, nu: (b, 0)),
                      pl.BlockSpec((1, D_MODEL, 2 * D_FF), lambda b, be, nu: (be[b], 0, 0)),
                      pl.BlockSpec((1, 1, 2 * D_FF), lambda b, be, nu: (be[b], 0, 0)),
                      pl.BlockSpec((1, D_FF, D_MODEL), lambda b, be, nu: (be[b], 0, 0)),
                      pl.BlockSpec((1, 1, D_MODEL), lambda b, be, nu: (be[b], 0, 0))],
            out_specs=pl.BlockSpec((bm * ROW_TILE_ROWS, LANES), lambda b, be, nu: (b, 0)),
            scratch_shapes=[pltpu.VMEM((D_MODEL, 2 * D_FF), BF16),
                            pltpu.VMEM((D_FF, D_MODEL), BF16)]),
        out_shape=jax.ShapeDtypeStruct((cap * ROW_TILE_ROWS, LANES), F32),
        compiler_params=_cparams(1, VMEM_LIMIT_EXPERTS),
        name="moe_experts",
    )(block_e, n_used, xs, w_gu, b_gu, w_down, b_down)


def _combine_kernel(dest_sm, yb_ref, h_ref, g_ref, op_ref, os_ref, buf, sems, *, t_all):
    i = pl.program_id(0)
    last = pl.num_programs(0) - 1
    tm = h_ref.shape[0]

    def gather(step, slot):
        def start_body(j, c):
            for k in range(TOP_K):
                d = dest_sm[k * t_all + step * tm + j]
                tile = pl.ds(pl.multiple_of(j * ROW_TILE_ROWS, ROW_TILE_ROWS), ROW_TILE_ROWS)
                pltpu.make_async_copy(yb_ref.at[d], buf.at[slot, k, tile],
                                      sems.at[slot]).start(priority=k % 2)
            return c

        lax.fori_loop(0, tm, start_body, 0, unroll=ROW_DMA_UNROLL)

    @pl.when(i == 0)
    def _():
        gather(0, 0)

    @pl.when(i < last)
    def _():
        gather(i + 1, (i + 1) % 2)

    slot = i % 2
    for k in range(TOP_K):
        pltpu.make_async_copy(buf.at[slot, k], buf.at[slot, k], sems.at[slot]).wait()

    g = g_ref[...]
    y = h_ref[...]
    for k in range(TOP_K):
        y = y + g[:, k:k + 1] * _load_row_tiles(buf.at[slot, k], tm)

    @pl.when(i < last)
    def _():
        op_ref[...] = y

    @pl.when(i == last)
    def _():
        os_ref[...] = y


def _combine(dest_flat, yb, h2_all, gates, n_prompt_rows):
    t_all = h2_all.shape[0]
    tm = ROW_TILE
    n_p_tiles = n_prompt_rows // tm
    assert t_all == n_prompt_rows + tm
    return pl.pallas_call(
        functools.partial(_combine_kernel, t_all=t_all),
        grid_spec=pltpu.PrefetchScalarGridSpec(
            num_scalar_prefetch=1,
            grid=(t_all // tm,),
            in_specs=[pl.BlockSpec(memory_space=pl.ANY),
                      pl.BlockSpec((tm, D_MODEL), lambda i, d: (i, 0)),
                      pl.BlockSpec((tm, TOP_K), lambda i, d: (i, 0))],
            out_specs=[pl.BlockSpec((tm, D_MODEL), lambda i, d: (jnp.minimum(i, n_p_tiles - 1), 0)),
                       pl.BlockSpec((tm, D_MODEL), lambda i, d: (0, 0))],
            scratch_shapes=[pltpu.VMEM((2, TOP_K, tm * ROW_TILE_ROWS, LANES), F32),
                            pltpu.SemaphoreType.DMA((2,))]),
        out_shape=[jax.ShapeDtypeStruct((n_prompt_rows, D_MODEL), F32),
                   jax.ShapeDtypeStruct((tm, D_MODEL), F32)],
        compiler_params=_cparams(1),
        name="moe_combine",
    )(dest_flat, yb, h2_all, gates)


def _rope_tables(pos):
    half = HEAD_DIM // 2
    inv = jnp.exp(-math.log(ROPE_THETA) * jnp.arange(half, dtype=F32) / half)
    ang = pos.astype(F32)[:, None] * inv[None, :]
    cos, sin = jnp.cos(ang), jnp.sin(ang)
    cos_t = jnp.concatenate([cos, cos] * HEADS_PER_TILE, axis=-1)
    sin_t = jnp.concatenate([-sin, sin] * HEADS_PER_TILE, axis=-1)
    return cos_t, sin_t


def _block_diag_ones(n, blk):
    r = jnp.arange(n) // blk
    return (r[:, None] == r[None, :]).astype(BF16)


def _layer(x_prompt, x_sample, state_ret, cache_win_k, cache_win_v, cache_mem_k, cache_mem_v,
           mem_prompt, g_norm1, w_in, g_att_q, g_att_k, g_ret_gn, w_out, g_norm2, g_mem,
           w_q_x, w_kv_x, g_x_q, g_x_k, w_o_x, g_norm3, w_router, b_router, w_gu, b_gu,
           w_down, b_down):
    n_b, seq, _ = x_prompt.shape
    n_s = x_sample.shape[0]
    assert x_sample.shape[1] == 1 and n_s == ROW_TILE
    t_p = n_b * seq
    t_all = t_p + n_s
    assert t_all % TOKEN_TILE == 0 and t_p % ROW_TILE == 0

    x_all = jnp.concatenate([x_prompt.reshape(t_p, D_MODEL), x_sample.reshape(n_s, D_MODEL)], axis=0)
    pos = jnp.concatenate([jnp.tile(jnp.arange(seq, dtype=jnp.int32), n_b),
                           jnp.full((n_s,), PAST_LEN, jnp.int32)])
    cos_t, sin_t = _rope_tables(pos)
    log_g = jnp.log1p(-jnp.exp2(-5.0 - jnp.arange(RET_HEADS, dtype=F32)))
    lg_row = jnp.repeat(log_g, HEAD_DIM)[None, :]
    gn_row = g_ret_gn.reshape(1, RET_WIDTH)
    gn_col = g_ret_gn.reshape(RET_WIDTH, 1)
    gq = jnp.tile(g_att_q.reshape(1, HEAD_DIM), (1, ATT_HEADS))
    gk = jnp.tile(g_att_k.reshape(1, HEAD_DIM), (1, ATT_HEADS))
    seg = _block_diag_ones(ATT_WIDTH, HEAD_DIM)

    rq, rk, rv, rg, aq, ak, av = _in_proj(
        x_all, g_norm1.reshape(1, D_MODEL), w_in.astype(BF16), cos_t, sin_t, gq, gk, seg)

    ret_n, st_p = _ret_prompt(rq, rk, rv, rg, gn_row, lg_row, n_b, seq)
    att_o, wk_p, wv_p = _dil_prompt(aq, ak, av, n_b, seq)
    state_t = jnp.transpose(state_ret, (1, 2, 3, 0))
    ret_n, st_s = _ret_sample(rq, rk, rv, ret_n, gn_col, lg_row, state_t)
    ck = jnp.transpose(cache_win_k, (0, 2, 3, 1))
    cv = jnp.transpose(cache_win_v, (0, 2, 3, 1))
    wk_s, wv_s, att_o = _win_sample(att_o, ak, av, ck, cv)

    h_all, qx = _mix_out(ret_n, att_o, x_all, w_out.astype(BF16), g_norm2.reshape(1, D_MODEL),
                         w_q_x.astype(BF16), g_x_q.reshape(1, X_HEAD_DIM))

    mk_p, mv_p = _mem_kv(mem_prompt, g_mem.reshape(1, D_MODEL), w_kv_x.astype(BF16),
                         g_x_k.reshape(1, X_HEAD_DIM))
    o_all = _xattn_prompt(qx, mk_p, mv_p, n_b, seq)
    n_mem = cache_mem_k.shape[1]
    o_all = _xattn_sample(o_all, cache_mem_k.reshape(n_s, n_mem * X_HEADS, X_HEAD_DIM),
                          cache_mem_v.reshape(n_s, n_mem * X_HEADS, X_HEAD_DIM))

    h2_all, xn_all, idx_t, gate_t = _xout_router(
        o_all, h_all, w_o_x.astype(BF16), g_norm3.reshape(1, D_MODEL),
        jnp.transpose(w_router), b_router.reshape(N_EXPERTS, 1))

    bm = MOE_BLOCK_ROWS
    n_blocks = -(-(t_all * TOP_K) // bm) + N_EXPERTS
    n_blocks_pad = -(-n_blocks // LANES) * LANES
    dest_t, be, nu = _route(idx_t, n_blocks_pad)
    dest_flat = dest_t.reshape(TOP_K * t_all)
    cap = n_blocks * bm
    xs = _dispatch(dest_flat, xn_all.reshape(t_all, ROW_TILE_ROWS, LANES),
                   jnp.zeros((cap, ROW_TILE_ROWS, LANES), F32))
    yb = _experts(be.reshape(n_blocks_pad), nu.reshape(LANES), xs.reshape(cap * ROW_TILE_ROWS, LANES),
                  w_gu, b_gu.reshape(N_EXPERTS, 1, 2 * D_FF), w_down,
                  b_down.reshape(N_EXPERTS, 1, D_MODEL))
    y_p, y_s = _combine(dest_flat, yb.reshape(cap, ROW_TILE_ROWS, LANES), h2_all,
                        jnp.transpose(gate_t), t_p)

    st_p = jnp.stack([st_p[:, :, :HEAD_DIM, :HEAD_DIM], st_p[:, :, HEAD_DIM:, HEAD_DIM:]], axis=2)
    st_p = st_p.reshape(n_b, RET_HEADS, HEAD_DIM, HEAD_DIM)
    return (y_p.reshape(n_b, seq, D_MODEL),
            y_s.reshape(n_s, 1, D_MODEL),
            st_p,
            jnp.transpose(st_s, (3, 0, 1, 2)),
            jnp.transpose(wk_p, (0, 3, 1, 2)),
            jnp.transpose(wv_p, (0, 3, 1, 2)),
            jnp.transpose(wk_s, (0, 3, 1, 2)),
            jnp.transpose(wv_s, (0, 3, 1, 2)),
            mk_p.reshape(n_b, n_mem, X_HEADS, X_HEAD_DIM),
            mv_p.reshape(n_b, n_mem, X_HEADS, X_HEAD_DIM))


def kernel(x_prompt, x_sample, state_ret, cache_win_k, cache_win_v, cache_mem_k, cache_mem_v,
           mem_prompt, g_norm1, w_in, g_att_q, g_att_k, g_ret_gn, w_out, g_norm2, g_mem,
           w_q_x, w_kv_x, g_x_q, g_x_k, w_o_x, g_norm3, w_router, b_router, w_gu, b_gu,
           w_down, b_down):
    assert state_ret.shape[0] == 1, "single-layer trunk"
    outs = _layer(x_prompt, x_sample, state_ret[0], cache_win_k[0], cache_win_v[0],
                  cache_mem_k[0], cache_mem_v[0], mem_prompt, g_norm1[0], w_in[0], g_att_q[0],
                  g_att_k[0], g_ret_gn[0], w_out[0], g_norm2[0], g_mem[0], w_q_x[0], w_kv_x[0],
                  g_x_q[0], g_x_k[0], w_o_x[0], g_norm3[0], w_router[0], b_router[0], w_gu[0],
                  b_gu[0], w_down[0], b_down[0])
    y_p, y_s = outs[0], outs[1]
    return (y_p, y_s) + tuple(o[None] for o in outs[2:])
```

```python
import functools
import math

import jax
import jax.numpy as jnp
from jax import lax
from jax.experimental import pallas as pl
from jax.experimental.pallas import tpu as pltpu

F32 = jnp.float32
BF16 = jnp.bfloat16
I32 = jnp.int32

D_MODEL = 1024
HEAD_DIM = 64
RET_HEADS = 8
ATT_HEADS = 8
RET_WIDTH = RET_HEADS * HEAD_DIM
ATT_WIDTH = ATT_HEADS * HEAD_DIM
IN_COLS = 4 * RET_WIDTH + 3 * ATT_WIDTH
RET_CHUNK = 128
DIL_PATTERNS = ((128, 1), (512, 4), (2048, 16))
DIL_BLOCK = 128
MAX_WINDOW = 2048
X_HEADS = 4
X_HEAD_DIM = 128
X_WIDTH = X_HEADS * X_HEAD_DIM
N_EXPERTS = 32
TOP_K = 4
D_FF = D_MODEL
SWIGLU_LIMIT = 7.0
SWIGLU_ALPHA = 1.702
ROPE_THETA = 10000.0
EPS = 1e-6
PAST_LEN = 8192

LANES = 128
SUBLANES = 8
HEADS_PER_TILE = LANES // HEAD_DIM

TOKEN_TILE = 384
PROMPT_TILE = 512
ROW_TILE = 128
MOE_BLOCK_ROWS = 512
NEG_BIG = -1e30
VMEM_LIMIT = 48 * 1024 * 1024
VMEM_LIMIT_EXPERTS = 58 * 1024 * 1024


def _cparams(n_axes, vmem_limit=VMEM_LIMIT):
    return pltpu.CompilerParams(
        dimension_semantics=("arbitrary",) * n_axes, vmem_limit_bytes=vmem_limit)


def _rms(x, g):
    return x * lax.rsqrt(jnp.mean(x * x, axis=-1, keepdims=True) + EPS) * g


def _dot(a, b):
    return jnp.dot(a, b, preferred_element_type=F32)


def _dot_nt(a, b):
    return lax.dot_general(a, b, (((1,), (1,)), ((), ())), preferred_element_type=F32)


def _dot_tn(a, b):
    return lax.dot_general(a, b, (((0,), (0,)), ((), ())), preferred_element_type=F32)


def _sigmoid(x):
    return 1.0 / (1.0 + jnp.exp(-x))


N_PROJ_OUT = 7


def _in_proj_kernel(x_ref, g1_ref, w_ref, gq_ref, gk_ref, seg_ref, cos_ref, sin_ref, *rest):
    rq_ref, rk_ref, rv_ref, rg_ref, aq_ref, ak_ref, av_ref = rest[-N_PROJ_OUT:]
    xn = _rms(x_ref[...], g1_ref[...]).astype(BF16)
    proj = _dot(xn, w_ref[...])
    reps = RET_WIDTH // LANES
    cos = jnp.concatenate([cos_ref[...]] * reps, axis=-1)
    sin = jnp.concatenate([sin_ref[...]] * reps, axis=-1)
    lane = lax.broadcasted_iota(I32, (1, RET_WIDTH), 1)
    first_half = (lane % HEAD_DIM) < (HEAD_DIM // 2)
    seg = seg_ref[...]

    def rope(t):
        partner = jnp.where(first_half,
                            pltpu.roll(t, RET_WIDTH - HEAD_DIM // 2, 1),
                            pltpu.roll(t, HEAD_DIM // 2, 1))
        return t * cos + partner * sin

    def head_norm(t, g):
        ssum = _dot((t * t).astype(BF16), seg)
        return t * lax.rsqrt(ssum * (1.0 / HEAD_DIM) + EPS) * g

    w = RET_WIDTH
    rq_ref[...] = rope(proj[:, 0:w])
    rk_ref[...] = rope(proj[:, w:2 * w]) * (HEAD_DIM ** -0.5)
    rv_ref[...] = proj[:, 2 * w:3 * w]
    rg_ref[...] = proj[:, 3 * w:4 * w]
    aq_ref[...] = rope(head_norm(proj[:, 4 * w:5 * w], gq_ref[...]))
    ak_ref[...] = rope(head_norm(proj[:, 5 * w:6 * w], gk_ref[...]))
    av_ref[...] = proj[:, 6 * w:7 * w]


def _group_rows_call(kernel, name, shared_ins, row_ins, fixed_ins, tables, out_widths, t_all, tm,
                     row_start, prev):
    rows = row_ins[0].shape[0]
    off = row_start // tm
    assert rows % tm == 0 and row_start % tm == 0
    n_steps = rows // tm
    extra = 1 if (prev is None and row_start + rows < t_all) else 0
    src = lambda i: jnp.minimum(i, n_steps - 1)
    fixed = lambda i: (0, 0)
    in_specs = [pl.BlockSpec((tm, a.shape[1]), lambda i: (src(i) + off, 0)) for a in shared_ins]
    in_specs += [pl.BlockSpec((tm, a.shape[1]), lambda i: (src(i), 0)) for a in row_ins]
    in_specs += [pl.BlockSpec(a.shape, fixed) for a in fixed_ins]
    for t in tables:
        period = t.shape[0] // tm
        in_specs.append(pl.BlockSpec((tm, t.shape[1]),
                                     lambda i, period=period: (src(i) % period, 0)))
    n_in = len(in_specs)
    aliases = {}
    if prev is not None:
        in_specs += [pl.BlockSpec(memory_space=pl.ANY)] * len(prev)
        aliases = {n_in + j: j for j in range(len(prev))}
    return pl.pallas_call(
        kernel,
        grid=(n_steps + extra,),
        in_specs=in_specs,
        out_specs=[pl.BlockSpec((tm, w), lambda i: (i + off, 0)) for w in out_widths],
        out_shape=[jax.ShapeDtypeStruct((t_all, w), F32) for w in out_widths],
        input_output_aliases=aliases,
        compiler_params=_cparams(1),
        name=name,
    )(*shared_ins, *row_ins, *fixed_ins, *tables, *(prev or ()))


def _in_proj(x, g1, w_in, cos_t, sin_t, gq, gk, seg, t_all, tm, row_start, prev=None):
    return _group_rows_call(_in_proj_kernel, "in_proj", [], [x], [g1, w_in, gq, gk, seg],
                            [cos_t, sin_t], [RET_WIDTH] * N_PROJ_OUT, t_all, tm, row_start, prev)


def _group_norm_gate(o, mask_a, gn, gate):
    inv = 1.0 / HEAD_DIM
    sa = jnp.sum(jnp.where(mask_a, o, 0.0), axis=-1, keepdims=True)
    sb = jnp.sum(jnp.where(mask_a, 0.0, o), axis=-1, keepdims=True)
    cen = o - jnp.where(mask_a, sa, sb) * inv
    c2 = cen * cen
    va = jnp.sum(jnp.where(mask_a, c2, 0.0), axis=-1, keepdims=True)
    vb = jnp.sum(jnp.where(mask_a, 0.0, c2), axis=-1, keepdims=True)
    var = jnp.where(mask_a, va, vb) * inv
    return cen * lax.rsqrt(var + EPS) * gn * (gate * _sigmoid(gate))


def _ret_prompt_kernel(q_ref, k_ref, v_ref, g_ref, gn_ref, lg_ref, o_ref, st_ref,
                       state, dec, qdec, kdec):
    n = pl.program_id(0)
    c = pl.program_id(1)
    ch = RET_CHUNK
    n_pairs = RET_WIDTH // LANES
    lane = lax.broadcasted_iota(I32, (1, LANES), 1)
    mask_a = lane < HEAD_DIM
    row_i = lax.broadcasted_iota(I32, (ch, 1), 0)

    @pl.when(jnp.logical_and(n == 0, c == 0))
    def _():
        row = row_i.astype(F32)
        col = lax.broadcasted_iota(I32, (1, ch), 1).astype(F32)
        diff = row - col
        causal = diff >= 0.0
        dpos = jnp.maximum(diff, 0.0)
        lg = lg_ref[...]
        for h in range(RET_HEADS):
            dec[h] = jnp.where(causal, jnp.exp(dpos * lg[:, h * HEAD_DIM:h * HEAD_DIM + 1]), 0.0)
        qdec[...] = jnp.exp((row + 1.0) * lg)
        kdec[...] = jnp.exp((ch - 1.0 - row) * lg)

    @pl.when(c == 0)
    def _():
        state[...] = jnp.zeros_like(state)

    same_head = (row_i // HEAD_DIM) == (lane // HEAD_DIM)
    cdec = jnp.exp(ch * lg_ref[...])
    for p in range(n_pairs):
        sl = slice(p * LANES, (p + 1) * LANES)
        q = q_ref[:, sl]
        k = k_ref[:, sl]
        kb = k.astype(BF16)
        vb = v_ref[:, sl].astype(BF16)
        lhs = jnp.concatenate([jnp.where(mask_a, q, 0.0), jnp.where(mask_a, 0.0, q)],
                              axis=0).astype(BF16)
        s = _dot_nt(lhs, kb)
        pr = (s * jnp.concatenate([dec[2 * p], dec[2 * p + 1]], axis=0)).astype(BF16)
        o2 = _dot(pr, vb)
        o_intra = jnp.where(mask_a, o2[:ch], o2[ch:])
        st = state[p]
        o = o_intra + _dot(q.astype(BF16), st.astype(BF16)) * qdec[:, sl]
        upd = _dot_tn((k * kdec[:, sl]).astype(BF16), vb)
        state[p] = cdec[:, sl] * st + jnp.where(same_head, upd, 0.0)
        o_ref[:, sl] = _group_norm_gate(o, mask_a, gn_ref[:, sl], g_ref[:, sl])

    @pl.when(c == pl.num_programs(1) - 1)
    def _():
        st_ref[0] = state[...]


def _ret_prompt(rq, rk, rv, rg, gn_row, lg_row, n_batch, seq):
    t_all = rq.shape[0]
    n_pairs = RET_WIDTH // LANES
    n_chunks = seq // RET_CHUNK
    blk = pl.BlockSpec((RET_CHUNK, RET_WIDTH), lambda n, c: (n * n_chunks + c, 0))
    lane_row = pl.BlockSpec((1, RET_WIDTH), lambda n, c: (0, 0))
    return pl.pallas_call(
        _ret_prompt_kernel,
        grid=(n_batch, n_chunks),
        in_specs=[blk, blk, blk, blk, lane_row, lane_row],
        out_specs=[blk, pl.BlockSpec((1, n_pairs, LANES, LANES), lambda n, c: (n, 0, 0, 0))],
        out_shape=[jax.ShapeDtypeStruct((t_all, RET_WIDTH), F32),
                   jax.ShapeDtypeStruct((n_batch, n_pairs, LANES, LANES), F32)],
        scratch_shapes=[pltpu.VMEM((n_pairs, LANES, LANES), F32),
                        pltpu.VMEM((RET_HEADS, RET_CHUNK, RET_CHUNK), F32),
                        pltpu.VMEM((RET_CHUNK, RET_WIDTH), F32),
                        pltpu.VMEM((RET_CHUNK, RET_WIDTH), F32)],
        input_output_aliases={3: 0},
        compiler_params=_cparams(2),
        name="ret_prompt",
    )(rq, rk, rv, rg, gn_row, lg_row)


DIL_UNROLL = 4
DIL_MERGE_ROWS = 256


def _dil_prompt_kernel(q_ref, k_ref, v_ref, o_ref, wk_ref, wv_ref, acc, m_s, l_s, *, seq, wp):
    b = DIL_BLOCK
    lane = lax.broadcasted_iota(I32, (1, LANES), 1)
    mask_a = lane < HEAD_DIM
    qi = lax.broadcasted_iota(I32, (2 * b, 1), 0) % b
    kk = lax.broadcasted_iota(I32, (1, 2 * b), 1)
    dist = qi + b - kk
    scale = HEAD_DIM ** -0.5

    for pat, (window, dil) in enumerate(DIL_PATTERNS):
        steps = window // dil
        band = (dist >= 0) & (dist <= steps)
        nb = seq // (b * dil)

        def block(idx, dil=dil, nb=nb, band=band):
            r = idx // nb
            i = idx % nb
            rows_q = pl.ds(r + i * (b * dil), b, stride=dil)
            rows_p = pl.ds(r + jnp.maximum(i - 1, 0) * (b * dil), b, stride=dil)
            q = q_ref[rows_q, :]
            kc = jnp.concatenate([k_ref[rows_p, :], k_ref[rows_q, :]], axis=0).astype(BF16)
            vc = jnp.concatenate([v_ref[rows_p, :], v_ref[rows_q, :]], axis=0).astype(BF16)
            lhs = jnp.concatenate([jnp.where(mask_a, q, 0.0), jnp.where(mask_a, 0.0, q)],
                                  axis=0).astype(BF16)
            s = _dot_nt(lhs, kc) * scale
            valid = band & ((kk >= b) | (jnp.broadcast_to(i, kk.shape) > 0))
            s = jnp.where(valid, s, NEG_BIG)
            m_blk = jnp.max(s, axis=-1, keepdims=True)
            p = jnp.exp(s - m_blk)
            l_blk = jnp.sum(p, axis=-1, keepdims=True)
            pv = _dot(p.astype(BF16), vc)
            return rows_q, m_blk, l_blk, pv

        def emit(rows_q, m_blk, l_blk, pv, pat=pat):
            acc[pat, rows_q, :] = jnp.where(mask_a, pv[:b], pv[b:])
            m_s[pat, rows_q, :] = jnp.where(mask_a, m_blk[:b], m_blk[b:])
            l_s[pat, rows_q, :] = jnp.where(mask_a, l_blk[:b], l_blk[b:])

        def body(it, carry, block=block, emit=emit):
            parts = [block(it * DIL_UNROLL + u) for u in range(DIL_UNROLL)]
            for part in parts:
                emit(*part)
            return carry

        lax.fori_loop(0, (dil * nb) // DIL_UNROLL, body, 0)

    n_pat = len(DIL_PATTERNS)

    def merge_body(c, carry):
        rows = pl.ds(pl.multiple_of(c * DIL_MERGE_ROWS, DIL_MERGE_ROWS), DIL_MERGE_ROWS)
        ms = [m_s[pat, rows, :] for pat in range(n_pat)]
        m = functools.reduce(jnp.maximum, ms)
        num = jnp.zeros((DIL_MERGE_ROWS, LANES), F32)
        den = jnp.zeros((DIL_MERGE_ROWS, LANES), F32)
        for pat in range(n_pat):
            w = jnp.exp(ms[pat] - m)
            num = num + w * acc[pat, rows, :]
            den = den + w * l_s[pat, rows, :]
        o_ref[rows, :] = num / den
        return carry

    lax.fori_loop(0, seq // DIL_MERGE_ROWS, merge_body, 0)

    tchunk = 4 * LANES
    for j in range(wp // tchunk):
        rows = pl.ds(seq - wp + j * tchunk, tchunk)
        cols = slice(j * tchunk, (j + 1) * tchunk)
        kt = k_ref[rows, :].T
        vt = v_ref[rows, :].T
        for hh in range(HEADS_PER_TILE):
            wk_ref[0, hh, :, cols] = kt[hh * HEAD_DIM:(hh + 1) * HEAD_DIM]
            wv_ref[0, hh, :, cols] = vt[hh * HEAD_DIM:(hh + 1) * HEAD_DIM]


def _dil_prompt(aq, ak, av, n_batch, seq):
    t_all = aq.shape[0]
    n_pairs = ATT_WIDTH // LANES
    wp = min(MAX_WINDOW, seq)
    blk = pl.BlockSpec((seq, LANES), lambda n, p: (n, p))
    wblk = pl.BlockSpec((1, HEADS_PER_TILE, HEAD_DIM, wp), lambda n, p: (n, p, 0, 0))
    wshape = jax.ShapeDtypeStruct((n_batch, ATT_HEADS, HEAD_DIM, wp), F32)
    return pl.pallas_call(
        functools.partial(_dil_prompt_kernel, seq=seq, wp=wp),
        grid=(n_batch, n_pairs),
        in_specs=[blk, blk, blk],
        out_specs=[blk, wblk, wblk],
        out_shape=[jax.ShapeDtypeStruct((t_all, ATT_WIDTH), F32), wshape, wshape],
        scratch_shapes=[pltpu.VMEM((len(DIL_PATTERNS), seq, LANES), F32)] * 3,
        input_output_aliases={0: 0},
        compiler_params=_cparams(2),
        name="dil_prompt",
    )(aq, ak, av)


def _ret_sample_kernel(q_ref, k_ref, v_ref, g_ref, gn_ref, lg_ref, st_ref,
                       o_ref, nst_ref, qt, kt):
    qt[...] = q_ref[...].T
    kt[...] = k_ref[...].T
    vt = v_ref[...].T
    gt = g_ref[...].T
    lg = lg_ref[...]
    outs = []
    for hh in range(HEADS_PER_TILE):
        lo = hh * HEAD_DIM
        gdec = jnp.exp(lg[:, lo:lo + 1])
        vth = vt[lo:lo + HEAD_DIM, :]

        def body(d, o, hh=hh, lo=lo, gdec=gdec, vth=vth):
            new = gdec * st_ref[hh, d] + kt[pl.ds(lo + d, 1), :] * vth
            nst_ref[hh, d] = new
            return o + qt[pl.ds(lo + d, 1), :] * new

        o = lax.fori_loop(0, HEAD_DIM, body, jnp.zeros_like(vth))
        mu = jnp.mean(o, axis=0, keepdims=True)
        cen = o - mu
        var = jnp.mean(cen * cen, axis=0, keepdims=True)
        gate = gt[lo:lo + HEAD_DIM, :]
        outs.append(cen * lax.rsqrt(var + EPS) * gn_ref[lo:lo + HEAD_DIM, :] * (gate * _sigmoid(gate)))
    o_ref[...] = jnp.concatenate(outs, axis=0).T


def _ret_sample(rq, rk, rv, ret_n, gn_col, lg_row, state_t):
    t_all = rq.shape[0]
    n_s = state_t.shape[-1]
    n_pairs = RET_WIDTH // LANES
    last = t_all // n_s - 1
    blk = pl.BlockSpec((n_s, LANES), lambda p: (last, p))
    st_blk = pl.BlockSpec((HEADS_PER_TILE, HEAD_DIM, HEAD_DIM, n_s), lambda p: (p, 0, 0, 0))
    return pl.pallas_call(
        _ret_sample_kernel,
        grid=(n_pairs,),
        in_specs=[blk, blk, blk, blk,
                  pl.BlockSpec((LANES, 1), lambda p: (p, 0)),
                  pl.BlockSpec((1, LANES), lambda p: (0, p)),
                  st_blk],
        out_specs=[blk, st_blk],
        out_shape=[jax.ShapeDtypeStruct(ret_n.shape, F32),
                   jax.ShapeDtypeStruct(state_t.shape, F32)],
        scratch_shapes=[pltpu.VMEM((LANES, n_s), F32), pltpu.VMEM((LANES, n_s), F32)],
        input_output_aliases={3: 0},
        compiler_params=_cparams(1),
        name="ret_sample",
    )(rq, rk, rv, ret_n, gn_col, lg_row, state_t)


def _win_sample_kernel(aq_ref, akn_ref, avn_ref, kc_ref, vc_ref,
                       ko_ref, vo_ref, att_ref, qt, kt, vt, acct, *, win):
    n = pl.program_id(0)
    n_s = qt.shape[1]

    @pl.when(n == 0)
    def _():
        qt[...] = aq_ref[...].T
        kt[...] = akn_ref[...].T
        vt[...] = avn_ref[...].T
        acct[...] = jnp.zeros_like(acct)

    onehot = (lax.broadcasted_iota(I32, (n_s, LANES), 0) == n).astype(F32)
    hp = lax.Precision.HIGHEST
    qb = jnp.dot(qt[...], onehot, precision=hp, preferred_element_type=F32)
    kb = jnp.dot(kt[...], onehot, precision=hp, preferred_element_type=F32)
    vb = jnp.dot(vt[...], onehot, precision=hp, preferred_element_type=F32)

    w_pos = lax.broadcasted_iota(I32, (1, win), 1)
    back = win - w_pos
    mult = jnp.zeros((1, win), F32)
    for window, dil in DIL_PATTERNS:
        mult = mult + ((back <= window) & (back % dil == 0)).astype(F32)
    valid = mult > 0.0
    is_last = w_pos == win - 1
    reps = win // LANES
    scale = HEAD_DIM ** -0.5

    s_rows, s0_rows = [], []
    for h in range(ATT_HEADS):
        lo = h * HEAD_DIM
        k_t = kc_ref[0, h]
        qh = qb[lo:lo + HEAD_DIM, :]
        kh = kb[lo:lo + HEAD_DIM, :]
        s_rows.append(jnp.sum(k_t * jnp.concatenate([qh] * reps, axis=1), axis=0, keepdims=True))
        s0_rows.append(jnp.sum(qh * kh, axis=0, keepdims=True)[:, 0:1])
        ko_ref[0, h] = jnp.where(is_last, jnp.concatenate([kh] * reps, axis=1),
                                 pltpu.roll(k_t, win - 1, 1))
    s = jnp.concatenate(s_rows, axis=0) * scale
    s0 = jnp.concatenate(s0_rows, axis=0) * scale
    m = jnp.maximum(jnp.max(jnp.where(valid, s, NEG_BIG), axis=-1, keepdims=True), s0)
    e = jnp.where(valid, jnp.exp(s - m), 0.0) * mult
    e0 = len(DIL_PATTERNS) * jnp.exp(s0 - m)
    denom = jnp.sum(e, axis=-1, keepdims=True) + e0

    cols = []
    for h in range(ATT_HEADS):
        lo = h * HEAD_DIM
        v_t = vc_ref[0, h]
        vh = vb[lo:lo + HEAD_DIM, :]
        num = jnp.sum(v_t * e[h:h + 1, :], axis=1, keepdims=True) + e0[h:h + 1, :] * vh[:, 0:1]
        cols.append(num / denom[h:h + 1, :])
        vo_ref[0, h] = jnp.where(is_last, jnp.concatenate([vh] * reps, axis=1),
                                 pltpu.roll(v_t, win - 1, 1))
    o_col = jnp.concatenate(cols, axis=0)
    lane_n = lax.broadcasted_iota(I32, (1, n_s), 1) == n
    acct[...] = jnp.where(lane_n, o_col, acct[...])

    @pl.when(n == pl.num_programs(0) - 1)
    def _():
        att_ref[...] = acct[...].T


def _win_sample(att_o, ak, av, cache_k, cache_v):
    n_s, n_h, hd, win = cache_k.shape
    assert win >= max(w for w, _ in DIL_PATTERNS)
    t_all = att_o.shape[0]
    last = t_all // n_s - 1
    rows = pl.BlockSpec((n_s, ATT_WIDTH), lambda n: (last, 0))
    cblk = pl.BlockSpec((1, n_h, hd, win), lambda n: (n, 0, 0, 0))
    cshape = jax.ShapeDtypeStruct(cache_k.shape, F32)
    return pl.pallas_call(
        functools.partial(_win_sample_kernel, win=win),
        grid=(n_s,),
        in_specs=[rows, rows, rows, cblk, cblk],
        out_specs=[cblk, cblk, rows],
        out_shape=[cshape, cshape, jax.ShapeDtypeStruct(att_o.shape, F32)],
        scratch_shapes=[pltpu.VMEM((ATT_WIDTH, n_s), F32)] * 4,
        input_output_aliases={0: 2},
        compiler_params=_cparams(1),
        name="win_sample",
    )(att_o, ak, av, cache_k, cache_v)


def _lane_tile_norm(t, g, n_tiles):
    outs = []
    for h in range(n_tiles):
        outs.append(_rms(t[:, h * LANES:(h + 1) * LANES], g))
    return jnp.concatenate(outs, axis=-1)


def _mem_kv_kernel(mem_ref, gm_ref, w_ref, gk_ref, k_ref, v_ref):
    xn = _rms(mem_ref[0], gm_ref[...]).astype(BF16)
    kv = _dot(xn, w_ref[...])
    k_ref[0] = _lane_tile_norm(kv[:, :X_WIDTH], gk_ref[...], X_HEADS)
    v_ref[0] = kv[:, X_WIDTH:]


def _mem_kv(mem, g_mem, w_kv, g_xk):
    n, m, _ = mem.shape
    out = jax.ShapeDtypeStruct((n, m, X_WIDTH), F32)
    return pl.pallas_call(
        _mem_kv_kernel,
        grid=(n,),
        in_specs=[pl.BlockSpec((1, m, D_MODEL), lambda i: (i, 0, 0)),
                  pl.BlockSpec((1, D_MODEL), lambda i: (0, 0)),
                  pl.BlockSpec((D_MODEL, 2 * X_WIDTH), lambda i: (0, 0)),
                  pl.BlockSpec((1, X_HEAD_DIM), lambda i: (0, 0))],
        out_specs=[pl.BlockSpec((1, m, X_WIDTH), lambda i: (i, 0, 0))] * 2,
        out_shape=[out, out],
        compiler_params=_cparams(1),
        name="mem_kv",
    )(mem, g_mem, w_kv, g_xk)


def _mix_out_kernel(ret_ref, att_ref, x_ref, wo_ref, g2_ref, wq_ref, gq_ref, *rest):
    h_ref, q_ref = rest[-2:]
    mixed = jnp.concatenate([ret_ref[...], att_ref[...]], axis=-1).astype(BF16)
    h = x_ref[...] + _dot(mixed, wo_ref[...])
    h_ref[...] = h
    q = _dot(_rms(h, g2_ref[...]).astype(BF16), wq_ref[...])
    q_ref[...] = _lane_tile_norm(q, gq_ref[...], X_HEADS)


def _mix_out(ret_n, att_o, x, w_out, g2, w_qx, g_xq, tm, row_start, prev=None):
    return _group_rows_call(_mix_out_kernel, "mix_out", [ret_n, att_o], [x],
                            [w_out, g2, w_qx, g_xq], [], [D_MODEL, X_WIDTH], ret_n.shape[0], tm,
                            row_start, prev)


def _xattn_prompt_kernel(q_ref, k_ref, v_ref, o_ref):
    q = q_ref[...]
    k = k_ref[0]
    v = v_ref[0]
    scale = X_HEAD_DIM ** -0.5
    outs = []
    for h in range(X_HEADS):
        sl = slice(h * LANES, (h + 1) * LANES)
        s = _dot_nt(q[:, sl].astype(BF16), k[:, sl].astype(BF16)) * scale
        p = jnp.exp(s - jnp.max(s, axis=-1, keepdims=True))
        o = _dot(p.astype(BF16), v[:, sl].astype(BF16))
        outs.append(o / jnp.sum(p, axis=-1, keepdims=True))
    o_ref[...] = jnp.concatenate(outs, axis=-1)


def _xattn_prompt(qx, mk, mv, n_batch, seq, tq=512):
    t_all = qx.shape[0]
    per = seq // tq
    m = mk.shape[1]
    rows = pl.BlockSpec((tq, X_WIDTH), lambda n, i: (n * per + i, 0))
    mem = pl.BlockSpec((1, m, X_WIDTH), lambda n, i: (n, 0, 0))
    return pl.pallas_call(
        _xattn_prompt_kernel,
        grid=(n_batch, per),
        in_specs=[rows, mem, mem],
        out_specs=rows,
        out_shape=jax.ShapeDtypeStruct((t_all, X_WIDTH), F32),
        input_output_aliases={0: 0},
        compiler_params=_cparams(2),
        name="xattn_prompt",
    )(qx, mk, mv)


def _xattn_sample_kernel(q_ref, k_ref, v_ref, o_ref, *, group, n_mem):
    i = pl.program_id(0)
    scale = X_HEAD_DIM ** -0.5
    for j in range(group):
        n = i * group + j
        q = q_ref[pl.ds(n, 1), :]
        outs = []
        for h in range(X_HEADS):
            rows = pl.ds(h, n_mem, stride=X_HEADS)
            qh = q[:, h * LANES:(h + 1) * LANES]
            s = jnp.sum(k_ref[j, rows, :] * qh, axis=-1, keepdims=True) * scale
            p = jnp.exp(s - jnp.max(s, axis=0, keepdims=True))
            o = jnp.sum(p * v_ref[j, rows, :], axis=0, keepdims=True)
            outs.append(o / jnp.sum(p, axis=0, keepdims=True))
        o_ref[pl.ds(n, 1), :] = jnp.concatenate(outs, axis=-1)


XATTN_SAMPLE_GROUP = 8


def _xattn_sample(o_all, mk, mv):
    n_s, mh, _ = mk.shape
    t_all = o_all.shape[0]
    last = t_all // n_s - 1
    group = XATTN_SAMPLE_GROUP
    rows = pl.BlockSpec((n_s, X_WIDTH), lambda n: (last, 0))
    mem = pl.BlockSpec((group, mh, X_HEAD_DIM), lambda n: (n, 0, 0))
    return pl.pallas_call(
        functools.partial(_xattn_sample_kernel, group=group, n_mem=mh // X_HEADS),
        grid=(n_s // group,),
        in_specs=[rows, mem, mem],
        out_specs=rows,
        out_shape=jax.ShapeDtypeStruct(o_all.shape, F32),
        input_output_aliases={0: 0},
        compiler_params=_cparams(1),
        name="xattn_sample",
    )(o_all, mk, mv)


ROW_TILE_ROWS = D_MODEL // LANES


def _store_row_tiles(ref, x):
    rows = x.shape[0]
    for c in range(ROW_TILE_ROWS):
        ref[pl.ds(c, rows, stride=ROW_TILE_ROWS), :] = x[:, c * LANES:(c + 1) * LANES]


def _load_row_tiles(ref, rows):
    return jnp.concatenate([ref[pl.ds(c, rows, stride=ROW_TILE_ROWS), :]
                            for c in range(ROW_TILE_ROWS)], axis=1)


def _xout_router_kernel(o_ref, h_ref, wo_ref, g3_ref, wr_ref, br_ref,
                        h2_ref, xn_ref, idx_ref, gate_ref):
    h2 = h_ref[...] + _dot(o_ref[...].astype(BF16), wo_ref[...])
    h2_ref[...] = h2
    xn = _rms(h2, g3_ref[...])
    _store_row_tiles(xn_ref, xn)
    logits = lax.dot_general(wr_ref[...], xn, (((1,), (1,)), ((), ())),
                             precision=lax.Precision.HIGHEST,
                             preferred_element_type=F32) + br_ref[...]
    eid = lax.broadcasted_iota(I32, logits.shape, 0)
    work = logits
    vals, idxs = [], []
    for _ in range(TOP_K):
        mx = jnp.max(work, axis=0, keepdims=True)
        ix = jnp.min(jnp.where(work == mx, eid, N_EXPERTS), axis=0, keepdims=True)
        vals.append(mx)
        idxs.append(ix)
        work = jnp.where(eid == ix, -jnp.inf, work)
    ex = [jnp.exp(v - vals[0]) for v in vals]
    tot = ex[0] + ex[1] + ex[2] + ex[3]
    idx_ref[...] = jnp.concatenate(idxs, axis=0)
    gate_ref[...] = jnp.concatenate([e / tot for e in ex], axis=0)


def _xout_router(o_all, h_all, w_ox, g3, w_rt, b_r):
    t_all = h_all.shape[0]
    tm = TOKEN_TILE
    row = lambda i: (i, 0)
    fixed = lambda i: (0, 0)
    colblk = lambda i: (0, i)
    return pl.pallas_call(
        _xout_router_kernel,
        grid=(t_all // tm,),
        in_specs=[pl.BlockSpec((tm, X_WIDTH), row),
                  pl.BlockSpec((tm, D_MODEL), row),
                  pl.BlockSpec((X_WIDTH, D_MODEL), fixed),
                  pl.BlockSpec((1, D_MODEL), fixed),
                  pl.BlockSpec((N_EXPERTS, D_MODEL), fixed),
                  pl.BlockSpec((N_EXPERTS, 1), fixed)],
        out_specs=[pl.BlockSpec((tm, D_MODEL), row),
                   pl.BlockSpec((tm * ROW_TILE_ROWS, LANES), row),
                   pl.BlockSpec((TOP_K, tm), colblk), pl.BlockSpec((TOP_K, tm), colblk)],
        out_shape=[jax.ShapeDtypeStruct((t_all, D_MODEL), F32),
                   jax.ShapeDtypeStruct((t_all * ROW_TILE_ROWS, LANES), F32),
                   jax.ShapeDtypeStruct((TOP_K, t_all), I32),
                   jax.ShapeDtypeStruct((TOP_K, t_all), F32)],
        compiler_params=_cparams(1),
        name="xout_router",
    )(o_all, h_all, w_ox, g3, w_rt, b_r)


def _route_kernel(idx_ref, dest_ref, be_ref, nu_ref, *, t_all, n_blocks_pad):
    bm = MOE_BLOCK_ROWS
    nt = t_all // LANES
    e_col = lax.broadcasted_iota(I32, (N_EXPERTS, 1), 0)
    hp = lax.Precision.HIGHEST

    def multi_hot(j):
        blk = idx_ref[:, pl.ds(pl.multiple_of(j * LANES, LANES), LANES)]
        mh = jnp.zeros((N_EXPERTS, LANES), F32)
        for k in range(TOP_K):
            mh = mh + (e_col == blk[k:k + 1, :]).astype(F32)
        return blk, mh

    def count_body(j, c):
        _, mh = multi_hot(j)
        return c + jnp.sum(mh, axis=1, keepdims=True)

    counts = lax.fori_loop(0, nt, count_body, jnp.zeros((N_EXPERTS, 1), F32))
    padded = jnp.ceil(counts * (1.0 / bm)) * bm
    tri = (lax.broadcasted_iota(I32, (N_EXPERTS, N_EXPERTS), 1)
           <= lax.broadcasted_iota(I32, (N_EXPERTS, N_EXPERTS), 0)).astype(F32)
    pad_end = jnp.dot(tri, jnp.broadcast_to(padded, (N_EXPERTS, LANES)), precision=hp,
                      preferred_element_type=F32)
    pad_start = pad_end[:, 0:1] - padded
    upper = (lax.broadcasted_iota(I32, (LANES, LANES), 0)
             < lax.broadcasted_iota(I32, (LANES, LANES), 1)).astype(BF16)

    def dest_body(j, carry):
        blk, mh = multi_hot(j)
        rank = carry + _dot(mh.astype(BF16), upper)
        base = pad_start + rank
        for k in range(TOP_K):
            d = jnp.sum(jnp.where(e_col == blk[k:k + 1, :], base, 0.0), axis=0, keepdims=True)
            dest_ref[pl.ds(k, 1), pl.ds(pl.multiple_of(j * LANES, LANES), LANES)] = d.astype(I32)
        return carry + jnp.sum(mh, axis=1, keepdims=True)

    lax.fori_loop(0, nt, dest_body, jnp.zeros((N_EXPERTS, 1), F32))

    b_row = lax.broadcasted_iota(I32, (1, n_blocks_pad), 1).astype(F32) * bm
    be = jnp.sum((pad_end[:, 0:1] <= b_row).astype(F32), axis=0, keepdims=True)
    be_ref[...] = jnp.minimum(be, N_EXPERTS - 1.0).astype(I32)
    nu_ref[...] = (pad_end[N_EXPERTS - 1:N_EXPERTS, :] * (1.0 / bm)).astype(I32)


def _route(idx_t, n_blocks_pad):
    t_all = idx_t.shape[1]
    return pl.pallas_call(
        functools.partial(_route_kernel, t_all=t_all, n_blocks_pad=n_blocks_pad),
        out_shape=[jax.ShapeDtypeStruct((TOP_K, t_all), I32),
                   jax.ShapeDtypeStruct((1, n_blocks_pad), I32),
                   jax.ShapeDtypeStruct((1, LANES), I32)],
        compiler_params=pltpu.CompilerParams(vmem_limit_bytes=VMEM_LIMIT),
        name="moe_route",
    )(idx_t)


def _dispatch_kernel(dest_sm, x_ref, xs_in, xs_ref, sem, *, t_all):
    del xs_in
    i = pl.program_id(0)
    tm = x_ref.shape[0]

    def start_body(j, c):
        for k in range(TOP_K):
            d = dest_sm[k * t_all + i * tm + j]
            pltpu.make_async_copy(x_ref.at[j], xs_ref.at[d], sem).start(priority=k % 2)
        return c

    lax.fori_loop(0, tm, start_body, 0, unroll=ROW_DMA_UNROLL)
    for k in range(TOP_K):
        pltpu.make_async_copy(x_ref, xs_ref.at[pl.ds(0, tm)], sem).wait()


ROW_DMA_UNROLL = 4


def _dispatch(dest_flat, xn_all, xs_init):
    t_all = xn_all.shape[0]
    tm = TOKEN_TILE
    return pl.pallas_call(
        functools.partial(_dispatch_kernel, t_all=t_all),
        grid_spec=pltpu.PrefetchScalarGridSpec(
            num_scalar_prefetch=1,
            grid=(t_all // tm,),
            in_specs=[pl.BlockSpec((tm, ROW_TILE_ROWS, LANES), lambda i, d: (i, 0, 0)),
                      pl.BlockSpec(memory_space=pl.ANY)],
            out_specs=pl.BlockSpec(memory_space=pl.ANY),
            scratch_shapes=[pltpu.SemaphoreType.DMA]),
        out_shape=jax.ShapeDtypeStruct(xs_init.shape, F32),
        input_output_aliases={2: 0},
        compiler_params=_cparams(1),
        name="moe_dispatch",
    )(dest_flat, xn_all, xs_init)


def _expert_kernel(be_sm, nu_sm, x_ref, wgu_ref, bgu_ref, wd_ref, bd_ref, y_ref, wgu_bf, wd_bf):
    b = pl.program_id(0)
    changed = jnp.logical_or(b == 0, be_sm[b] != be_sm[jnp.maximum(b - 1, 0)])

    @pl.when(changed)
    def _():
        wgu_bf[...] = wgu_ref[0].astype(BF16)
        wd_bf[...] = wd_ref[0].astype(BF16)

    @pl.when(b < nu_sm[0])
    def _():
        x = _load_row_tiles(x_ref, MOE_BLOCK_ROWS).astype(BF16)
        h = _dot(x, wgu_bf[...]) + bgu_ref[0]
        glu = jnp.minimum(h[:, :D_FF], SWIGLU_LIMIT)
        lin = jnp.clip(h[:, D_FF:], -SWIGLU_LIMIT, SWIGLU_LIMIT)
        act = glu * _sigmoid(SWIGLU_ALPHA * glu) * (lin + 1.0)
        _store_row_tiles(y_ref, _dot(act.astype(BF16), wd_bf[...]) + bd_ref[0])

    @pl.when(b >= nu_sm[0])
    def _():
        y_ref[...] = jnp.zeros_like(y_ref)


def _experts(block_e, n_used, xs, w_gu, b_gu, w_down, b_down):
    cap = xs.shape[0] // ROW_TILE_ROWS
    bm = MOE_BLOCK_ROWS
    return pl.pallas_call(
        _expert_kernel,
        grid_spec=pltpu.PrefetchScalarGridSpec(
            num_scalar_prefetch=2,
            grid=(cap // bm,),
            in_specs=[pl.BlockSpec((bm * ROW_TILE_ROWS, LANES), lambda b, be, nu: (b, 0)),
                      pl.BlockSpec((1, D_MODEL, 2 * D_FF), lambda b, be, nu: (be[b], 0, 0)),
                      pl.BlockSpec((1, 1, 2 * D_FF), lambda b, be, nu: (be[b], 0, 0)),
                      pl.BlockSpec((1, D_FF, D_MODEL), lambda b, be, nu: (be[b], 0, 0)),
                      pl.BlockSpec((1, 1, D_MODEL), lambda b, be, nu: (be[b], 0, 0))],
            out_specs=pl.BlockSpec((bm * ROW_TILE_ROWS, LANES), lambda b, be, nu: (b, 0)),
            scratch_shapes=[pltpu.VMEM((D_MODEL, 2 * D_FF), BF16),
                            pltpu.VMEM((D_FF, D_MODEL), BF16)]),
        out_shape=jax.ShapeDtypeStruct((cap * ROW_TILE_ROWS, LANES), F32),
        compiler_params=_cparams(1, VMEM_LIMIT_EXPERTS),
        name="moe_experts",
    )(block_e, n_used, xs, w_gu, b_gu, w_down, b_down)


def _combine_kernel(dest_sm, yb_ref, h_ref, g_ref, op_ref, os_ref, buf, sems, *, t_all):
    i = pl.program_id(0)
    last = pl.num_programs(0) - 1
    tm = h_ref.shape[0]

    def gather(step, slot):
        def start_body(j, c):
            for k in range(TOP_K):
                d = dest_sm[k * t_all + step * tm + j]
                tile = pl.ds(pl.multiple_of(j * ROW_TILE_ROWS, ROW_TILE_ROWS), ROW_TILE_ROWS)
                pltpu.make_async_copy(yb_ref.at[d], buf.at[slot, k, tile],
                                      sems.at[slot]).start(priority=k % 2)
            return c

        lax.fori_loop(0, tm, start_body, 0, unroll=ROW_DMA_UNROLL)

    @pl.when(i == 0)
    def _():
        gather(0, 0)

    @pl.when(i < last)
    def _():
        gather(i + 1, (i + 1) % 2)

    slot = i % 2
    for k in range(TOP_K):
        pltpu.make_async_copy(buf.at[slot, k], buf.at[slot, k], sems.at[slot]).wait()

    g = g_ref[...]
    y = h_ref[...]
    for k in range(TOP_K):
        y = y + g[:, k:k + 1] * _load_row_tiles(buf.at[slot, k], tm)

    @pl.when(i < last)
    def _():
        op_ref[...] = y

    @pl.when(i == last)
    def _():
        os_ref[...] = y


def _combine(dest_flat, yb, h2_all, gates, n_prompt_rows):
    t_all = h2_all.shape[0]
    tm = ROW_TILE
    n_p_tiles = n_prompt_rows // tm
    assert t_all == n_prompt_rows + tm
    return pl.pallas_call(
        functools.partial(_combine_kernel, t_all=t_all),
        grid_spec=pltpu.PrefetchScalarGridSpec(
            num_scalar_prefetch=1,
            grid=(t_all // tm,),
            in_specs=[pl.BlockSpec(memory_space=pl.ANY),
                      pl.BlockSpec((tm, D_MODEL), lambda i, d: (i, 0)),
                      pl.BlockSpec((tm, TOP_K), lambda i, d: (i, 0))],
            out_specs=[pl.BlockSpec((tm, D_MODEL), lambda i, d: (jnp.minimum(i, n_p_tiles - 1), 0)),
                       pl.BlockSpec((tm, D_MODEL), lambda i, d: (0, 0))],
            scratch_shapes=[pltpu.VMEM((2, TOP_K, tm * ROW_TILE_ROWS, LANES), F32),
                            pltpu.SemaphoreType.DMA((2,))]),
        out_shape=[jax.ShapeDtypeStruct((n_prompt_rows, D_MODEL), F32),
                   jax.ShapeDtypeStruct((tm, D_MODEL), F32)],
        compiler_params=_cparams(1),
        name="moe_combine",
    )(dest_flat, yb, h2_all, gates)


def _rope_tables(pos):
    half = HEAD_DIM // 2
    inv = jnp.exp(-math.log(ROPE_THETA) * jnp.arange(half, dtype=F32) / half)
    ang = pos.astype(F32)[:, None] * inv[None, :]
    cos, sin = jnp.cos(ang), jnp.sin(ang)
    cos_t = jnp.concatenate([cos, cos] * HEADS_PER_TILE, axis=-1)
    sin_t = jnp.concatenate([-sin, sin] * HEADS_PER_TILE, axis=-1)
    return cos_t, sin_t


def _block_diag_ones(n, blk):
    r = jnp.arange(n) // blk
    return (r[:, None] == r[None, :]).astype(BF16)


def _layer(x_prompt, x_sample, state_ret, cache_win_k, cache_win_v, cache_mem_k, cache_mem_v,
           mem_prompt, g_norm1, w_in, g_att_q, g_att_k, g_ret_gn, w_out, g_norm2, g_mem,
           w_q_x, w_kv_x, g_x_q, g_x_k, w_o_x, g_norm3, w_router, b_router, w_gu, b_gu,
           w_down, b_down):
    n_b, seq, _ = x_prompt.shape
    n_s = x_sample.shape[0]
    assert x_sample.shape[1] == 1 and n_s == ROW_TILE
    t_p = n_b * seq
    t_all = t_p + n_s
    assert t_all % TOKEN_TILE == 0 and t_p % ROW_TILE == 0

    assert seq % PROMPT_TILE == 0
    x_p = x_prompt.reshape(t_p, D_MODEL)
    x_s = x_sample.reshape(n_s, D_MODEL)
    cos_p, sin_p = _rope_tables(jnp.arange(seq, dtype=jnp.int32))
    cos_s, sin_s = _rope_tables(jnp.full((n_s,), PAST_LEN, jnp.int32))
    log_g = jnp.log1p(-jnp.exp2(-5.0 - jnp.arange(RET_HEADS, dtype=F32)))
    lg_row = jnp.repeat(log_g, HEAD_DIM)[None, :]
    gn_row = g_ret_gn.reshape(1, RET_WIDTH)
    gn_col = g_ret_gn.reshape(RET_WIDTH, 1)
    gq = jnp.tile(g_att_q.reshape(1, HEAD_DIM), (1, ATT_HEADS))
    gk = jnp.tile(g_att_k.reshape(1, HEAD_DIM), (1, ATT_HEADS))
    seg = _block_diag_ones(ATT_WIDTH, HEAD_DIM)

    g1 = g_norm1.reshape(1, D_MODEL)
    w_in_bf = w_in.astype(BF16)
    proj = _in_proj(x_p, g1, w_in_bf, cos_p, sin_p, gq, gk, seg, t_all, PROMPT_TILE, 0)
    rq, rk, rv, rg, aq, ak, av = _in_proj(x_s, g1, w_in_bf, cos_s, sin_s, gq, gk, seg, t_all, n_s,
                                          t_p, prev=proj)

    ret_n, st_p = _ret_prompt(rq, rk, rv, rg, gn_row, lg_row, n_b, seq)
    att_o, wk_p, wv_p = _dil_prompt(aq, ak, av, n_b, seq)
    state_t = jnp.transpose(state_ret, (1, 2, 3, 0))
    ret_n, st_s = _ret_sample(rq, rk, rv, ret_n, gn_col, lg_row, state_t)
    ck = jnp.transpose(cache_win_k, (0, 2, 3, 1))
    cv = jnp.transpose(cache_win_v, (0, 2, 3, 1))
    wk_s, wv_s, att_o = _win_sample(att_o, ak, av, ck, cv)

    mix_w = (w_out.astype(BF16), g_norm2.reshape(1, D_MODEL), w_q_x.astype(BF16),
             g_x_q.reshape(1, X_HEAD_DIM))
    mixed = _mix_out(ret_n, att_o, x_p, *mix_w, PROMPT_TILE, 0)
    h_all, qx = _mix_out(ret_n, att_o, x_s, *mix_w, n_s, t_p, prev=mixed)

    mk_p, mv_p = _mem_kv(mem_prompt, g_mem.reshape(1, D_MODEL), w_kv_x.astype(BF16),
                         g_x_k.reshape(1, X_HEAD_DIM))
    o_all = _xattn_prompt(qx, mk_p, mv_p, n_b, seq)
    n_mem = cache_mem_k.shape[1]
    o_all = _xattn_sample(o_all, cache_mem_k.reshape(n_s, n_mem * X_HEADS, X_HEAD_DIM),
                          cache_mem_v.reshape(n_s, n_mem * X_HEADS, X_HEAD_DIM))

    h2_all, xn_all, idx_t, gate_t = _xout_router(
        o_all, h_all, w_o_x.astype(BF16), g_norm3.reshape(1, D_MODEL),
        jnp.transpose(w_router), b_router.reshape(N_EXPERTS, 1))

    bm = MOE_BLOCK_ROWS
    n_blocks = -(-(t_all * TOP_K) // bm) + N_EXPERTS
    n_blocks_pad = -(-n_blocks // LANES) * LANES
    dest_t, be, nu = _route(idx_t, n_blocks_pad)
    dest_flat = dest_t.reshape(TOP_K * t_all)
    cap = n_blocks * bm
    xs = _dispatch(dest_flat, xn_all.reshape(t_all, ROW_TILE_ROWS, LANES),
                   jnp.zeros((cap, ROW_TILE_ROWS, LANES), F32))
    yb = _experts(be.reshape(n_blocks_pad), nu.reshape(LANES), xs.reshape(cap * ROW_TILE_ROWS, LANES),
                  w_gu, b_gu.reshape(N_EXPERTS, 1, 2 * D_FF), w_down,
                  b_down.reshape(N_EXPERTS, 1, D_MODEL))
    y_p, y_s = _combine(dest_flat, yb.reshape(cap, ROW_TILE_ROWS, LANES), h2_all,
                        jnp.transpose(gate_t), t_p)

    st_p = jnp.stack([st_p[:, :, :HEAD_DIM, :HEAD_DIM], st_p[:, :, HEAD_DIM:, HEAD_DIM:]], axis=2)
    st_p = st_p.reshape(n_b, RET_HEADS, HEAD_DIM, HEAD_DIM)
    return (y_p.reshape(n_b, seq, D_MODEL),
            y_s.reshape(n_s, 1, D_MODEL),
            st_p,
            jnp.transpose(st_s, (3, 0, 1, 2)),
            jnp.transpose(wk_p, (0, 3, 1, 2)),
            jnp.transpose(wv_p, (0, 3, 1, 2)),
            jnp.transpose(wk_s, (0, 3, 1, 2)),
            jnp.transpose(wv_s, (0, 3, 1, 2)),
            mk_p.reshape(n_b, n_mem, X_HEADS, X_HEAD_DIM),
            mv_p.reshape(n_b, n_mem, X_HEADS, X_HEAD_DIM))


def kernel(x_prompt, x_sample, state_ret, cache_win_k, cache_win_v, cache_mem_k, cache_mem_v,
           mem_prompt, g_norm1, w_in, g_att_q, g_att_k, g_ret_gn, w_out, g_norm2, g_mem,
           w_q_x, w_kv_x, g_x_q, g_x_k, w_o_x, g_norm3, w_router, b_router, w_gu, b_gu,
           w_down, b_down):
    assert state_ret.shape[0] == 1, "single-layer trunk"
    outs = _layer(x_prompt, x_sample, state_ret[0], cache_win_k[0], cache_win_v[0],
                  cache_mem_k[0], cache_mem_v[0], mem_prompt, g_norm1[0], w_in[0], g_att_q[0],
                  g_att_k[0], g_ret_gn[0], w_out[0], g_norm2[0], g_mem[0], w_q_x[0], w_kv_x[0],
                  g_x_q[0], g_x_k[0], w_o_x[0], g_norm3[0], w_router[0], b_router[0], w_gu[0],
                  b_gu[0], w_down[0], b_down[0])
    y_p, y_s = outs[0], outs[1]
    return (y_p, y_s) + tuple(o[None] for o in outs[2:])
```

```python
import functools
import math

import jax
import jax.numpy as jnp
from jax import lax
from jax.experimental import pallas as pl
from jax.experimental.pallas import tpu as pltpu

F32 = jnp.float32
BF16 = jnp.bfloat16
I32 = jnp.int32

D_MODEL = 1024
HEAD_DIM = 64
RET_HEADS = 8
ATT_HEADS = 8
RET_WIDTH = RET_HEADS * HEAD_DIM
ATT_WIDTH = ATT_HEADS * HEAD_DIM
IN_COLS = 4 * RET_WIDTH + 3 * ATT_WIDTH
RET_CHUNK = 128
DIL_PATTERNS = ((128, 1), (512, 4), (2048, 16))
DIL_BLOCK = 128
MAX_WINDOW = 2048
X_HEADS = 4
X_HEAD_DIM = 128
X_WIDTH = X_HEADS * X_HEAD_DIM
N_EXPERTS = 32
TOP_K = 4
D_FF = D_MODEL
SWIGLU_LIMIT = 7.0
SWIGLU_ALPHA = 1.702
ROPE_THETA = 10000.0
EPS = 1e-6
PAST_LEN = 8192

LANES = 128
SUBLANES = 8
HEADS_PER_TILE = LANES // HEAD_DIM

TOKEN_TILE = 384
PROMPT_TILE = 512
ROW_TILE = 128
MOE_BLOCK_ROWS = 256
NEG_BIG = -1e30
VMEM_LIMIT = 48 * 1024 * 1024
VMEM_LIMIT_EXPERTS = 60 * 1024 * 1024


def _cparams(n_axes, vmem_limit=VMEM_LIMIT):
    return pltpu.CompilerParams(
        dimension_semantics=("arbitrary",) * n_axes, vmem_limit_bytes=vmem_limit)


def _rms(x, g):
    return x * lax.rsqrt(jnp.mean(x * x, axis=-1, keepdims=True) + EPS) * g


def _dot(a, b):
    return jnp.dot(a, b, preferred_element_type=F32)


def _dot_nt(a, b):
    return lax.dot_general(a, b, (((1,), (1,)), ((), ())), preferred_element_type=F32)


def _dot_tn(a, b):
    return lax.dot_general(a, b, (((0,), (0,)), ((), ())), preferred_element_type=F32)


def _sigmoid(x):
    return 1.0 / (1.0 + jnp.exp(-x))


N_PROJ_OUT = 7


def _in_proj_kernel(x_ref, g1_ref, w_ref, gq_ref, gk_ref, seg_ref, cos_ref, sin_ref, *rest):
    rq_ref, rk_ref, rv_ref, rg_ref, aq_ref, ak_ref, av_ref = rest[-N_PROJ_OUT:]
    xn = _rms(x_ref[...], g1_ref[...]).astype(BF16)
    proj = _dot(xn, w_ref[...])
    reps = RET_WIDTH // LANES
    cos = jnp.concatenate([cos_ref[...]] * reps, axis=-1)
    sin = jnp.concatenate([sin_ref[...]] * reps, axis=-1)
    lane = lax.broadcasted_iota(I32, (1, RET_WIDTH), 1)
    first_half = (lane % HEAD_DIM) < (HEAD_DIM // 2)
    seg = seg_ref[...]

    def rope(t):
        partner = jnp.where(first_half,
                            pltpu.roll(t, RET_WIDTH - HEAD_DIM // 2, 1),
                            pltpu.roll(t, HEAD_DIM // 2, 1))
        return t * cos + partner * sin

    def head_norm(t, g):
        ssum = _dot((t * t).astype(BF16), seg)
        return t * lax.rsqrt(ssum * (1.0 / HEAD_DIM) + EPS) * g

    w = RET_WIDTH
    rq_ref[...] = rope(proj[:, 0:w])
    rk_ref[...] = rope(proj[:, w:2 * w]) * (HEAD_DIM ** -0.5)
    rv_ref[...] = proj[:, 2 * w:3 * w]
    rg_ref[...] = proj[:, 3 * w:4 * w]
    aq_ref[...] = rope(head_norm(proj[:, 4 * w:5 * w], gq_ref[...]))
    ak_ref[...] = rope(head_norm(proj[:, 5 * w:6 * w], gk_ref[...]))
    av_ref[...] = proj[:, 6 * w:7 * w]


def _group_rows_call(kernel, name, shared_ins, row_ins, fixed_ins, tables, out_widths, t_all, tm,
                     row_start, prev):
    rows = row_ins[0].shape[0]
    off = row_start // tm
    assert rows % tm == 0 and row_start % tm == 0
    n_steps = rows // tm
    extra = 1 if (prev is None and row_start + rows < t_all) else 0
    src = lambda i: jnp.minimum(i, n_steps - 1)
    fixed = lambda i: (0, 0)
    in_specs = [pl.BlockSpec((tm, a.shape[1]), lambda i: (src(i) + off, 0)) for a in shared_ins]
    in_specs += [pl.BlockSpec((tm, a.shape[1]), lambda i: (src(i), 0)) for a in row_ins]
    in_specs += [pl.BlockSpec(a.shape, fixed) for a in fixed_ins]
    for t in tables:
        period = t.shape[0] // tm
        in_specs.append(pl.BlockSpec((tm, t.shape[1]),
                                     lambda i, period=period: (src(i) % period, 0)))
    n_in = len(in_specs)
    aliases = {}
    if prev is not None:
        in_specs += [pl.BlockSpec(memory_space=pl.ANY)] * len(prev)
        aliases = {n_in + j: j for j in range(len(prev))}
    return pl.pallas_call(
        kernel,
        grid=(n_steps + extra,),
        in_specs=in_specs,
        out_specs=[pl.BlockSpec((tm, w), lambda i: (i + off, 0)) for w in out_widths],
        out_shape=[jax.ShapeDtypeStruct((t_all, w), F32) for w in out_widths],
        input_output_aliases=aliases,
        compiler_params=_cparams(1),
        name=name,
    )(*shared_ins, *row_ins, *fixed_ins, *tables, *(prev or ()))


def _in_proj(x, g1, w_in, cos_t, sin_t, gq, gk, seg, t_all, tm, row_start, prev=None):
    return _group_rows_call(_in_proj_kernel, "in_proj", [], [x], [g1, w_in, gq, gk, seg],
                            [cos_t, sin_t], [RET_WIDTH] * N_PROJ_OUT, t_all, tm, row_start, prev)


def _group_norm_gate(o, mask_a, gn, gate):
    inv = 1.0 / HEAD_DIM
    sa = jnp.sum(jnp.where(mask_a, o, 0.0), axis=-1, keepdims=True)
    sb = jnp.sum(jnp.where(mask_a, 0.0, o), axis=-1, keepdims=True)
    cen = o - jnp.where(mask_a, sa, sb) * inv
    c2 = cen * cen
    va = jnp.sum(jnp.where(mask_a, c2, 0.0), axis=-1, keepdims=True)
    vb = jnp.sum(jnp.where(mask_a, 0.0, c2), axis=-1, keepdims=True)
    var = jnp.where(mask_a, va, vb) * inv
    return cen * lax.rsqrt(var + EPS) * gn * (gate * _sigmoid(gate))


def _ret_prompt_kernel(q_ref, k_ref, v_ref, g_ref, gn_ref, lg_ref, o_ref, st_ref,
                       state, dec, qdec, kdec):
    n = pl.program_id(0)
    c = pl.program_id(1)
    ch = RET_CHUNK
    n_pairs = RET_WIDTH // LANES
    lane = lax.broadcasted_iota(I32, (1, LANES), 1)
    mask_a = lane < HEAD_DIM
    row_i = lax.broadcasted_iota(I32, (ch, 1), 0)

    @pl.when(jnp.logical_and(n == 0, c == 0))
    def _():
        row = row_i.astype(F32)
        col = lax.broadcasted_iota(I32, (1, ch), 1).astype(F32)
        diff = row - col
        causal = diff >= 0.0
        dpos = jnp.maximum(diff, 0.0)
        lg = lg_ref[...]
        for h in range(RET_HEADS):
            dec[h] = jnp.where(causal, jnp.exp(dpos * lg[:, h * HEAD_DIM:h * HEAD_DIM + 1]), 0.0)
        qdec[...] = jnp.exp((row + 1.0) * lg)
        kdec[...] = jnp.exp((ch - 1.0 - row) * lg)

    @pl.when(c == 0)
    def _():
        state[...] = jnp.zeros_like(state)

    same_head = (row_i // HEAD_DIM) == (lane // HEAD_DIM)
    cdec = jnp.exp(ch * lg_ref[...])
    for p in range(n_pairs):
        sl = slice(p * LANES, (p + 1) * LANES)
        q = q_ref[:, sl]
        k = k_ref[:, sl]
        kb = k.astype(BF16)
        vb = v_ref[:, sl].astype(BF16)
        lhs = jnp.concatenate([jnp.where(mask_a, q, 0.0), jnp.where(mask_a, 0.0, q)],
                              axis=0).astype(BF16)
        s = _dot_nt(lhs, kb)
        pr = (s * jnp.concatenate([dec[2 * p], dec[2 * p + 1]], axis=0)).astype(BF16)
        o2 = _dot(pr, vb)
        o_intra = jnp.where(mask_a, o2[:ch], o2[ch:])
        st = state[p]
        o = o_intra + _dot(q.astype(BF16), st.astype(BF16)) * qdec[:, sl]
        upd = _dot_tn((k * kdec[:, sl]).astype(BF16), vb)
        state[p] = cdec[:, sl] * st + jnp.where(same_head, upd, 0.0)
        o_ref[:, sl] = _group_norm_gate(o, mask_a, gn_ref[:, sl], g_ref[:, sl])

    @pl.when(c == pl.num_programs(1) - 1)
    def _():
        st_ref[0] = state[...]


def _ret_prompt(rq, rk, rv, rg, gn_row, lg_row, n_batch, seq):
    t_all = rq.shape[0]
    n_pairs = RET_WIDTH // LANES
    n_chunks = seq // RET_CHUNK
    blk = pl.BlockSpec((RET_CHUNK, RET_WIDTH), lambda n, c: (n * n_chunks + c, 0))
    lane_row = pl.BlockSpec((1, RET_WIDTH), lambda n, c: (0, 0))
    return pl.pallas_call(
        _ret_prompt_kernel,
        grid=(n_batch, n_chunks),
        in_specs=[blk, blk, blk, blk, lane_row, lane_row],
        out_specs=[blk, pl.BlockSpec((1, n_pairs, LANES, LANES), lambda n, c: (n, 0, 0, 0))],
        out_shape=[jax.ShapeDtypeStruct((t_all, RET_WIDTH), F32),
                   jax.ShapeDtypeStruct((n_batch, n_pairs, LANES, LANES), F32)],
        scratch_shapes=[pltpu.VMEM((n_pairs, LANES, LANES), F32),
                        pltpu.VMEM((RET_HEADS, RET_CHUNK, RET_CHUNK), F32),
                        pltpu.VMEM((RET_CHUNK, RET_WIDTH), F32),
                        pltpu.VMEM((RET_CHUNK, RET_WIDTH), F32)],
        input_output_aliases={3: 0},
        compiler_params=_cparams(2),
        name="ret_prompt",
    )(rq, rk, rv, rg, gn_row, lg_row)


DIL_UNROLL = 4
DIL_MERGE_ROWS = 256


def _dil_prompt_kernel(q_ref, k_ref, v_ref, o_ref, wk_ref, wv_ref, acc, m_s, l_s, *, seq, wp):
    b = DIL_BLOCK
    lane = lax.broadcasted_iota(I32, (1, LANES), 1)
    mask_a = lane < HEAD_DIM
    qi = lax.broadcasted_iota(I32, (2 * b, 1), 0) % b
    kk = lax.broadcasted_iota(I32, (1, 2 * b), 1)
    dist = qi + b - kk
    scale = HEAD_DIM ** -0.5

    for pat, (window, dil) in enumerate(DIL_PATTERNS):
        steps = window // dil
        band = (dist >= 0) & (dist <= steps)
        nb = seq // (b * dil)

        def block(idx, dil=dil, nb=nb, band=band):
            r = idx // nb
            i = idx % nb
            rows_q = pl.ds(r + i * (b * dil), b, stride=dil)
            rows_p = pl.ds(r + jnp.maximum(i - 1, 0) * (b * dil), b, stride=dil)
            q = q_ref[rows_q, :]
            kc = jnp.concatenate([k_ref[rows_p, :], k_ref[rows_q, :]], axis=0).astype(BF16)
            vc = jnp.concatenate([v_ref[rows_p, :], v_ref[rows_q, :]], axis=0).astype(BF16)
            lhs = jnp.concatenate([jnp.where(mask_a, q, 0.0), jnp.where(mask_a, 0.0, q)],
                                  axis=0).astype(BF16)
            s = _dot_nt(lhs, kc) * scale
            valid = band & ((kk >= b) | (jnp.broadcast_to(i, kk.shape) > 0))
            s = jnp.where(valid, s, NEG_BIG)
            m_blk = jnp.max(s, axis=-1, keepdims=True)
            p = jnp.exp(s - m_blk)
            l_blk = jnp.sum(p, axis=-1, keepdims=True)
            pv = _dot(p.astype(BF16), vc)
            return rows_q, m_blk, l_blk, pv

        def emit(rows_q, m_blk, l_blk, pv, pat=pat):
            acc[pat, rows_q, :] = jnp.where(mask_a, pv[:b], pv[b:])
            m_s[pat, rows_q, :] = jnp.where(mask_a, m_blk[:b], m_blk[b:])
            l_s[pat, rows_q, :] = jnp.where(mask_a, l_blk[:b], l_blk[b:])

        def body(it, carry, block=block, emit=emit):
            parts = [block(it * DIL_UNROLL + u) for u in range(DIL_UNROLL)]
            for part in parts:
                emit(*part)
            return carry

        lax.fori_loop(0, (dil * nb) // DIL_UNROLL, body, 0)

    n_pat = len(DIL_PATTERNS)

    def merge_body(c, carry):
        rows = pl.ds(pl.multiple_of(c * DIL_MERGE_ROWS, DIL_MERGE_ROWS), DIL_MERGE_ROWS)
        ms = [m_s[pat, rows, :] for pat in range(n_pat)]
        m = functools.reduce(jnp.maximum, ms)
        num = jnp.zeros((DIL_MERGE_ROWS, LANES), F32)
        den = jnp.zeros((DIL_MERGE_ROWS, LANES), F32)
        for pat in range(n_pat):
            w = jnp.exp(ms[pat] - m)
            num = num + w * acc[pat, rows, :]
            den = den + w * l_s[pat, rows, :]
        o_ref[rows, :] = num / den
        return carry

    lax.fori_loop(0, seq // DIL_MERGE_ROWS, merge_body, 0)

    tchunk = 4 * LANES
    for j in range(wp // tchunk):
        rows = pl.ds(seq - wp + j * tchunk, tchunk)
        cols = slice(j * tchunk, (j + 1) * tchunk)
        kt = k_ref[rows, :].T
        vt = v_ref[rows, :].T
        for hh in range(HEADS_PER_TILE):
            wk_ref[0, hh, :, cols] = kt[hh * HEAD_DIM:(hh + 1) * HEAD_DIM]
            wv_ref[0, hh, :, cols] = vt[hh * HEAD_DIM:(hh + 1) * HEAD_DIM]


def _dil_prompt(aq, ak, av, n_batch, seq):
    t_all = aq.shape[0]
    n_pairs = ATT_WIDTH // LANES
    wp = min(MAX_WINDOW, seq)
    blk = pl.BlockSpec((seq, LANES), lambda n, p: (n, p))
    wblk = pl.BlockSpec((1, HEADS_PER_TILE, HEAD_DIM, wp), lambda n, p: (n, p, 0, 0))
    wshape = jax.ShapeDtypeStruct((n_batch, ATT_HEADS, HEAD_DIM, wp), F32)
    return pl.pallas_call(
        functools.partial(_dil_prompt_kernel, seq=seq, wp=wp),
        grid=(n_batch, n_pairs),
        in_specs=[blk, blk, blk],
        out_specs=[blk, wblk, wblk],
        out_shape=[jax.ShapeDtypeStruct((t_all, ATT_WIDTH), F32), wshape, wshape],
        scratch_shapes=[pltpu.VMEM((len(DIL_PATTERNS), seq, LANES), F32)] * 3,
        input_output_aliases={0: 0},
        compiler_params=_cparams(2),
        name="dil_prompt",
    )(aq, ak, av)


def _ret_sample_kernel(q_ref, k_ref, v_ref, g_ref, gn_ref, lg_ref, st_ref,
                       o_ref, nst_ref, qt, kt):
    qt[...] = q_ref[...].T
    kt[...] = k_ref[...].T
    vt = v_ref[...].T
    gt = g_ref[...].T
    lg = lg_ref[...]
    outs = []
    for hh in range(HEADS_PER_TILE):
        lo = hh * HEAD_DIM
        gdec = jnp.exp(lg[:, lo:lo + 1])
        vth = vt[lo:lo + HEAD_DIM, :]

        def body(d, o, hh=hh, lo=lo, gdec=gdec, vth=vth):
            new = gdec * st_ref[hh, d] + kt[pl.ds(lo + d, 1), :] * vth
            nst_ref[hh, d] = new
            return o + qt[pl.ds(lo + d, 1), :] * new

        o = lax.fori_loop(0, HEAD_DIM, body, jnp.zeros_like(vth), unroll=8)
        mu = jnp.mean(o, axis=0, keepdims=True)
        cen = o - mu
        var = jnp.mean(cen * cen, axis=0, keepdims=True)
        gate = gt[lo:lo + HEAD_DIM, :]
        outs.append(cen * lax.rsqrt(var + EPS) * gn_ref[lo:lo + HEAD_DIM, :] * (gate * _sigmoid(gate)))
    o_ref[...] = jnp.concatenate(outs, axis=0).T


def _ret_sample(rq, rk, rv, ret_n, gn_col, lg_row, state_t):
    t_all = rq.shape[0]
    n_s = state_t.shape[-1]
    n_pairs = RET_WIDTH // LANES
    last = t_all // n_s - 1
    blk = pl.BlockSpec((n_s, LANES), lambda p: (last, p))
    st_blk = pl.BlockSpec((HEADS_PER_TILE, HEAD_DIM, HEAD_DIM, n_s), lambda p: (p, 0, 0, 0))
    return pl.pallas_call(
        _ret_sample_kernel,
        grid=(n_pairs,),
        in_specs=[blk, blk, blk, blk,
                  pl.BlockSpec((LANES, 1), lambda p: (p, 0)),
                  pl.BlockSpec((1, LANES), lambda p: (0, p)),
                  st_blk],
        out_specs=[blk, st_blk],
        out_shape=[jax.ShapeDtypeStruct(ret_n.shape, F32),
                   jax.ShapeDtypeStruct(state_t.shape, F32)],
        scratch_shapes=[pltpu.VMEM((LANES, n_s), F32), pltpu.VMEM((LANES, n_s), F32)],
        input_output_aliases={3: 0},
        compiler_params=_cparams(1),
        name="ret_sample",
    )(rq, rk, rv, ret_n, gn_col, lg_row, state_t)


def _win_sample_kernel(aq_ref, akn_ref, avn_ref, kc_ref, vc_ref,
                       tk_ref, tv_ref, att_ref, qt, kt, vt, acct, *, win):
    n = pl.program_id(0)
    n_s = qt.shape[1]

    @pl.when(n == 0)
    def _():
        qt[...] = aq_ref[...].T
        kt[...] = akn_ref[...].T
        vt[...] = avn_ref[...].T
        acct[...] = jnp.zeros_like(acct)

    onehot = (lax.broadcasted_iota(I32, (n_s, LANES), 0) == n).astype(F32)
    hp = lax.Precision.HIGHEST
    qb = jnp.dot(qt[...], onehot, precision=hp, preferred_element_type=F32)
    kb = jnp.dot(kt[...], onehot, precision=hp, preferred_element_type=F32)
    vb = jnp.dot(vt[...], onehot, precision=hp, preferred_element_type=F32)

    w_pos = lax.broadcasted_iota(I32, (1, win), 1)
    back = win - w_pos
    mult = jnp.zeros((1, win), F32)
    for window, dil in DIL_PATTERNS:
        mult = mult + ((back <= window) & (back % dil == 0)).astype(F32)
    valid = mult > 0.0
    tail_last = lax.broadcasted_iota(I32, (1, LANES), 1) == LANES - 1
    reps = win // LANES
    scale = HEAD_DIM ** -0.5

    s_rows, s0_rows = [], []
    for h in range(ATT_HEADS):
        lo = h * HEAD_DIM
        k_t = kc_ref[0, h]
        qh = qb[lo:lo + HEAD_DIM, :]
        kh = kb[lo:lo + HEAD_DIM, :]
        s_rows.append(jnp.sum(k_t * jnp.concatenate([qh] * reps, axis=1), axis=0, keepdims=True))
        s0_rows.append(jnp.sum(qh * kh, axis=0, keepdims=True)[:, 0:1])
        tk_ref[0, h] = jnp.where(tail_last, kh, pltpu.roll(k_t[:, win - LANES:], LANES - 1, 1))
    s = jnp.concatenate(s_rows, axis=0) * scale
    s0 = jnp.concatenate(s0_rows, axis=0) * scale
    m = jnp.maximum(jnp.max(jnp.where(valid, s, NEG_BIG), axis=-1, keepdims=True), s0)
    e = jnp.where(valid, jnp.exp(s - m), 0.0) * mult
    e0 = len(DIL_PATTERNS) * jnp.exp(s0 - m)
    denom = jnp.sum(e, axis=-1, keepdims=True) + e0

    cols = []
    for h in range(ATT_HEADS):
        lo = h * HEAD_DIM
        v_t = vc_ref[0, h]
        vh = vb[lo:lo + HEAD_DIM, :]
        num = jnp.sum(v_t * e[h:h + 1, :], axis=1, keepdims=True) + e0[h:h + 1, :] * vh[:, 0:1]
        cols.append(num / denom[h:h + 1, :])
        tv_ref[0, h] = jnp.where(tail_last, vh, pltpu.roll(v_t[:, win - LANES:], LANES - 1, 1))
    o_col = jnp.concatenate(cols, axis=0)
    lane_n = lax.broadcasted_iota(I32, (1, n_s), 1) == n
    acct[...] = jnp.where(lane_n, o_col, acct[...])

    @pl.when(n == pl.num_programs(0) - 1)
    def _():
        att_ref[...] = acct[...].T


def _win_sample(att_o, ak, av, cache_k, cache_v):
    n_s, n_h, hd, win = cache_k.shape
    assert win >= max(w for w, _ in DIL_PATTERNS)
    t_all = att_o.shape[0]
    last = t_all // n_s - 1
    rows = pl.BlockSpec((n_s, ATT_WIDTH), lambda n: (last, 0))
    cblk = pl.BlockSpec((1, n_h, hd, win), lambda n: (n, 0, 0, 0))
    tblk = pl.BlockSpec((1, n_h, hd, LANES), lambda n: (n, 0, 0, 0))
    tshape = jax.ShapeDtypeStruct((n_s, n_h, hd, LANES), F32)
    return pl.pallas_call(
        functools.partial(_win_sample_kernel, win=win),
        grid=(n_s,),
        in_specs=[rows, rows, rows, cblk, cblk],
        out_specs=[tblk, tblk, rows],
        out_shape=[tshape, tshape, jax.ShapeDtypeStruct(att_o.shape, F32)],
        scratch_shapes=[pltpu.VMEM((ATT_WIDTH, n_s), F32)] * 4,
        input_output_aliases={0: 2},
        compiler_params=_cparams(1),
        name="win_sample",
    )(att_o, ak, av, cache_k, cache_v)


def _lane_tile_norm(t, g, n_tiles):
    outs = []
    for h in range(n_tiles):
        outs.append(_rms(t[:, h * LANES:(h + 1) * LANES], g))
    return jnp.concatenate(outs, axis=-1)


def _mem_kv_kernel(mem_ref, gm_ref, w_ref, gk_ref, k_ref, v_ref):
    xn = _rms(mem_ref[0], gm_ref[...]).astype(BF16)
    kv = _dot(xn, w_ref[...])
    k_ref[0] = _lane_tile_norm(kv[:, :X_WIDTH], gk_ref[...], X_HEADS)
    v_ref[0] = kv[:, X_WIDTH:]


def _mem_kv(mem, g_mem, w_kv, g_xk):
    n, m, _ = mem.shape
    out = jax.ShapeDtypeStruct((n, m, X_WIDTH), F32)
    return pl.pallas_call(
        _mem_kv_kernel,
        grid=(n,),
        in_specs=[pl.BlockSpec((1, m, D_MODEL), lambda i: (i, 0, 0)),
                  pl.BlockSpec((1, D_MODEL), lambda i: (0, 0)),
                  pl.BlockSpec((D_MODEL, 2 * X_WIDTH), lambda i: (0, 0)),
                  pl.BlockSpec((1, X_HEAD_DIM), lambda i: (0, 0))],
        out_specs=[pl.BlockSpec((1, m, X_WIDTH), lambda i: (i, 0, 0))] * 2,
        out_shape=[out, out],
        compiler_params=_cparams(1),
        name="mem_kv",
    )(mem, g_mem, w_kv, g_xk)


def _mix_out_kernel(ret_ref, att_ref, x_ref, wo_ref, g2_ref, wq_ref, gq_ref, *rest):
    h_ref, q_ref = rest[-2:]
    mixed = jnp.concatenate([ret_ref[...], att_ref[...]], axis=-1).astype(BF16)
    h = x_ref[...] + _dot(mixed, wo_ref[...])
    h_ref[...] = h
    q = _dot(_rms(h, g2_ref[...]).astype(BF16), wq_ref[...])
    q_ref[...] = _lane_tile_norm(q, gq_ref[...], X_HEADS)


def _mix_out(ret_n, att_o, x, w_out, g2, w_qx, g_xq, tm, row_start, prev=None):
    return _group_rows_call(_mix_out_kernel, "mix_out", [ret_n, att_o], [x],
                            [w_out, g2, w_qx, g_xq], [], [D_MODEL, X_WIDTH], ret_n.shape[0], tm,
                            row_start, prev)


def _xattn_prompt_kernel(q_ref, k_ref, v_ref, o_ref):
    q = q_ref[...]
    k = k_ref[0]
    v = v_ref[0]
    scale = X_HEAD_DIM ** -0.5
    outs = []
    for h in range(X_HEADS):
        sl = slice(h * LANES, (h + 1) * LANES)
        s = _dot_nt(q[:, sl].astype(BF16), k[:, sl].astype(BF16)) * scale
        p = jnp.exp(s - jnp.max(s, axis=-1, keepdims=True))
        o = _dot(p.astype(BF16), v[:, sl].astype(BF16))
        outs.append(o / jnp.sum(p, axis=-1, keepdims=True))
    o_ref[...] = jnp.concatenate(outs, axis=-1)


def _xattn_prompt(qx, mk, mv, n_batch, seq, tq=512):
    t_all = qx.shape[0]
    per = seq // tq
    m = mk.shape[1]
    rows = pl.BlockSpec((tq, X_WIDTH), lambda n, i: (n * per + i, 0))
    mem = pl.BlockSpec((1, m, X_WIDTH), lambda n, i: (n, 0, 0))
    return pl.pallas_call(
        _xattn_prompt_kernel,
        grid=(n_batch, per),
        in_specs=[rows, mem, mem],
        out_specs=rows,
        out_shape=jax.ShapeDtypeStruct((t_all, X_WIDTH), F32),
        input_output_aliases={0: 0},
        compiler_params=_cparams(2),
        name="xattn_prompt",
    )(qx, mk, mv)


def _xattn_sample_kernel(q_ref, k_ref, v_ref, o_ref, *, group, n_mem):
    i = pl.program_id(0)
    scale = X_HEAD_DIM ** -0.5
    for j in range(group):
        n = i * group + j
        q = q_ref[pl.ds(n, 1), :]
        outs = []
        for h in range(X_HEADS):
            rows = pl.ds(h, n_mem, stride=X_HEADS)
            qh = q[:, h * LANES:(h + 1) * LANES]
            s = jnp.sum(k_ref[j, rows, :] * qh, axis=-1, keepdims=True) * scale
            p = jnp.exp(s - jnp.max(s, axis=0, keepdims=True))
            o = jnp.sum(p * v_ref[j, rows, :], axis=0, keepdims=True)
            outs.append(o / jnp.sum(p, axis=0, keepdims=True))
        o_ref[pl.ds(n, 1), :] = jnp.concatenate(outs, axis=-1)


XATTN_SAMPLE_GROUP = 8


def _xattn_sample(o_all, mk, mv):
    n_s, mh, _ = mk.shape
    t_all = o_all.shape[0]
    last = t_all // n_s - 1
    group = XATTN_SAMPLE_GROUP
    rows = pl.BlockSpec((n_s, X_WIDTH), lambda n: (last, 0))
    mem = pl.BlockSpec((group, mh, X_HEAD_DIM), lambda n: (n, 0, 0))
    return pl.pallas_call(
        functools.partial(_xattn_sample_kernel, group=group, n_mem=mh // X_HEADS),
        grid=(n_s // group,),
        in_specs=[rows, mem, mem],
        out_specs=rows,
        out_shape=jax.ShapeDtypeStruct(o_all.shape, F32),
        input_output_aliases={0: 0},
        compiler_params=_cparams(1),
        name="xattn_sample",
    )(o_all, mk, mv)


ROW_TILE_ROWS = D_MODEL // LANES


def _store_row_tiles(ref, x):
    rows = x.shape[0]
    for c in range(ROW_TILE_ROWS):
        ref[pl.ds(c, rows, stride=ROW_TILE_ROWS), :] = x[:, c * LANES:(c + 1) * LANES]


def _load_row_tiles(ref, rows):
    return jnp.concatenate([ref[pl.ds(c, rows, stride=ROW_TILE_ROWS), :]
                            for c in range(ROW_TILE_ROWS)], axis=1)


def _xout_router_kernel(o_ref, h_ref, wo_ref, g3_ref, wr_ref, br_ref,
                        h2_ref, xn_ref, idx_ref, gate_ref):
    h2 = h_ref[...] + _dot(o_ref[...].astype(BF16), wo_ref[...])
    h2_ref[...] = h2
    xn = _rms(h2, g3_ref[...])
    _store_row_tiles(xn_ref, xn)
    logits = lax.dot_general(wr_ref[...], xn, (((1,), (1,)), ((), ())),
                             precision=lax.Precision.HIGHEST,
                             preferred_element_type=F32) + br_ref[...]
    eid = lax.broadcasted_iota(I32, logits.shape, 0)
    work = logits
    vals, idxs = [], []
    for _ in range(TOP_K):
        mx = jnp.max(work, axis=0, keepdims=True)
        ix = jnp.min(jnp.where(work == mx, eid, N_EXPERTS), axis=0, keepdims=True)
        vals.append(mx)
        idxs.append(ix)
        work = jnp.where(eid == ix, -jnp.inf, work)
    ex = [jnp.exp(v - vals[0]) for v in vals]
    tot = ex[0] + ex[1] + ex[2] + ex[3]
    idx_ref[...] = jnp.concatenate(idxs, axis=0)
    gate_ref[...] = jnp.concatenate([e / tot for e in ex], axis=0)


def _xout_router(o_all, h_all, w_ox, g3, w_rt, b_r):
    t_all = h_all.shape[0]
    tm = TOKEN_TILE
    row = lambda i: (i, 0)
    fixed = lambda i: (0, 0)
    colblk = lambda i: (0, i)
    return pl.pallas_call(
        _xout_router_kernel,
        grid=(t_all // tm,),
        in_specs=[pl.BlockSpec((tm, X_WIDTH), row),
                  pl.BlockSpec((tm, D_MODEL), row),
                  pl.BlockSpec((X_WIDTH, D_MODEL), fixed),
                  pl.BlockSpec((1, D_MODEL), fixed),
                  pl.BlockSpec((N_EXPERTS, D_MODEL), fixed),
                  pl.BlockSpec((N_EXPERTS, 1), fixed)],
        out_specs=[pl.BlockSpec((tm, D_MODEL), row),
                   pl.BlockSpec((tm * ROW_TILE_ROWS, LANES), row),
                   pl.BlockSpec((TOP_K, tm), colblk), pl.BlockSpec((TOP_K, tm), colblk)],
        out_shape=[jax.ShapeDtypeStruct((t_all, D_MODEL), F32),
                   jax.ShapeDtypeStruct((t_all * ROW_TILE_ROWS, LANES), F32),
                   jax.ShapeDtypeStruct((TOP_K, t_all), I32),
                   jax.ShapeDtypeStruct((TOP_K, t_all), F32)],
        compiler_params=_cparams(1),
        name="xout_router",
    )(o_all, h_all, w_ox, g3, w_rt, b_r)


def _route_kernel(idx_ref, dest_ref, be_ref, nu_ref, *, t_all, n_blocks_pad):
    bm = MOE_BLOCK_ROWS
    nt = t_all // LANES
    e_col = lax.broadcasted_iota(I32, (N_EXPERTS, 1), 0)
    hp = lax.Precision.HIGHEST

    def multi_hot(j):
        blk = idx_ref[:, pl.ds(pl.multiple_of(j * LANES, LANES), LANES)]
        mh = jnp.zeros((N_EXPERTS, LANES), F32)
        for k in range(TOP_K):
            mh = mh + (e_col == blk[k:k + 1, :]).astype(F32)
        return blk, mh

    def count_body(j, c):
        _, mh = multi_hot(j)
        return c + jnp.sum(mh, axis=1, keepdims=True)

    counts = lax.fori_loop(0, nt, count_body, jnp.zeros((N_EXPERTS, 1), F32))
    padded = jnp.ceil(counts * (1.0 / bm)) * bm
    tri = (lax.broadcasted_iota(I32, (N_EXPERTS, N_EXPERTS), 1)
           <= lax.broadcasted_iota(I32, (N_EXPERTS, N_EXPERTS), 0)).astype(F32)
    pad_end = jnp.dot(tri, jnp.broadcast_to(padded, (N_EXPERTS, LANES)), precision=hp,
                      preferred_element_type=F32)
    pad_start = pad_end[:, 0:1] - padded
    upper = (lax.broadcasted_iota(I32, (LANES, LANES), 0)
             < lax.broadcasted_iota(I32, (LANES, LANES), 1)).astype(BF16)

    def dest_body(j, carry):
        blk, mh = multi_hot(j)
        rank = carry + _dot(mh.astype(BF16), upper)
        base = pad_start + rank
        for k in range(TOP_K):
            d = jnp.sum(jnp.where(e_col == blk[k:k + 1, :], base, 0.0), axis=0, keepdims=True)
            dest_ref[pl.ds(k, 1), pl.ds(pl.multiple_of(j * LANES, LANES), LANES)] = d.astype(I32)
        return carry + jnp.sum(mh, axis=1, keepdims=True)

    lax.fori_loop(0, nt, dest_body, jnp.zeros((N_EXPERTS, 1), F32))

    b_row = lax.broadcasted_iota(I32, (1, n_blocks_pad), 1).astype(F32) * bm
    be = jnp.sum((pad_end[:, 0:1] <= b_row).astype(F32), axis=0, keepdims=True)
    be_ref[...] = jnp.minimum(be, N_EXPERTS - 1.0).astype(I32)
    nu_ref[...] = (pad_end[N_EXPERTS - 1:N_EXPERTS, :] * (1.0 / bm)).astype(I32)


def _route(idx_t, n_blocks_pad):
    t_all = idx_t.shape[1]
    return pl.pallas_call(
        functools.partial(_route_kernel, t_all=t_all, n_blocks_pad=n_blocks_pad),
        out_shape=[jax.ShapeDtypeStruct((TOP_K, t_all), I32),
                   jax.ShapeDtypeStruct((1, n_blocks_pad), I32),
                   jax.ShapeDtypeStruct((1, LANES), I32)],
        compiler_params=pltpu.CompilerParams(vmem_limit_bytes=VMEM_LIMIT),
        name="moe_route",
    )(idx_t)


def _dispatch_kernel(dest_sm, x_ref, xs_in, xs_ref, sem, *, t_all):
    del xs_in
    i = pl.program_id(0)
    tm = x_ref.shape[0]

    def start_body(j, c):
        for k in range(TOP_K):
            d = dest_sm[k * t_all + i * tm + j]
            pltpu.make_async_copy(x_ref.at[j], xs_ref.at[d], sem).start(priority=k % 2)
        return c

    lax.fori_loop(0, tm, start_body, 0, unroll=ROW_DMA_UNROLL)
    for k in range(TOP_K):
        pltpu.make_async_copy(x_ref, xs_ref.at[pl.ds(0, tm)], sem).wait()


ROW_DMA_UNROLL = 4


def _dispatch(dest_flat, xn_all, xs_init):
    t_all = xn_all.shape[0]
    tm = TOKEN_TILE
    return pl.pallas_call(
        functools.partial(_dispatch_kernel, t_all=t_all),
        grid_spec=pltpu.PrefetchScalarGridSpec(
            num_scalar_prefetch=1,
            grid=(t_all // tm,),
            in_specs=[pl.BlockSpec((tm, ROW_TILE_ROWS, LANES), lambda i, d: (i, 0, 0)),
                      pl.BlockSpec(memory_space=pl.ANY)],
            out_specs=pl.BlockSpec(memory_space=pl.ANY),
            scratch_shapes=[pltpu.SemaphoreType.DMA]),
        out_shape=jax.ShapeDtypeStruct(xs_init.shape, F32),
        input_output_aliases={2: 0},
        compiler_params=_cparams(1),
        name="moe_dispatch",
    )(dest_flat, xn_all, xs_init)


CACHE_UNIT_HEADS = ATT_HEADS // 2


def _cache_shift_step(b, ck_ref, cv_ref, tk_ref, tv_ref, ko_ref, vo_ref,
                      kin, vin, tkin, tvin, kout, vout, in_sems, out_sems, *, n_units, win):
    uh = CACHE_UNIT_HEADS

    def loads(u, slot):
        n, hs = u // 2, pl.ds((u % 2) * uh, uh)
        return [pltpu.make_async_copy(ck_ref.at[n, hs], kin.at[slot], in_sems.at[slot]),
                pltpu.make_async_copy(cv_ref.at[n, hs], vin.at[slot], in_sems.at[slot]),
                pltpu.make_async_copy(tk_ref.at[n, hs], tkin.at[slot], in_sems.at[slot]),
                pltpu.make_async_copy(tv_ref.at[n, hs], tvin.at[slot], in_sems.at[slot])]

    def stores(u, slot):
        n, hs = u // 2, pl.ds((u % 2) * uh, uh)
        return [pltpu.make_async_copy(kout.at[slot], ko_ref.at[n, hs], out_sems.at[slot]),
                pltpu.make_async_copy(vout.at[slot], vo_ref.at[n, hs], out_sems.at[slot])]

    @pl.when(b == 0)
    def _():
        for cp in loads(0, 0):
            cp.start()

    @pl.when(b + 1 < n_units)
    def _():
        for cp in loads(b + 1, (b + 1) % 2):
            cp.start()

    @pl.when(jnp.logical_and(b >= 2, b < n_units + 2))
    def _():
        for cp in stores(b - 2, b % 2):
            cp.wait()

    @pl.when(b < n_units)
    def _():
        slot = b % 2
        for cp in loads(b, slot):
            cp.wait()
        for src, tail, dst in ((kin, tkin, kout), (vin, tvin, vout)):
            for h in range(uh):
                shifted = pltpu.roll(src[slot, h], win - 1, 1)
                dst[slot, h] = jnp.concatenate([shifted[:, :win - LANES], tail[slot, h]], axis=1)
        for cp in stores(b, slot):
            cp.start()


def _expert_kernel(be_sm, nu_sm, x_ref, wgu_ref, bgu_ref, wd_ref, bd_ref,
                   ck_ref, cv_ref, tk_ref, tv_ref, y_ref, ko_ref, vo_ref, wgu_bf, wd_bf,
                   kin, vin, tkin, tvin, kout, vout, in_sems, out_sems, *, n_units, win):
    b = pl.program_id(0)
    _cache_shift_step(b, ck_ref, cv_ref, tk_ref, tv_ref, ko_ref, vo_ref, kin, vin, tkin, tvin,
                      kout, vout, in_sems, out_sems, n_units=n_units, win=win)
    changed = jnp.logical_or(b == 0, be_sm[b] != be_sm[jnp.maximum(b - 1, 0)])

    @pl.when(changed)
    def _():
        wgu_bf[...] = wgu_ref[0].astype(BF16)
        wd_bf[...] = wd_ref[0].astype(BF16)

    @pl.when(b < nu_sm[0])
    def _():
        x = _load_row_tiles(x_ref, MOE_BLOCK_ROWS).astype(BF16)
        h = _dot(x, wgu_bf[...]) + bgu_ref[0]
        glu = jnp.minimum(h[:, :D_FF], SWIGLU_LIMIT)
        lin = jnp.clip(h[:, D_FF:], -SWIGLU_LIMIT, SWIGLU_LIMIT)
        act = glu * _sigmoid(SWIGLU_ALPHA * glu) * (lin + 1.0)
        _store_row_tiles(y_ref, _dot(act.astype(BF16), wd_bf[...]) + bd_ref[0])

    @pl.when(b >= nu_sm[0])
    def _():
        y_ref[...] = jnp.zeros_like(y_ref)


def _experts(block_e, n_used, xs, w_gu, b_gu, w_down, b_down, cache_k, cache_v, tail_k, tail_v):
    cap = xs.shape[0] // ROW_TILE_ROWS
    bm = MOE_BLOCK_ROWS
    n_s, n_h, hd, win = cache_k.shape
    n_units = 2 * n_s
    assert n_h == 2 * CACHE_UNIT_HEADS and cap // bm >= n_units + 2
    uh = CACHE_UNIT_HEADS
    hbm = pl.BlockSpec(memory_space=pl.ANY)
    cshape = jax.ShapeDtypeStruct(cache_k.shape, F32)
    return pl.pallas_call(
        functools.partial(_expert_kernel, n_units=n_units, win=win),
        grid_spec=pltpu.PrefetchScalarGridSpec(
            num_scalar_prefetch=2,
            grid=(cap // bm,),
            in_specs=[pl.BlockSpec((bm * ROW_TILE_ROWS, LANES), lambda b, be, nu: (b, 0)),
                      pl.BlockSpec((1, D_MODEL, 2 * D_FF), lambda b, be, nu: (be[b], 0, 0)),
                      pl.BlockSpec((1, 1, 2 * D_FF), lambda b, be, nu: (be[b], 0, 0)),
                      pl.BlockSpec((1, D_FF, D_MODEL), lambda b, be, nu: (be[b], 0, 0)),
                      pl.BlockSpec((1, 1, D_MODEL), lambda b, be, nu: (be[b], 0, 0)),
                      hbm, hbm, hbm, hbm],
            out_specs=[pl.BlockSpec((bm * ROW_TILE_ROWS, LANES), lambda b, be, nu: (b, 0)),
                       hbm, hbm],
            scratch_shapes=[pltpu.VMEM((D_MODEL, 2 * D_FF), BF16),
                            pltpu.VMEM((D_FF, D_MODEL), BF16),
                            pltpu.VMEM((2, uh, hd, win), F32), pltpu.VMEM((2, uh, hd, win), F32),
                            pltpu.VMEM((2, uh, hd, LANES), F32), pltpu.VMEM((2, uh, hd, LANES), F32),
                            pltpu.VMEM((2, uh, hd, win), F32), pltpu.VMEM((2, uh, hd, win), F32),
                            pltpu.SemaphoreType.DMA((2,)), pltpu.SemaphoreType.DMA((2,))]),
        out_shape=[jax.ShapeDtypeStruct((cap * ROW_TILE_ROWS, LANES), F32), cshape, cshape],
        compiler_params=_cparams(1, VMEM_LIMIT_EXPERTS),
        name="moe_experts",
    )(block_e, n_used, xs, w_gu, b_gu, w_down, b_down, cache_k, cache_v, tail_k, tail_v)


def _combine_kernel(dest_sm, yb_ref, h_ref, g_ref, op_ref, os_ref, buf, sems, *, t_all):
    i = pl.program_id(0)
    last = pl.num_programs(0) - 1
    tm = h_ref.shape[0]

    def gather(step, slot):
        def start_body(j, c):
            for k in range(TOP_K):
                d = dest_sm[k * t_all + step * tm + j]
                tile = pl.ds(pl.multiple_of(j * ROW_TILE_ROWS, ROW_TILE_ROWS), ROW_TILE_ROWS)
                pltpu.make_async_copy(yb_ref.at[d], buf.at[slot, k, tile],
                                      sems.at[slot]).start(priority=k % 2)
            return c

        lax.fori_loop(0, tm, start_body, 0, unroll=ROW_DMA_UNROLL)

    @pl.when(i == 0)
    def _():
        gather(0, 0)

    @pl.when(i < last)
    def _():
        gather(i + 1, (i + 1) % 2)

    slot = i % 2
    for k in range(TOP_K):
        pltpu.make_async_copy(buf.at[slot, k], buf.at[slot, k], sems.at[slot]).wait()

    g = g_ref[...]
    y = h_ref[...]
    for k in range(TOP_K):
        y = y + g[:, k:k + 1] * _load_row_tiles(buf.at[slot, k], tm)

    @pl.when(i < last)
    def _():
        op_ref[...] = y

    @pl.when(i == last)
    def _():
        os_ref[...] = y


def _combine(dest_flat, yb, h2_all, gates, n_prompt_rows):
    t_all = h2_all.shape[0]
    tm = ROW_TILE
    n_p_tiles = n_prompt_rows // tm
    assert t_all == n_prompt_rows + tm
    return pl.pallas_call(
        functools.partial(_combine_kernel, t_all=t_all),
        grid_spec=pltpu.PrefetchScalarGridSpec(
            num_scalar_prefetch=1,
            grid=(t_all // tm,),
            in_specs=[pl.BlockSpec(memory_space=pl.ANY),
                      pl.BlockSpec((tm, D_MODEL), lambda i, d: (i, 0)),
                      pl.BlockSpec((tm, TOP_K), lambda i, d: (i, 0))],
            out_specs=[pl.BlockSpec((tm, D_MODEL), lambda i, d: (jnp.minimum(i, n_p_tiles - 1), 0)),
                       pl.BlockSpec((tm, D_MODEL), lambda i, d: (0, 0))],
            scratch_shapes=[pltpu.VMEM((2, TOP_K, tm * ROW_TILE_ROWS, LANES), F32),
                            pltpu.SemaphoreType.DMA((2,))]),
        out_shape=[jax.ShapeDtypeStruct((n_prompt_rows, D_MODEL), F32),
                   jax.ShapeDtypeStruct((tm, D_MODEL), F32)],
        compiler_params=_cparams(1),
        name="moe_combine",
    )(dest_flat, yb, h2_all, gates)


def _rope_tables(pos):
    half = HEAD_DIM // 2
    inv = jnp.exp(-math.log(ROPE_THETA) * jnp.arange(half, dtype=F32) / half)
    ang = pos.astype(F32)[:, None] * inv[None, :]
    cos, sin = jnp.cos(ang), jnp.sin(ang)
    cos_t = jnp.concatenate([cos, cos] * HEADS_PER_TILE, axis=-1)
    sin_t = jnp.concatenate([-sin, sin] * HEADS_PER_TILE, axis=-1)
    return cos_t, sin_t


def _block_diag_ones(n, blk):
    r = jnp.arange(n) // blk
    return (r[:, None] == r[None, :]).astype(BF16)


def _layer(x_prompt, x_sample, state_ret, cache_win_k, cache_win_v, cache_mem_k, cache_mem_v,
           mem_prompt, g_norm1, w_in, g_att_q, g_att_k, g_ret_gn, w_out, g_norm2, g_mem,
           w_q_x, w_kv_x, g_x_q, g_x_k, w_o_x, g_norm3, w_router, b_router, w_gu, b_gu,
           w_down, b_down):
    n_b, seq, _ = x_prompt.shape
    n_s = x_sample.shape[0]
    assert x_sample.shape[1] == 1 and n_s == ROW_TILE
    t_p = n_b * seq
    t_all = t_p + n_s
    assert t_all % TOKEN_TILE == 0 and t_p % ROW_TILE == 0

    assert seq % PROMPT_TILE == 0
    x_p = x_prompt.reshape(t_p, D_MODEL)
    x_s = x_sample.reshape(n_s, D_MODEL)
    cos_p, sin_p = _rope_tables(jnp.arange(seq, dtype=jnp.int32))
    cos_s, sin_s = _rope_tables(jnp.full((n_s,), PAST_LEN, jnp.int32))
    log_g = jnp.log1p(-jnp.exp2(-5.0 - jnp.arange(RET_HEADS, dtype=F32)))
    lg_row = jnp.repeat(log_g, HEAD_DIM)[None, :]
    gn_row = g_ret_gn.reshape(1, RET_WIDTH)
    gn_col = g_ret_gn.reshape(RET_WIDTH, 1)
    gq = jnp.tile(g_att_q.reshape(1, HEAD_DIM), (1, ATT_HEADS))
    gk = jnp.tile(g_att_k.reshape(1, HEAD_DIM), (1, ATT_HEADS))
    seg = _block_diag_ones(ATT_WIDTH, HEAD_DIM)

    g1 = g_norm1.reshape(1, D_MODEL)
    w_in_bf = w_in.astype(BF16)
    proj = _in_proj(x_p, g1, w_in_bf, cos_p, sin_p, gq, gk, seg, t_all, PROMPT_TILE, 0)
    rq, rk, rv, rg, aq, ak, av = _in_proj(x_s, g1, w_in_bf, cos_s, sin_s, gq, gk, seg, t_all, n_s,
                                          t_p, prev=proj)

    ret_n, st_p = _ret_prompt(rq, rk, rv, rg, gn_row, lg_row, n_b, seq)
    att_o, wk_p, wv_p = _dil_prompt(aq, ak, av, n_b, seq)
    state_t = jnp.transpose(state_ret, (1, 2, 3, 0))
    ret_n, st_s = _ret_sample(rq, rk, rv, ret_n, gn_col, lg_row, state_t)
    ck = jnp.transpose(cache_win_k, (0, 2, 3, 1))
    cv = jnp.transpose(cache_win_v, (0, 2, 3, 1))
    tail_k, tail_v, att_o = _win_sample(att_o, ak, av, ck, cv)

    mix_w = (w_out.astype(BF16), g_norm2.reshape(1, D_MODEL), w_q_x.astype(BF16),
             g_x_q.reshape(1, X_HEAD_DIM))
    mixed = _mix_out(ret_n, att_o, x_p, *mix_w, PROMPT_TILE, 0)
    h_all, qx = _mix_out(ret_n, att_o, x_s, *mix_w, n_s, t_p, prev=mixed)

    mk_p, mv_p = _mem_kv(mem_prompt, g_mem.reshape(1, D_MODEL), w_kv_x.astype(BF16),
                         g_x_k.reshape(1, X_HEAD_DIM))
    o_all = _xattn_prompt(qx, mk_p, mv_p, n_b, seq)
    n_mem = cache_mem_k.shape[1]
    o_all = _xattn_sample(o_all, cache_mem_k.reshape(n_s, n_mem * X_HEADS, X_HEAD_DIM),
                          cache_mem_v.reshape(n_s, n_mem * X_HEADS, X_HEAD_DIM))

    h2_all, xn_all, idx_t, gate_t = _xout_router(
        o_all, h_all, w_o_x.astype(BF16), g_norm3.reshape(1, D_MODEL),
        jnp.transpose(w_router), b_router.reshape(N_EXPERTS, 1))

    bm = MOE_BLOCK_ROWS
    n_blocks = -(-(t_all * TOP_K) // bm) + N_EXPERTS
    n_blocks_pad = -(-n_blocks // LANES) * LANES
    dest_t, be, nu = _route(idx_t, n_blocks_pad)
    dest_flat = dest_t.reshape(TOP_K * t_all)
    cap = n_blocks * bm
    xs = _dispatch(dest_flat, xn_all.reshape(t_all, ROW_TILE_ROWS, LANES),
                   jnp.zeros((cap, ROW_TILE_ROWS, LANES), F32))
    yb, wk_s, wv_s = _experts(
        be.reshape(n_blocks_pad), nu.reshape(LANES), xs.reshape(cap * ROW_TILE_ROWS, LANES),
        w_gu, b_gu.reshape(N_EXPERTS, 1, 2 * D_FF), w_down, b_down.reshape(N_EXPERTS, 1, D_MODEL),
        ck, cv, tail_k, tail_v)
    y_p, y_s = _combine(dest_flat, yb.reshape(cap, ROW_TILE_ROWS, LANES), h2_all,
                        jnp.transpose(gate_t), t_p)

    st_p = jnp.stack([st_p[:, :, :HEAD_DIM, :HEAD_DIM], st_p[:, :, HEAD_DIM:, HEAD_DIM:]], axis=2)
    st_p = st_p.reshape(n_b, RET_HEADS, HEAD_DIM, HEAD_DIM)
    return (y_p.reshape(n_b, seq, D_MODEL),
            y_s.reshape(n_s, 1, D_MODEL),
            st_p,
            jnp.transpose(st_s, (3, 0, 1, 2)),
            jnp.transpose(wk_p, (0, 3, 1, 2)),
            jnp.transpose(wv_p, (0, 3, 1, 2)),
            jnp.transpose(wk_s, (0, 3, 1, 2)),
            jnp.transpose(wv_s, (0, 3, 1, 2)),
            mk_p.reshape(n_b, n_mem, X_HEADS, X_HEAD_DIM),
            mv_p.reshape(n_b, n_mem, X_HEADS, X_HEAD_DIM))


def kernel(x_prompt, x_sample, state_ret, cache_win_k, cache_win_v, cache_mem_k, cache_mem_v,
           mem_prompt, g_norm1, w_in, g_att_q, g_att_k, g_ret_gn, w_out, g_norm2, g_mem,
           w_q_x, w_kv_x, g_x_q, g_x_k, w_o_x, g_norm3, w_router, b_router, w_gu, b_gu,
           w_down, b_down):
    assert state_ret.shape[0] == 1, "single-layer trunk"
    outs = _layer(x_prompt, x_sample, state_ret[0], cache_win_k[0], cache_win_v[0],
                  cache_mem_k[0], cache_mem_v[0], mem_prompt, g_norm1[0], w_in[0], g_att_q[0],
                  g_att_k[0], g_ret_gn[0], w_out[0], g_norm2[0], g_mem[0], w_q_x[0], w_kv_x[0],
                  g_x_q[0], g_x_k[0], w_o_x[0], g_norm3[0], w_router[0], b_router[0], w_gu[0],
                  b_gu[0], w_down[0], b_down[0])
    y_p, y_s = outs[0], outs[1]
    return (y_p, y_s) + tuple(o[None] for o in outs[2:])
```

```python
import functools
import math

import jax
import jax.numpy as jnp
from jax import lax
from jax.experimental import pallas as pl
from jax.experimental.pallas import tpu as pltpu

F32 = jnp.float32
BF16 = jnp.bfloat16
I32 = jnp.int32

D_MODEL = 1024
HEAD_DIM = 64
RET_HEADS = 8
ATT_HEADS = 8
RET_WIDTH = RET_HEADS * HEAD_DIM
ATT_WIDTH = ATT_HEADS * HEAD_DIM
IN_COLS = 4 * RET_WIDTH + 3 * ATT_WIDTH
RET_CHUNK = 128
DIL_PATTERNS = ((128, 1), (512, 4), (2048, 16))
DIL_BLOCK = 128
MAX_WINDOW = 2048
X_HEADS = 4
X_HEAD_DIM = 128
X_WIDTH = X_HEADS * X_HEAD_DIM
N_EXPERTS = 32
TOP_K = 4
D_FF = D_MODEL
SWIGLU_LIMIT = 7.0
SWIGLU_ALPHA = 1.702
ROPE_THETA = 10000.0
EPS = 1e-6
PAST_LEN = 8192

LANES = 128
SUBLANES = 8
HEADS_PER_TILE = LANES // HEAD_DIM

TOKEN_TILE = 384
PROMPT_TILE = 512
ROW_TILE = 128
MOE_BLOCK_ROWS = 512
NEG_BIG = -1e30
VMEM_LIMIT = 48 * 1024 * 1024
VMEM_LIMIT_EXPERTS = 58 * 1024 * 1024


def _cparams(n_axes, vmem_limit=VMEM_LIMIT):
    return pltpu.CompilerParams(
        dimension_semantics=("arbitrary",) * n_axes, vmem_limit_bytes=vmem_limit)


def _rms(x, g):
    return x * lax.rsqrt(jnp.mean(x * x, axis=-1, keepdims=True) + EPS) * g


def _dot(a, b):
    return jnp.dot(a, b, preferred_element_type=F32)


def _dot_nt(a, b):
    return lax.dot_general(a, b, (((1,), (1,)), ((), ())), preferred_element_type=F32)


def _dot_tn(a, b):
    return lax.dot_general(a, b, (((0,), (0,)), ((), ())), preferred_element_type=F32)


def _sigmoid(x):
    return 1.0 / (1.0 + jnp.exp(-x))


N_PROJ_OUT = 7


def _in_proj_kernel(x_ref, g1_ref, w_ref, gq_ref, gk_ref, seg_ref, cos_ref, sin_ref, *rest):
    rq_ref, rk_ref, rv_ref, rg_ref, aq_ref, ak_ref, av_ref = rest[-N_PROJ_OUT:]
    xn = _rms(x_ref[...], g1_ref[...]).astype(BF16)
    proj = _dot(xn, w_ref[...])
    reps = RET_WIDTH // LANES
    cos = jnp.concatenate([cos_ref[...]] * reps, axis=-1)
    sin = jnp.concatenate([sin_ref[...]] * reps, axis=-1)
    lane = lax.broadcasted_iota(I32, (1, RET_WIDTH), 1)
    first_half = (lane % HEAD_DIM) < (HEAD_DIM // 2)
    seg = seg_ref[...]

    def rope(t):
        partner = jnp.where(first_half,
                            pltpu.roll(t, RET_WIDTH - HEAD_DIM // 2, 1),
                            pltpu.roll(t, HEAD_DIM // 2, 1))
        return t * cos + partner * sin

    def head_norm(t, g):
        ssum = _dot((t * t).astype(BF16), seg)
        return t * lax.rsqrt(ssum * (1.0 / HEAD_DIM) + EPS) * g

    w = RET_WIDTH
    rq_ref[...] = rope(proj[:, 0:w])
    rk_ref[...] = rope(proj[:, w:2 * w]) * (HEAD_DIM ** -0.5)
    rv_ref[...] = proj[:, 2 * w:3 * w]
    rg_ref[...] = proj[:, 3 * w:4 * w]
    aq_ref[...] = rope(head_norm(proj[:, 4 * w:5 * w], gq_ref[...]))
    ak_ref[...] = rope(head_norm(proj[:, 5 * w:6 * w], gk_ref[...]))
    av_ref[...] = proj[:, 6 * w:7 * w]


def _group_rows_call(kernel, name, shared_ins, row_ins, fixed_ins, tables, out_widths, t_all, tm,
                     row_start, prev):
    rows = row_ins[0].shape[0]
    off = row_start // tm
    assert rows % tm == 0 and row_start % tm == 0
    n_steps = rows // tm
    extra = 1 if (prev is None and row_start + rows < t_all) else 0
    src = lambda i: jnp.minimum(i, n_steps - 1)
    fixed = lambda i: (0, 0)
    in_specs = [pl.BlockSpec((tm, a.shape[1]), lambda i: (src(i) + off, 0)) for a in shared_ins]
    in_specs += [pl.BlockSpec((tm, a.shape[1]), lambda i: (src(i), 0)) for a in row_ins]
    in_specs += [pl.BlockSpec(a.shape, fixed) for a in fixed_ins]
    for t in tables:
        period = t.shape[0] // tm
        in_specs.append(pl.BlockSpec((tm, t.shape[1]),
                                     lambda i, period=period: (src(i) % period, 0)))
    n_in = len(in_specs)
    aliases = {}
    if prev is not None:
        in_specs += [pl.BlockSpec(memory_space=pl.ANY)] * len(prev)
        aliases = {n_in + j: j for j in range(len(prev))}
    return pl.pallas_call(
        kernel,
        grid=(n_steps + extra,),
        in_specs=in_specs,
        out_specs=[pl.BlockSpec((tm, w), lambda i: (i + off, 0)) for w in out_widths],
        out_shape=[jax.ShapeDtypeStruct((t_all, w), F32) for w in out_widths],
        input_output_aliases=aliases,
        compiler_params=_cparams(1),
        name=name,
    )(*shared_ins, *row_ins, *fixed_ins, *tables, *(prev or ()))


def _in_proj(x, g1, w_in, cos_t, sin_t, gq, gk, seg, t_all, tm, row_start, prev=None):
    return _group_rows_call(_in_proj_kernel, "in_proj", [], [x], [g1, w_in, gq, gk, seg],
                            [cos_t, sin_t], [RET_WIDTH] * N_PROJ_OUT, t_all, tm, row_start, prev)


def _group_norm_gate(o, mask_a, gn, gate):
    inv = 1.0 / HEAD_DIM
    sa = jnp.sum(jnp.where(mask_a, o, 0.0), axis=-1, keepdims=True)
    sb = jnp.sum(jnp.where(mask_a, 0.0, o), axis=-1, keepdims=True)
    cen = o - jnp.where(mask_a, sa, sb) * inv
    c2 = cen * cen
    va = jnp.sum(jnp.where(mask_a, c2, 0.0), axis=-1, keepdims=True)
    vb = jnp.sum(jnp.where(mask_a, 0.0, c2), axis=-1, keepdims=True)
    var = jnp.where(mask_a, va, vb) * inv
    return cen * lax.rsqrt(var + EPS) * gn * (gate * _sigmoid(gate))


def _ret_prompt_kernel(q_ref, k_ref, v_ref, g_ref, gn_ref, lg_ref, o_ref, st_ref,
                       state, dec, qdec, kdec):
    n = pl.program_id(0)
    c = pl.program_id(1)
    ch = RET_CHUNK
    n_pairs = RET_WIDTH // LANES
    lane = lax.broadcasted_iota(I32, (1, LANES), 1)
    mask_a = lane < HEAD_DIM
    row_i = lax.broadcasted_iota(I32, (ch, 1), 0)

    @pl.when(jnp.logical_and(n == 0, c == 0))
    def _():
        row = row_i.astype(F32)
        col = lax.broadcasted_iota(I32, (1, ch), 1).astype(F32)
        diff = row - col
        causal = diff >= 0.0
        dpos = jnp.maximum(diff, 0.0)
        lg = lg_ref[...]
        for h in range(RET_HEADS):
            dec[h] = jnp.where(causal, jnp.exp(dpos * lg[:, h * HEAD_DIM:h * HEAD_DIM + 1]), 0.0)
        qdec[...] = jnp.exp((row + 1.0) * lg)
        kdec[...] = jnp.exp((ch - 1.0 - row) * lg)

    @pl.when(c == 0)
    def _():
        state[...] = jnp.zeros_like(state)

    same_head = (row_i // HEAD_DIM) == (lane // HEAD_DIM)
    cdec = jnp.exp(ch * lg_ref[...])
    for p in range(n_pairs):
        sl = slice(p * LANES, (p + 1) * LANES)
        q = q_ref[:, sl]
        k = k_ref[:, sl]
        kb = k.astype(BF16)
        vb = v_ref[:, sl].astype(BF16)
        lhs = jnp.concatenate([jnp.where(mask_a, q, 0.0), jnp.where(mask_a, 0.0, q)],
                              axis=0).astype(BF16)
        s = _dot_nt(lhs, kb)
        pr = (s * jnp.concatenate([dec[2 * p], dec[2 * p + 1]], axis=0)).astype(BF16)
        o2 = _dot(pr, vb)
        o_intra = jnp.where(mask_a, o2[:ch], o2[ch:])
        st = state[p]
        o = o_intra + _dot(q.astype(BF16), st.astype(BF16)) * qdec[:, sl]
        upd = _dot_tn((k * kdec[:, sl]).astype(BF16), vb)
        state[p] = cdec[:, sl] * st + jnp.where(same_head, upd, 0.0)
        o_ref[:, sl] = _group_norm_gate(o, mask_a, gn_ref[:, sl], g_ref[:, sl])

    @pl.when(c == pl.num_programs(1) - 1)
    def _():
        st_ref[0] = state[...]


def _ret_prompt(rq, rk, rv, rg, gn_row, lg_row, n_batch, seq):
    t_all = rq.shape[0]
    n_pairs = RET_WIDTH // LANES
    n_chunks = seq // RET_CHUNK
    blk = pl.BlockSpec((RET_CHUNK, RET_WIDTH), lambda n, c: (n * n_chunks + c, 0))
    lane_row = pl.BlockSpec((1, RET_WIDTH), lambda n, c: (0, 0))
    return pl.pallas_call(
        _ret_prompt_kernel,
        grid=(n_batch, n_chunks),
        in_specs=[blk, blk, blk, blk, lane_row, lane_row],
        out_specs=[blk, pl.BlockSpec((1, n_pairs, LANES, LANES), lambda n, c: (n, 0, 0, 0))],
        out_shape=[jax.ShapeDtypeStruct((t_all, RET_WIDTH), F32),
                   jax.ShapeDtypeStruct((n_batch, n_pairs, LANES, LANES), F32)],
        scratch_shapes=[pltpu.VMEM((n_pairs, LANES, LANES), F32),
                        pltpu.VMEM((RET_HEADS, RET_CHUNK, RET_CHUNK), F32),
                        pltpu.VMEM((RET_CHUNK, RET_WIDTH), F32),
                        pltpu.VMEM((RET_CHUNK, RET_WIDTH), F32)],
        input_output_aliases={3: 0},
        compiler_params=_cparams(2),
        name="ret_prompt",
    )(rq, rk, rv, rg, gn_row, lg_row)


DIL_UNROLL = 4
DIL_MERGE_ROWS = 256


def _dil_prompt_kernel(q_ref, k_ref, v_ref, o_ref, wk_ref, wv_ref, acc, m_s, l_s, *, seq, wp):
    b = DIL_BLOCK
    lane = lax.broadcasted_iota(I32, (1, LANES), 1)
    mask_a = lane < HEAD_DIM
    qi = lax.broadcasted_iota(I32, (2 * b, 1), 0) % b
    kk = lax.broadcasted_iota(I32, (1, 2 * b), 1)
    dist = qi + b - kk
    scale = HEAD_DIM ** -0.5

    for pat, (window, dil) in enumerate(DIL_PATTERNS):
        steps = window // dil
        band = (dist >= 0) & (dist <= steps)
        nb = seq // (b * dil)

        def block(idx, dil=dil, nb=nb, band=band):
            r = idx // nb
            i = idx % nb
            rows_q = pl.ds(r + i * (b * dil), b, stride=dil)
            rows_p = pl.ds(r + jnp.maximum(i - 1, 0) * (b * dil), b, stride=dil)
            q = q_ref[rows_q, :]
            kc = jnp.concatenate([k_ref[rows_p, :], k_ref[rows_q, :]], axis=0).astype(BF16)
            vc = jnp.concatenate([v_ref[rows_p, :], v_ref[rows_q, :]], axis=0).astype(BF16)
            lhs = jnp.concatenate([jnp.where(mask_a, q, 0.0), jnp.where(mask_a, 0.0, q)],
                                  axis=0).astype(BF16)
            s = _dot_nt(lhs, kc) * scale
            valid = band & ((kk >= b) | (jnp.broadcast_to(i, kk.shape) > 0))
            s = jnp.where(valid, s, NEG_BIG)
            m_blk = jnp.max(s, axis=-1, keepdims=True)
            p = jnp.exp(s - m_blk)
            l_blk = jnp.sum(p, axis=-1, keepdims=True)
            pv = _dot(p.astype(BF16), vc)
            return rows_q, m_blk, l_blk, pv

        def emit(rows_q, m_blk, l_blk, pv, pat=pat):
            acc[pat, rows_q, :] = jnp.where(mask_a, pv[:b], pv[b:])
            m_s[pat, rows_q, :] = jnp.where(mask_a, m_blk[:b], m_blk[b:])
            l_s[pat, rows_q, :] = jnp.where(mask_a, l_blk[:b], l_blk[b:])

        def body(it, carry, block=block, emit=emit):
            parts = [block(it * DIL_UNROLL + u) for u in range(DIL_UNROLL)]
            for part in parts:
                emit(*part)
            return carry

        lax.fori_loop(0, (dil * nb) // DIL_UNROLL, body, 0)

    n_pat = len(DIL_PATTERNS)

    def merge_body(c, carry):
        rows = pl.ds(pl.multiple_of(c * DIL_MERGE_ROWS, DIL_MERGE_ROWS), DIL_MERGE_ROWS)
        ms = [m_s[pat, rows, :] for pat in range(n_pat)]
        m = functools.reduce(jnp.maximum, ms)
        num = jnp.zeros((DIL_MERGE_ROWS, LANES), F32)
        den = jnp.zeros((DIL_MERGE_ROWS, LANES), F32)
        for pat in range(n_pat):
            w = jnp.exp(ms[pat] - m)
            num = num + w * acc[pat, rows, :]
            den = den + w * l_s[pat, rows, :]
        o_ref[rows, :] = num / den
        return carry

    lax.fori_loop(0, seq // DIL_MERGE_ROWS, merge_body, 0)

    tchunk = 4 * LANES
    for j in range(wp // tchunk):
        rows = pl.ds(seq - wp + j * tchunk, tchunk)
        cols = slice(j * tchunk, (j + 1) * tchunk)
        kt = k_ref[rows, :].T
        vt = v_ref[rows, :].T
        for hh in range(HEADS_PER_TILE):
            wk_ref[0, hh, :, cols] = kt[hh * HEAD_DIM:(hh + 1) * HEAD_DIM]
            wv_ref[0, hh, :, cols] = vt[hh * HEAD_DIM:(hh + 1) * HEAD_DIM]


def _dil_prompt(aq, ak, av, n_batch, seq):
    t_all = aq.shape[0]
    n_pairs = ATT_WIDTH // LANES
    wp = min(MAX_WINDOW, seq)
    blk = pl.BlockSpec((seq, LANES), lambda n, p: (n, p))
    wblk = pl.BlockSpec((1, HEADS_PER_TILE, HEAD_DIM, wp), lambda n, p: (n, p, 0, 0))
    wshape = jax.ShapeDtypeStruct((n_batch, ATT_HEADS, HEAD_DIM, wp), F32)
    return pl.pallas_call(
        functools.partial(_dil_prompt_kernel, seq=seq, wp=wp),
        grid=(n_batch, n_pairs),
        in_specs=[blk, blk, blk],
        out_specs=[blk, wblk, wblk],
        out_shape=[jax.ShapeDtypeStruct((t_all, ATT_WIDTH), F32), wshape, wshape],
        scratch_shapes=[pltpu.VMEM((len(DIL_PATTERNS), seq, LANES), F32)] * 3,
        input_output_aliases={0: 0},
        compiler_params=_cparams(2),
        name="dil_prompt",
    )(aq, ak, av)


def _ret_sample_kernel(q_ref, k_ref, v_ref, g_ref, gn_ref, lg_ref, st_ref,
                       o_ref, nst_ref, qt, kt):
    qt[...] = q_ref[...].T
    kt[...] = k_ref[...].T
    vt = v_ref[...].T
    gt = g_ref[...].T
    lg = lg_ref[...]
    outs = []
    for hh in range(HEADS_PER_TILE):
        lo = hh * HEAD_DIM
        gdec = jnp.exp(lg[:, lo:lo + 1])
        vth = vt[lo:lo + HEAD_DIM, :]

        def body(d, o, hh=hh, lo=lo, gdec=gdec, vth=vth):
            new = gdec * st_ref[hh, d] + kt[pl.ds(lo + d, 1), :] * vth
            nst_ref[hh, d] = new
            return o + qt[pl.ds(lo + d, 1), :] * new

        o = lax.fori_loop(0, HEAD_DIM, body, jnp.zeros_like(vth), unroll=8)
        mu = jnp.mean(o, axis=0, keepdims=True)
        cen = o - mu
        var = jnp.mean(cen * cen, axis=0, keepdims=True)
        gate = gt[lo:lo + HEAD_DIM, :]
        outs.append(cen * lax.rsqrt(var + EPS) * gn_ref[lo:lo + HEAD_DIM, :] * (gate * _sigmoid(gate)))
    o_ref[...] = jnp.concatenate(outs, axis=0).T


def _ret_sample(rq, rk, rv, ret_n, gn_col, lg_row, state_t):
    t_all = rq.shape[0]
    n_s = state_t.shape[-1]
    n_pairs = RET_WIDTH // LANES
    last = t_all // n_s - 1
    blk = pl.BlockSpec((n_s, LANES), lambda p: (last, p))
    st_blk = pl.BlockSpec((HEADS_PER_TILE, HEAD_DIM, HEAD_DIM, n_s), lambda p: (p, 0, 0, 0))
    return pl.pallas_call(
        _ret_sample_kernel,
        grid=(n_pairs,),
        in_specs=[blk, blk, blk, blk,
                  pl.BlockSpec((LANES, 1), lambda p: (p, 0)),
                  pl.BlockSpec((1, LANES), lambda p: (0, p)),
                  st_blk],
        out_specs=[blk, st_blk],
        out_shape=[jax.ShapeDtypeStruct(ret_n.shape, F32),
                   jax.ShapeDtypeStruct(state_t.shape, F32)],
        scratch_shapes=[pltpu.VMEM((LANES, n_s), F32), pltpu.VMEM((LANES, n_s), F32)],
        input_output_aliases={3: 0},
        compiler_params=_cparams(1),
        name="ret_sample",
    )(rq, rk, rv, ret_n, gn_col, lg_row, state_t)


def _win_sample_kernel(aq_ref, akn_ref, avn_ref, kc_ref, vc_ref,
                       ko_ref, vo_ref, att_ref, qt, kt, vt, acct, *, win):
    n = pl.program_id(0)
    n_s = qt.shape[1]

    @pl.when(n == 0)
    def _():
        qt[...] = aq_ref[...].T
        kt[...] = akn_ref[...].T
        vt[...] = avn_ref[...].T
        acct[...] = jnp.zeros_like(acct)

    onehot = (lax.broadcasted_iota(I32, (n_s, LANES), 0) == n).astype(F32)
    hp = lax.Precision.HIGHEST
    qb = jnp.dot(qt[...], onehot, precision=hp, preferred_element_type=F32)
    kb = jnp.dot(kt[...], onehot, precision=hp, preferred_element_type=F32)
    vb = jnp.dot(vt[...], onehot, precision=hp, preferred_element_type=F32)

    w_pos = lax.broadcasted_iota(I32, (1, win), 1)
    back = win - w_pos
    mult = jnp.zeros((1, win), F32)
    for window, dil in DIL_PATTERNS:
        mult = mult + ((back <= window) & (back % dil == 0)).astype(F32)
    valid = mult > 0.0
    is_last = w_pos == win - 1
    reps = win // LANES
    scale = HEAD_DIM ** -0.5

    s_rows, s0_rows = [], []
    for h in range(ATT_HEADS):
        lo = h * HEAD_DIM
        k_t = kc_ref[0, h]
        qh = qb[lo:lo + HEAD_DIM, :]
        kh = kb[lo:lo + HEAD_DIM, :]
        s_rows.append(jnp.sum(k_t * jnp.concatenate([qh] * reps, axis=1), axis=0, keepdims=True))
        s0_rows.append(jnp.sum(qh * kh, axis=0, keepdims=True)[:, 0:1])
        ko_ref[0, h] = jnp.where(is_last, jnp.concatenate([kh] * reps, axis=1),
                                 pltpu.roll(k_t, win - 1, 1))
    s = jnp.concatenate(s_rows, axis=0) * scale
    s0 = jnp.concatenate(s0_rows, axis=0) * scale
    m = jnp.maximum(jnp.max(jnp.where(valid, s, NEG_BIG), axis=-1, keepdims=True), s0)
    e = jnp.where(valid, jnp.exp(s - m), 0.0) * mult
    e0 = len(DIL_PATTERNS) * jnp.exp(s0 - m)
    denom = jnp.sum(e, axis=-1, keepdims=True) + e0

    cols = []
    for h in range(ATT_HEADS):
        lo = h * HEAD_DIM
        v_t = vc_ref[0, h]
        vh = vb[lo:lo + HEAD_DIM, :]
        num = jnp.sum(v_t * e[h:h + 1, :], axis=1, keepdims=True) + e0[h:h + 1, :] * vh[:, 0:1]
        cols.append(num / denom[h:h + 1, :])
        vo_ref[0, h] = jnp.where(is_last, jnp.concatenate([vh] * reps, axis=1),
                                 pltpu.roll(v_t, win - 1, 1))
    o_col = jnp.concatenate(cols, axis=0)
    lane_n = lax.broadcasted_iota(I32, (1, n_s), 1) == n
    acct[...] = jnp.where(lane_n, o_col, acct[...])

    @pl.when(n == pl.num_programs(0) - 1)
    def _():
        att_ref[...] = acct[...].T


def _win_sample(att_o, ak, av, cache_k, cache_v):
    n_s, n_h, hd, win = cache_k.shape
    assert win >= max(w for w, _ in DIL_PATTERNS)
    t_all = att_o.shape[0]
    last = t_all // n_s - 1
    rows = pl.BlockSpec((n_s, ATT_WIDTH), lambda n: (last, 0))
    cblk = pl.BlockSpec((1, n_h, hd, win), lambda n: (n, 0, 0, 0))
    cshape = jax.ShapeDtypeStruct(cache_k.shape, F32)
    return pl.pallas_call(
        functools.partial(_win_sample_kernel, win=win),
        grid=(n_s,),
        in_specs=[rows, rows, rows, cblk, cblk],
        out_specs=[cblk, cblk, rows],
        out_shape=[cshape, cshape, jax.ShapeDtypeStruct(att_o.shape, F32)],
        scratch_shapes=[pltpu.VMEM((ATT_WIDTH, n_s), F32)] * 4,
        input_output_aliases={0: 2},
        compiler_params=_cparams(1),
        name="win_sample",
    )(att_o, ak, av, cache_k, cache_v)


def _lane_tile_norm(t, g, n_tiles):
    outs = []
    for h in range(n_tiles):
        outs.append(_rms(t[:, h * LANES:(h + 1) * LANES], g))
    return jnp.concatenate(outs, axis=-1)


def _mem_kv_kernel(mem_ref, gm_ref, w_ref, gk_ref, k_ref, v_ref):
    xn = _rms(mem_ref[0], gm_ref[...]).astype(BF16)
    kv = _dot(xn, w_ref[...])
    k_ref[0] = _lane_tile_norm(kv[:, :X_WIDTH], gk_ref[...], X_HEADS)
    v_ref[0] = kv[:, X_WIDTH:]


def _mem_kv(mem, g_mem, w_kv, g_xk):
    n, m, _ = mem.shape
    out = jax.ShapeDtypeStruct((n, m, X_WIDTH), F32)
    return pl.pallas_call(
        _mem_kv_kernel,
        grid=(n,),
        in_specs=[pl.BlockSpec((1, m, D_MODEL), lambda i: (i, 0, 0)),
                  pl.BlockSpec((1, D_MODEL), lambda i: (0, 0)),
                  pl.BlockSpec((D_MODEL, 2 * X_WIDTH), lambda i: (0, 0)),
                  pl.BlockSpec((1, X_HEAD_DIM), lambda i: (0, 0))],
        out_specs=[pl.BlockSpec((1, m, X_WIDTH), lambda i: (i, 0, 0))] * 2,
        out_shape=[out, out],
        compiler_params=_cparams(1),
        name="mem_kv",
    )(mem, g_mem, w_kv, g_xk)


def _mix_out_kernel(ret_ref, att_ref, x_ref, wo_ref, g2_ref, wq_ref, gq_ref, *rest):
    h_ref, q_ref = rest[-2:]
    mixed = jnp.concatenate([ret_ref[...], att_ref[...]], axis=-1).astype(BF16)
    h = x_ref[...] + _dot(mixed, wo_ref[...])
    h_ref[...] = h
    q = _dot(_rms(h, g2_ref[...]).astype(BF16), wq_ref[...])
    q_ref[...] = _lane_tile_norm(q, gq_ref[...], X_HEADS)


def _mix_out(ret_n, att_o, x, w_out, g2, w_qx, g_xq, tm, row_start, prev=None):
    return _group_rows_call(_mix_out_kernel, "mix_out", [ret_n, att_o], [x],
                            [w_out, g2, w_qx, g_xq], [], [D_MODEL, X_WIDTH], ret_n.shape[0], tm,
                            row_start, prev)


def _xattn_prompt_kernel(q_ref, k_ref, v_ref, o_ref):
    q = q_ref[...]
    k = k_ref[0]
    v = v_ref[0]
    scale = X_HEAD_DIM ** -0.5
    outs = []
    for h in range(X_HEADS):
        sl = slice(h * LANES, (h + 1) * LANES)
        s = _dot_nt(q[:, sl].astype(BF16), k[:, sl].astype(BF16)) * scale
        p = jnp.exp(s - jnp.max(s, axis=-1, keepdims=True))
        o = _dot(p.astype(BF16), v[:, sl].astype(BF16))
        outs.append(o / jnp.sum(p, axis=-1, keepdims=True))
    o_ref[...] = jnp.concatenate(outs, axis=-1)


def _xattn_prompt(qx, mk, mv, n_batch, seq, tq=512):
    t_all = qx.shape[0]
    per = seq // tq
    m = mk.shape[1]
    rows = pl.BlockSpec((tq, X_WIDTH), lambda n, i: (n * per + i, 0))
    mem = pl.BlockSpec((1, m, X_WIDTH), lambda n, i: (n, 0, 0))
    return pl.pallas_call(
        _xattn_prompt_kernel,
        grid=(n_batch, per),
        in_specs=[rows, mem, mem],
        out_specs=rows,
        out_shape=jax.ShapeDtypeStruct((t_all, X_WIDTH), F32),
        input_output_aliases={0: 0},
        compiler_params=_cparams(2),
        name="xattn_prompt",
    )(qx, mk, mv)


def _xattn_sample_kernel(q_ref, k_ref, v_ref, o_ref, *, group, n_mem):
    i = pl.program_id(0)
    scale = X_HEAD_DIM ** -0.5
    for j in range(group):
        n = i * group + j
        q = q_ref[pl.ds(n, 1), :]
        outs = []
        for h in range(X_HEADS):
            rows = pl.ds(h, n_mem, stride=X_HEADS)
            qh = q[:, h * LANES:(h + 1) * LANES]
            s = jnp.sum(k_ref[j, rows, :] * qh, axis=-1, keepdims=True) * scale
            p = jnp.exp(s - jnp.max(s, axis=0, keepdims=True))
            o = jnp.sum(p * v_ref[j, rows, :], axis=0, keepdims=True)
            outs.append(o / jnp.sum(p, axis=0, keepdims=True))
        o_ref[pl.ds(n, 1), :] = jnp.concatenate(outs, axis=-1)


XATTN_SAMPLE_GROUP = 8


def _xattn_sample(o_all, mk, mv):
    n_s, mh, _ = mk.shape
    t_all = o_all.shape[0]
    last = t_all // n_s - 1
    group = XATTN_SAMPLE_GROUP
    rows = pl.BlockSpec((n_s, X_WIDTH), lambda n: (last, 0))
    mem = pl.BlockSpec((group, mh, X_HEAD_DIM), lambda n: (n, 0, 0))
    return pl.pallas_call(
        functools.partial(_xattn_sample_kernel, group=group, n_mem=mh // X_HEADS),
        grid=(n_s // group,),
        in_specs=[rows, mem, mem],
        out_specs=rows,
        out_shape=jax.ShapeDtypeStruct(o_all.shape, F32),
        input_output_aliases={0: 0},
        compiler_params=_cparams(1),
        name="xattn_sample",
    )(o_all, mk, mv)


ROW_TILE_ROWS = D_MODEL // LANES


def _store_row_tiles(ref, x):
    rows = x.shape[0]
    for c in range(ROW_TILE_ROWS):
        ref[pl.ds(c, rows, stride=ROW_TILE_ROWS), :] = x[:, c * LANES:(c + 1) * LANES]


def _load_row_tiles(ref, rows):
    return jnp.concatenate([ref[pl.ds(c, rows, stride=ROW_TILE_ROWS), :]
                            for c in range(ROW_TILE_ROWS)], axis=1)


def _xout_router_kernel(o_ref, h_ref, wo_ref, g3_ref, wr_ref, br_ref,
                        h2_ref, xn_ref, idx_ref, gate_ref):
    h2 = h_ref[...] + _dot(o_ref[...].astype(BF16), wo_ref[...])
    h2_ref[...] = h2
    xn = _rms(h2, g3_ref[...])
    _store_row_tiles(xn_ref, xn)
    logits = lax.dot_general(wr_ref[...], xn, (((1,), (1,)), ((), ())),
                             precision=lax.Precision.HIGHEST,
                             preferred_element_type=F32) + br_ref[...]
    eid = lax.broadcasted_iota(I32, logits.shape, 0)
    work = logits
    vals, idxs = [], []
    for _ in range(TOP_K):
        mx = jnp.max(work, axis=0, keepdims=True)
        ix = jnp.min(jnp.where(work == mx, eid, N_EXPERTS), axis=0, keepdims=True)
        vals.append(mx)
        idxs.append(ix)
        work = jnp.where(eid == ix, -jnp.inf, work)
    ex = [jnp.exp(v - vals[0]) for v in vals]
    tot = ex[0] + ex[1] + ex[2] + ex[3]
    idx_ref[...] = jnp.concatenate(idxs, axis=0)
    gate_ref[...] = jnp.concatenate([e / tot for e in ex], axis=0)


def _xout_router(o_all, h_all, w_ox, g3, w_rt, b_r):
    t_all = h_all.shape[0]
    tm = TOKEN_TILE
    row = lambda i: (i, 0)
    fixed = lambda i: (0, 0)
    colblk = lambda i: (0, i)
    return pl.pallas_call(
        _xout_router_kernel,
        grid=(t_all // tm,),
        in_specs=[pl.BlockSpec((tm, X_WIDTH), row),
                  pl.BlockSpec((tm, D_MODEL), row),
                  pl.BlockSpec((X_WIDTH, D_MODEL), fixed),
                  pl.BlockSpec((1, D_MODEL), fixed),
                  pl.BlockSpec((N_EXPERTS, D_MODEL), fixed),
                  pl.BlockSpec((N_EXPERTS, 1), fixed)],
        out_specs=[pl.BlockSpec((tm, D_MODEL), row),
                   pl.BlockSpec((tm * ROW_TILE_ROWS, LANES), row),
                   pl.BlockSpec((TOP_K, tm), colblk), pl.BlockSpec((TOP_K, tm), colblk)],
        out_shape=[jax.ShapeDtypeStruct((t_all, D_MODEL), F32),
                   jax.ShapeDtypeStruct((t_all * ROW_TILE_ROWS, LANES), F32),
                   jax.ShapeDtypeStruct((TOP_K, t_all), I32),
                   jax.ShapeDtypeStruct((TOP_K, t_all), F32)],
        compiler_params=_cparams(1),
        name="xout_router",
    )(o_all, h_all, w_ox, g3, w_rt, b_r)


def _route_kernel(idx_ref, dest_ref, be_ref, nu_ref, pe_ref, *, t_all, n_blocks_pad):
    bm = MOE_BLOCK_ROWS
    nt = t_all // LANES
    e_col = lax.broadcasted_iota(I32, (N_EXPERTS, 1), 0)
    hp = lax.Precision.HIGHEST

    def multi_hot(j):
        blk = idx_ref[:, pl.ds(pl.multiple_of(j * LANES, LANES), LANES)]
        mh = jnp.zeros((N_EXPERTS, LANES), F32)
        for k in range(TOP_K):
            mh = mh + (e_col == blk[k:k + 1, :]).astype(F32)
        return blk, mh

    def count_body(j, c):
        _, mh = multi_hot(j)
        return c + jnp.sum(mh, axis=1, keepdims=True)

    counts = lax.fori_loop(0, nt, count_body, jnp.zeros((N_EXPERTS, 1), F32))
    padded = jnp.ceil(counts * (1.0 / bm)) * bm
    tri = (lax.broadcasted_iota(I32, (N_EXPERTS, N_EXPERTS), 1)
           <= lax.broadcasted_iota(I32, (N_EXPERTS, N_EXPERTS), 0)).astype(F32)
    pad_end = jnp.dot(tri, jnp.broadcast_to(padded, (N_EXPERTS, LANES)), precision=hp,
                      preferred_element_type=F32)
    pad_start = pad_end[:, 0:1] - padded
    upper = (lax.broadcasted_iota(I32, (LANES, LANES), 0)
             < lax.broadcasted_iota(I32, (LANES, LANES), 1)).astype(BF16)

    def dest_body(j, carry):
        blk, mh = multi_hot(j)
        rank = carry + _dot(mh.astype(BF16), upper)
        base = pad_start + rank
        for k in range(TOP_K):
            d = jnp.sum(jnp.where(e_col == blk[k:k + 1, :], base, 0.0), axis=0, keepdims=True)
            dest_ref[pl.ds(k, 1), pl.ds(pl.multiple_of(j * LANES, LANES), LANES)] = d.astype(I32)
        return carry + jnp.sum(mh, axis=1, keepdims=True)

    lax.fori_loop(0, nt, dest_body, jnp.zeros((N_EXPERTS, 1), F32))

    b_row = lax.broadcasted_iota(I32, (1, n_blocks_pad), 1).astype(F32) * bm
    be = jnp.sum((pad_end[:, 0:1] <= b_row).astype(F32), axis=0, keepdims=True)
    be_ref[...] = jnp.minimum(be, N_EXPERTS - 1.0).astype(I32)
    nu_ref[...] = (pad_end[N_EXPERTS - 1:N_EXPERTS, :] * (1.0 / bm)).astype(I32)
    lane_row = lax.broadcasted_iota(I32, (1, LANES), 1)
    pe_ref[...] = jnp.sum(jnp.where(e_col == lane_row, pad_end, 0.0), axis=0,
                          keepdims=True).astype(I32)


def _route(idx_t, n_blocks_pad):
    t_all = idx_t.shape[1]
    return pl.pallas_call(
        functools.partial(_route_kernel, t_all=t_all, n_blocks_pad=n_blocks_pad),
        out_shape=[jax.ShapeDtypeStruct((TOP_K, t_all), I32),
                   jax.ShapeDtypeStruct((1, n_blocks_pad), I32),
                   jax.ShapeDtypeStruct((1, LANES), I32),
                   jax.ShapeDtypeStruct((1, LANES), I32)],
        compiler_params=pltpu.CompilerParams(vmem_limit_bytes=VMEM_LIMIT),
        name="moe_route",
    )(idx_t)


def _dispatch_kernel(dest_sm, pe_sm, nu_sm, x_ref, xs_ref, zbuf, sem, zsem, *, t_all, n_blocks):
    i = pl.program_id(0)
    tm = x_ref.shape[0]
    bm = MOE_BLOCK_ROWS

    @pl.when(i == 0)
    def _():
        zbuf[...] = jnp.zeros_like(zbuf)

        def zero_block(row0):
            return pltpu.make_async_copy(zbuf, xs_ref.at[pl.ds(row0, bm)], zsem)

        ends = [pe_sm[e] for e in range(N_EXPERTS)]
        used = [ends[e] > (ends[e - 1] if e else 0) for e in range(N_EXPERTS)]
        for e in range(N_EXPERTS):
            @pl.when(used[e])
            def _(e=e):
                zero_block(ends[e] - bm).start()

        def tail_start(b, c):
            zero_block(b * bm).start()
            return c

        lax.fori_loop(nu_sm[0], n_blocks, tail_start, 0)
        for e in range(N_EXPERTS):
            @pl.when(used[e])
            def _():
                zero_block(0).wait()

        def tail_wait(b, c):
            zero_block(0).wait()
            return c

        lax.fori_loop(nu_sm[0], n_blocks, tail_wait, 0)

    def start_body(j, c):
        for k in range(TOP_K):
            d = dest_sm[k * t_all + i * tm + j]
            pltpu.make_async_copy(x_ref.at[j], xs_ref.at[d], sem).start(priority=k % 2)
        return c

    lax.fori_loop(0, tm, start_body, 0, unroll=ROW_DMA_UNROLL)
    for k in range(TOP_K):
        pltpu.make_async_copy(x_ref, xs_ref.at[pl.ds(0, tm)], sem).wait()


ROW_DMA_UNROLL = 4


def _dispatch(dest_flat, pad_end, n_used, xn_all, n_blocks):
    t_all = xn_all.shape[0]
    tm = TOKEN_TILE
    bm = MOE_BLOCK_ROWS
    return pl.pallas_call(
        functools.partial(_dispatch_kernel, t_all=t_all, n_blocks=n_blocks),
        grid_spec=pltpu.PrefetchScalarGridSpec(
            num_scalar_prefetch=3,
            grid=(t_all // tm,),
            in_specs=[pl.BlockSpec((tm, ROW_TILE_ROWS, LANES), lambda i, d, pe, nu: (i, 0, 0))],
            out_specs=pl.BlockSpec(memory_space=pl.ANY),
            scratch_shapes=[pltpu.VMEM((bm, ROW_TILE_ROWS, LANES), F32),
                            pltpu.SemaphoreType.DMA, pltpu.SemaphoreType.DMA]),
        out_shape=jax.ShapeDtypeStruct((n_blocks * bm, ROW_TILE_ROWS, LANES), F32),
        compiler_params=_cparams(1),
        name="moe_dispatch",
    )(dest_flat, pad_end, n_used, xn_all)


def _expert_kernel(be_sm, nu_sm, x_ref, wgu_ref, bgu_ref, wd_ref, bd_ref, y_ref, wgu_bf, wd_bf):
    b = pl.program_id(0)
    changed = jnp.logical_or(b == 0, be_sm[b] != be_sm[jnp.maximum(b - 1, 0)])

    @pl.when(changed)
    def _():
        wgu_bf[...] = wgu_ref[0].astype(BF16)
        wd_bf[...] = wd_ref[0].astype(BF16)

    @pl.when(b < nu_sm[0])
    def _():
        x = _load_row_tiles(x_ref, MOE_BLOCK_ROWS).astype(BF16)
        h = _dot(x, wgu_bf[...]) + bgu_ref[0]
        glu = jnp.minimum(h[:, :D_FF], SWIGLU_LIMIT)
        lin = jnp.clip(h[:, D_FF:], -SWIGLU_LIMIT, SWIGLU_LIMIT)
        act = glu * _sigmoid(SWIGLU_ALPHA * glu) * (lin + 1.0)
        _store_row_tiles(y_ref, _dot(act.astype(BF16), wd_bf[...]) + bd_ref[0])

    @pl.when(b >= nu_sm[0])
    def _():
        y_ref[...] = jnp.zeros_like(y_ref)


def _experts(block_e, n_used, xs, w_gu, b_gu, w_down, b_down):
    cap = xs.shape[0] // ROW_TILE_ROWS
    bm = MOE_BLOCK_ROWS
    return pl.pallas_call(
        _expert_kernel,
        grid_spec=pltpu.PrefetchScalarGridSpec(
            num_scalar_prefetch=2,
            grid=(cap // bm,),
            in_specs=[pl.BlockSpec((bm * ROW_TILE_ROWS, LANES), lambda b, be, nu: (b, 0)),
                      pl.BlockSpec((1, D_MODEL, 2 * D_FF), lambda b, be, nu: (be[b], 0, 0)),
                      pl.BlockSpec((1, 1, 2 * D_FF), lambda b, be, nu: (be[b], 0, 0)),
                      pl.BlockSpec((1, D_FF, D_MODEL), lambda b, be, nu: (be[b], 0, 0)),
                      pl.BlockSpec((1, 1, D_MODEL), lambda b, be, nu: (be[b], 0, 0))],
            out_specs=pl.BlockSpec((bm * ROW_TILE_ROWS, LANES), lambda b, be, nu: (b, 0)),
            scratch_shapes=[pltpu.VMEM((D_MODEL, 2 * D_FF), BF16),
                            pltpu.VMEM((D_FF, D_MODEL), BF16)]),
        out_shape=jax.ShapeDtypeStruct((cap * ROW_TILE_ROWS, LANES), F32),
        compiler_params=_cparams(1, VMEM_LIMIT_EXPERTS),
        name="moe_experts",
    )(block_e, n_used, xs, w_gu, b_gu, w_down, b_down)


def _combine_kernel(dest_sm, yb_ref, h_ref, g_ref, op_ref, os_ref, buf, sems, *, t_all):
    i = pl.program_id(0)
    last = pl.num_programs(0) - 1
    tm = h_ref.shape[0]

    def gather(step, slot):
        def start_body(j, c):
            for k in range(TOP_K):
                d = dest_sm[k * t_all + step * tm + j]
                tile = pl.ds(pl.multiple_of(j * ROW_TILE_ROWS, ROW_TILE_ROWS), ROW_TILE_ROWS)
                pltpu.make_async_copy(yb_ref.at[d], buf.at[slot, k, tile],
                                      sems.at[slot]).start(priority=k % 2)
            return c

        lax.fori_loop(0, tm, start_body, 0, unroll=ROW_DMA_UNROLL)

    @pl.when(i == 0)
    def _():
        gather(0, 0)

    @pl.when(i < last)
    def _():
        gather(i + 1, (i + 1) % 2)

    slot = i % 2
    for k in range(TOP_K):
        pltpu.make_async_copy(buf.at[slot, k], buf.at[slot, k], sems.at[slot]).wait()

    g = g_ref[...]
    y = h_ref[...]
    for k in range(TOP_K):
        y = y + g[:, k:k + 1] * _load_row_tiles(buf.at[slot, k], tm)

    @pl.when(i < last)
    def _():
        op_ref[...] = y

    @pl.when(i == last)
    def _():
        os_ref[...] = y


def _combine(dest_flat, yb, h2_all, gates, n_prompt_rows):
    t_all = h2_all.shape[0]
    tm = ROW_TILE
    n_p_tiles = n_prompt_rows // tm
    assert t_all == n_prompt_rows + tm
    return pl.pallas_call(
        functools.partial(_combine_kernel, t_all=t_all),
        grid_spec=pltpu.PrefetchScalarGridSpec(
            num_scalar_prefetch=1,
            grid=(t_all // tm,),
            in_specs=[pl.BlockSpec(memory_space=pl.ANY),
                      pl.BlockSpec((tm, D_MODEL), lambda i, d: (i, 0)),
                      pl.BlockSpec((tm, TOP_K), lambda i, d: (i, 0))],
            out_specs=[pl.BlockSpec((tm, D_MODEL), lambda i, d: (jnp.minimum(i, n_p_tiles - 1), 0)),
                       pl.BlockSpec((tm, D_MODEL), lambda i, d: (0, 0))],
            scratch_shapes=[pltpu.VMEM((2, TOP_K, tm * ROW_TILE_ROWS, LANES), F32),
                            pltpu.SemaphoreType.DMA((2,))]),
        out_shape=[jax.ShapeDtypeStruct((n_prompt_rows, D_MODEL), F32),
                   jax.ShapeDtypeStruct((tm, D_MODEL), F32)],
        compiler_params=_cparams(1),
        name="moe_combine",
    )(dest_flat, yb, h2_all, gates)


def _rope_tables(pos):
    half = HEAD_DIM // 2
    inv = jnp.exp(-math.log(ROPE_THETA) * jnp.arange(half, dtype=F32) / half)
    ang = pos.astype(F32)[:, None] * inv[None, :]
    cos, sin = jnp.cos(ang), jnp.sin(ang)
    cos_t = jnp.concatenate([cos, cos] * HEADS_PER_TILE, axis=-1)
    sin_t = jnp.concatenate([-sin, sin] * HEADS_PER_TILE, axis=-1)
    return cos_t, sin_t


def _block_diag_ones(n, blk):
    r = jnp.arange(n) // blk
    return (r[:, None] == r[None, :]).astype(BF16)


def _layer(x_prompt, x_sample, state_ret, cache_win_k, cache_win_v, cache_mem_k, cache_mem_v,
           mem_prompt, g_norm1, w_in, g_att_q, g_att_k, g_ret_gn, w_out, g_norm2, g_mem,
           w_q_x, w_kv_x, g_x_q, g_x_k, w_o_x, g_norm3, w_router, b_router, w_gu, b_gu,
           w_down, b_down):
    n_b, seq, _ = x_prompt.shape
    n_s = x_sample.shape[0]
    assert x_sample.shape[1] == 1 and n_s == ROW_TILE
    t_p = n_b * seq
    t_all = t_p + n_s
    assert t_all % TOKEN_TILE == 0 and t_p % ROW_TILE == 0

    assert seq % PROMPT_TILE == 0
    x_p = x_prompt.reshape(t_p, D_MODEL)
    x_s = x_sample.reshape(n_s, D_MODEL)
    cos_p, sin_p = _rope_tables(jnp.arange(seq, dtype=jnp.int32))
    cos_s, sin_s = _rope_tables(jnp.full((n_s,), PAST_LEN, jnp.int32))
    log_g = jnp.log1p(-jnp.exp2(-5.0 - jnp.arange(RET_HEADS, dtype=F32)))
    lg_row = jnp.repeat(log_g, HEAD_DIM)[None, :]
    gn_row = g_ret_gn.reshape(1, RET_WIDTH)
    gn_col = g_ret_gn.reshape(RET_WIDTH, 1)
    gq = jnp.tile(g_att_q.reshape(1, HEAD_DIM), (1, ATT_HEADS))
    gk = jnp.tile(g_att_k.reshape(1, HEAD_DIM), (1, ATT_HEADS))
    seg = _block_diag_ones(ATT_WIDTH, HEAD_DIM)

    g1 = g_norm1.reshape(1, D_MODEL)
    w_in_bf = w_in.astype(BF16)
    proj = _in_proj(x_p, g1, w_in_bf, cos_p, sin_p, gq, gk, seg, t_all, PROMPT_TILE, 0)
    rq, rk, rv, rg, aq, ak, av = _in_proj(x_s, g1, w_in_bf, cos_s, sin_s, gq, gk, seg, t_all, n_s,
                                          t_p, prev=proj)

    ret_n, st_p = _ret_prompt(rq, rk, rv, rg, gn_row, lg_row, n_b, seq)
    att_o, wk_p, wv_p = _dil_prompt(aq, ak, av, n_b, seq)
    state_t = jnp.transpose(state_ret, (1, 2, 3, 0))
    ret_n, st_s = _ret_sample(rq, rk, rv, ret_n, gn_col, lg_row, state_t)
    ck = jnp.transpose(cache_win_k, (0, 2, 3, 1))
    cv = jnp.transpose(cache_win_v, (0, 2, 3, 1))
    wk_s, wv_s, att_o = _win_sample(att_o, ak, av, ck, cv)

    mix_w = (w_out.astype(BF16), g_norm2.reshape(1, D_MODEL), w_q_x.astype(BF16),
             g_x_q.reshape(1, X_HEAD_DIM))
    mixed = _mix_out(ret_n, att_o, x_p, *mix_w, PROMPT_TILE, 0)
    h_all, qx = _mix_out(ret_n, att_o, x_s, *mix_w, n_s, t_p, prev=mixed)

    mk_p, mv_p = _mem_kv(mem_prompt, g_mem.reshape(1, D_MODEL), w_kv_x.astype(BF16),
                         g_x_k.reshape(1, X_HEAD_DIM))
    o_all = _xattn_prompt(qx, mk_p, mv_p, n_b, seq)
    n_mem = cache_mem_k.shape[1]
    o_all = _xattn_sample(o_all, cache_mem_k.reshape(n_s, n_mem * X_HEADS, X_HEAD_DIM),
                          cache_mem_v.reshape(n_s, n_mem * X_HEADS, X_HEAD_DIM))

    h2_all, xn_all, idx_t, gate_t = _xout_router(
        o_all, h_all, w_o_x.astype(BF16), g_norm3.reshape(1, D_MODEL),
        jnp.transpose(w_router), b_router.reshape(N_EXPERTS, 1))

    bm = MOE_BLOCK_ROWS
    n_blocks = -(-(t_all * TOP_K) // bm) + N_EXPERTS
    n_blocks_pad = -(-n_blocks // LANES) * LANES
    dest_t, be, nu, pe = _route(idx_t, n_blocks_pad)
    dest_flat = dest_t.reshape(TOP_K * t_all)
    cap = n_blocks * bm
    xs = _dispatch(dest_flat, pe.reshape(LANES), nu.reshape(LANES),
                   xn_all.reshape(t_all, ROW_TILE_ROWS, LANES), n_blocks)
    yb = _experts(be.reshape(n_blocks_pad), nu.reshape(LANES), xs.reshape(cap * ROW_TILE_ROWS, LANES),
                  w_gu, b_gu.reshape(N_EXPERTS, 1, 2 * D_FF), w_down,
                  b_down.reshape(N_EXPERTS, 1, D_MODEL))
    y_p, y_s = _combine(dest_flat, yb.reshape(cap, ROW_TILE_ROWS, LANES), h2_all,
                        jnp.transpose(gate_t), t_p)

    st_p = jnp.stack([st_p[:, :, :HEAD_DIM, :HEAD_DIM], st_p[:, :, HEAD_DIM:, HEAD_DIM:]], axis=2)
    st_p = st_p.reshape(n_b, RET_HEADS, HEAD_DIM, HEAD_DIM)
    return (y_p.reshape(n_b, seq, D_MODEL),
            y_s.reshape(n_s, 1, D_MODEL),
            st_p,
            jnp.transpose(st_s, (3, 0, 1, 2)),
            jnp.transpose(wk_p, (0, 3, 1, 2)),
            jnp.transpose(wv_p, (0, 3, 1, 2)),
            jnp.transpose(wk_s, (0, 3, 1, 2)),
            jnp.transpose(wv_s, (0, 3, 1, 2)),
            mk_p.reshape(n_b, n_mem, X_HEADS, X_HEAD_DIM),
            mv_p.reshape(n_b, n_mem, X_HEADS, X_HEAD_DIM))


def kernel(x_prompt, x_sample, state_ret, cache_win_k, cache_win_v, cache_mem_k, cache_mem_v,
           mem_prompt, g_norm1, w_in, g_att_q, g_att_k, g_ret_gn, w_out, g_norm2, g_mem,
           w_q_x, w_kv_x, g_x_q, g_x_k, w_o_x, g_norm3, w_router, b_router, w_gu, b_gu,
           w_down, b_down):
    assert state_ret.shape[0] == 1, "single-layer trunk"
    outs = _layer(x_prompt, x_sample, state_ret[0], cache_win_k[0], cache_win_v[0],
                  cache_mem_k[0], cache_mem_v[0], mem_prompt, g_norm1[0], w_in[0], g_att_q[0],
                  g_att_k[0], g_ret_gn[0], w_out[0], g_norm2[0], g_mem[0], w_q_x[0], w_kv_x[0],
                  g_x_q[0], g_x_k[0], w_o_x[0], g_norm3[0], w_router[0], b_router[0], w_gu[0],
                  b_gu[0], w_down[0], b_down[0])
    y_p, y_s = outs[0], outs[1]
    return (y_p, y_s) + tuple(o[None] for o in outs[2:])
```

```python
import functools
import math

import jax
import jax.numpy as jnp
from jax import lax
from jax.experimental import pallas as pl
from jax.experimental.pallas import tpu as pltpu

F32 = jnp.float32
BF16 = jnp.bfloat16
I32 = jnp.int32

D_MODEL = 1024
HEAD_DIM = 64
RET_HEADS = 8
ATT_HEADS = 8
RET_WIDTH = RET_HEADS * HEAD_DIM
ATT_WIDTH = ATT_HEADS * HEAD_DIM
IN_COLS = 4 * RET_WIDTH + 3 * ATT_WIDTH
RET_CHUNK = 128
DIL_PATTERNS = ((128, 1), (512, 4), (2048, 16))
DIL_BLOCK = 128
MAX_WINDOW = 2048
X_HEADS = 4
X_HEAD_DIM = 128
X_WIDTH = X_HEADS * X_HEAD_DIM
N_EXPERTS = 32
TOP_K = 4
D_FF = D_MODEL
SWIGLU_LIMIT = 7.0
SWIGLU_ALPHA = 1.702
ROPE_THETA = 10000.0
EPS = 1e-6
PAST_LEN = 8192

LANES = 128
SUBLANES = 8
HEADS_PER_TILE = LANES // HEAD_DIM

TOKEN_TILE = 384
ROUTER_TILE = 384
PROMPT_TILE = 512
ROW_TILE = 128
MOE_BLOCK_ROWS = 512
NEG_BIG = -1e30
VMEM_LIMIT = 48 * 1024 * 1024
VMEM_LIMIT_EXPERTS = 58 * 1024 * 1024


def _cparams(n_axes, vmem_limit=VMEM_LIMIT):
    return pltpu.CompilerParams(
        dimension_semantics=("arbitrary",) * n_axes, vmem_limit_bytes=vmem_limit)


def _rms(x, g):
    return x * lax.rsqrt(jnp.mean(x * x, axis=-1, keepdims=True) + EPS) * g


def _dot(a, b):
    return jnp.dot(a, b, preferred_element_type=F32)


def _dot_nt(a, b):
    return lax.dot_general(a, b, (((1,), (1,)), ((), ())), preferred_element_type=F32)


def _dot_tn(a, b):
    return lax.dot_general(a, b, (((0,), (0,)), ((), ())), preferred_element_type=F32)


def _sigmoid(x):
    return 1.0 / (1.0 + jnp.exp(-x))


N_PROJ_OUT = 7


def _in_proj_kernel(x_ref, g1_ref, w_ref, gq_ref, gk_ref, seg_ref, cos_ref, sin_ref, *rest):
    rq_ref, rk_ref, rv_ref, rg_ref, aq_ref, ak_ref, av_ref = rest[-N_PROJ_OUT:]
    xn = _rms(x_ref[...], g1_ref[...]).astype(BF16)
    proj = _dot(xn, w_ref[...])
    reps = RET_WIDTH // LANES
    cos = jnp.concatenate([cos_ref[...]] * reps, axis=-1)
    sin = jnp.concatenate([sin_ref[...]] * reps, axis=-1)
    lane = lax.broadcasted_iota(I32, (1, RET_WIDTH), 1)
    first_half = (lane % HEAD_DIM) < (HEAD_DIM // 2)
    seg = seg_ref[...]

    def rope(t):
        partner = jnp.where(first_half,
                            pltpu.roll(t, RET_WIDTH - HEAD_DIM // 2, 1),
                            pltpu.roll(t, HEAD_DIM // 2, 1))
        return t * cos + partner * sin

    def head_norm(t, g):
        ssum = _dot((t * t).astype(BF16), seg)
        return t * lax.rsqrt(ssum * (1.0 / HEAD_DIM) + EPS) * g

    w = RET_WIDTH
    rq_ref[...] = rope(proj[:, 0:w])
    rk_ref[...] = rope(proj[:, w:2 * w]) * (HEAD_DIM ** -0.5)
    rv_ref[...] = proj[:, 2 * w:3 * w]
    rg_ref[...] = proj[:, 3 * w:4 * w]
    aq_ref[...] = rope(head_norm(proj[:, 4 * w:5 * w], gq_ref[...]))
    ak_ref[...] = rope(head_norm(proj[:, 5 * w:6 * w], gk_ref[...]))
    av_ref[...] = proj[:, 6 * w:7 * w]


def _group_rows_call(kernel, name, shared_ins, row_ins, fixed_ins, tables, out_widths, t_all, tm,
                     row_start, prev):
    rows = row_ins[0].shape[0]
    off = row_start // tm
    assert rows % tm == 0 and row_start % tm == 0
    n_steps = rows // tm
    extra = 1 if (prev is None and row_start + rows < t_all) else 0
    src = lambda i: jnp.minimum(i, n_steps - 1)
    fixed = lambda i: (0, 0)
    in_specs = [pl.BlockSpec((tm, a.shape[1]), lambda i: (src(i) + off, 0)) for a in shared_ins]
    in_specs += [pl.BlockSpec((tm, a.shape[1]), lambda i: (src(i), 0)) for a in row_ins]
    in_specs += [pl.BlockSpec(a.shape, fixed) for a in fixed_ins]
    for t in tables:
        period = t.shape[0] // tm
        in_specs.append(pl.BlockSpec((tm, t.shape[1]),
                                     lambda i, period=period: (src(i) % period, 0)))
    n_in = len(in_specs)
    aliases = {}
    if prev is not None:
        in_specs += [pl.BlockSpec(memory_space=pl.ANY)] * len(prev)
        aliases = {n_in + j: j for j in range(len(prev))}
    return pl.pallas_call(
        kernel,
        grid=(n_steps + extra,),
        in_specs=in_specs,
        out_specs=[pl.BlockSpec((tm, w), lambda i: (i + off, 0)) for w in out_widths],
        out_shape=[jax.ShapeDtypeStruct((t_all, w), F32) for w in out_widths],
        input_output_aliases=aliases,
        compiler_params=_cparams(1),
        name=name,
    )(*shared_ins, *row_ins, *fixed_ins, *tables, *(prev or ()))


def _in_proj(x, g1, w_in, cos_t, sin_t, gq, gk, seg, t_all, tm, row_start, prev=None):
    return _group_rows_call(_in_proj_kernel, "in_proj", [], [x], [g1, w_in, gq, gk, seg],
                            [cos_t, sin_t], [RET_WIDTH] * N_PROJ_OUT, t_all, tm, row_start, prev)


def _group_norm_gate(o, mask_a, gn, gate):
    inv = 1.0 / HEAD_DIM
    sa = jnp.sum(jnp.where(mask_a, o, 0.0), axis=-1, keepdims=True)
    sb = jnp.sum(jnp.where(mask_a, 0.0, o), axis=-1, keepdims=True)
    cen = o - jnp.where(mask_a, sa, sb) * inv
    c2 = cen * cen
    va = jnp.sum(jnp.where(mask_a, c2, 0.0), axis=-1, keepdims=True)
    vb = jnp.sum(jnp.where(mask_a, 0.0, c2), axis=-1, keepdims=True)
    var = jnp.where(mask_a, va, vb) * inv
    return cen * lax.rsqrt(var + EPS) * gn * (gate * _sigmoid(gate))


RET_CHUNKS_PER_STEP = 4


def _ret_prompt_kernel(q_ref, k_ref, v_ref, g_ref, gn_ref, lg_ref, o_ref, st_ref,
                       state, dec, qdec, kdec):
    n = pl.program_id(0)
    c = pl.program_id(1)
    ch = RET_CHUNK
    n_pairs = RET_WIDTH // LANES
    lane = lax.broadcasted_iota(I32, (1, LANES), 1)
    mask_a = lane < HEAD_DIM
    row_i = lax.broadcasted_iota(I32, (ch, 1), 0)

    @pl.when(jnp.logical_and(n == 0, c == 0))
    def _():
        row = row_i.astype(F32)
        col = lax.broadcasted_iota(I32, (1, ch), 1).astype(F32)
        diff = row - col
        causal = diff >= 0.0
        dpos = jnp.maximum(diff, 0.0)
        lg = lg_ref[...]
        for h in range(RET_HEADS):
            dec[h] = jnp.where(causal, jnp.exp(dpos * lg[:, h * HEAD_DIM:h * HEAD_DIM + 1]), 0.0)
        qdec[...] = jnp.exp((row + 1.0) * lg)
        kdec[...] = jnp.exp((ch - 1.0 - row) * lg)

    @pl.when(c == 0)
    def _():
        state[...] = jnp.zeros_like(state)

    same_head = (row_i // HEAD_DIM) == (lane // HEAD_DIM)
    cdec = jnp.exp(ch * lg_ref[...])
    for p in range(n_pairs):
        sl = slice(p * LANES, (p + 1) * LANES)
        st = state[p]
        for cc in range(RET_CHUNKS_PER_STEP):
            rows = slice(cc * ch, (cc + 1) * ch)
            q = q_ref[rows, sl]
            k = k_ref[rows, sl]
            kb = k.astype(BF16)
            vb = v_ref[rows, sl].astype(BF16)
            lhs = jnp.concatenate([jnp.where(mask_a, q, 0.0), jnp.where(mask_a, 0.0, q)],
                                  axis=0).astype(BF16)
            s = _dot_nt(lhs, kb)
            pr = (s * jnp.concatenate([dec[2 * p], dec[2 * p + 1]], axis=0)).astype(BF16)
            o2 = _dot(pr, vb)
            o_intra = jnp.where(mask_a, o2[:ch], o2[ch:])
            o = o_intra + _dot(q.astype(BF16), st.astype(BF16)) * qdec[:, sl]
            upd = _dot_tn((k * kdec[:, sl]).astype(BF16), vb)
            st = cdec[:, sl] * st + jnp.where(same_head, upd, 0.0)
            o_ref[rows, sl] = _group_norm_gate(o, mask_a, gn_ref[:, sl], g_ref[rows, sl])
        state[p] = st

    @pl.when(c == pl.num_programs(1) - 1)
    def _():
        st_ref[0] = state[...]


def _ret_prompt(rq, rk, rv, rg, gn_row, lg_row, n_batch, seq):
    t_all = rq.shape[0]
    n_pairs = RET_WIDTH // LANES
    step_rows = RET_CHUNK * RET_CHUNKS_PER_STEP
    assert seq % step_rows == 0
    n_steps = seq // step_rows
    blk = pl.BlockSpec((step_rows, RET_WIDTH), lambda n, c: (n * n_steps + c, 0))
    lane_row = pl.BlockSpec((1, RET_WIDTH), lambda n, c: (0, 0))
    return pl.pallas_call(
        _ret_prompt_kernel,
        grid=(n_batch, n_steps),
        in_specs=[blk, blk, blk, blk, lane_row, lane_row],
        out_specs=[blk, pl.BlockSpec((1, n_pairs, LANES, LANES), lambda n, c: (n, 0, 0, 0))],
        out_shape=[jax.ShapeDtypeStruct((t_all, RET_WIDTH), F32),
                   jax.ShapeDtypeStruct((n_batch, n_pairs, LANES, LANES), F32)],
        scratch_shapes=[pltpu.VMEM((n_pairs, LANES, LANES), F32),
                        pltpu.VMEM((RET_HEADS, RET_CHUNK, RET_CHUNK), F32),
                        pltpu.VMEM((RET_CHUNK, RET_WIDTH), F32),
                        pltpu.VMEM((RET_CHUNK, RET_WIDTH), F32)],
        input_output_aliases={3: 0},
        compiler_params=_cparams(2),
        name="ret_prompt",
    )(rq, rk, rv, rg, gn_row, lg_row)


DIL_UNROLL = 16
DIL_MERGE_ROWS = 256


def _dil_prompt_kernel(q_ref, k_ref, v_ref, o_ref, wk_ref, wv_ref, acc, m_s, l_s, *, seq, wp):
    b = DIL_BLOCK
    lane = lax.broadcasted_iota(I32, (1, LANES), 1)
    mask_a = lane < HEAD_DIM
    qi = lax.broadcasted_iota(I32, (2 * b, 1), 0) % b
    kk = lax.broadcasted_iota(I32, (1, 2 * b), 1)
    dist = qi + b - kk
    scale = HEAD_DIM ** -0.5

    for pat, (window, dil) in enumerate(DIL_PATTERNS):
        steps = window // dil
        band = (dist >= 0) & (dist <= steps)
        nb = seq // (b * dil)

        def block(idx, dil=dil, nb=nb, band=band):
            r = idx // nb
            i = idx % nb
            rows_q = pl.ds(r + i * (b * dil), b, stride=dil)
            rows_p = pl.ds(r + jnp.maximum(i - 1, 0) * (b * dil), b, stride=dil)
            q = q_ref[rows_q, :]
            kc = jnp.concatenate([k_ref[rows_p, :], k_ref[rows_q, :]], axis=0).astype(BF16)
            vc = jnp.concatenate([v_ref[rows_p, :], v_ref[rows_q, :]], axis=0).astype(BF16)
            lhs = jnp.concatenate([jnp.where(mask_a, q, 0.0), jnp.where(mask_a, 0.0, q)],
                                  axis=0).astype(BF16)
            s = _dot_nt(lhs, kc) * scale
            valid = band & ((kk >= b) | (jnp.broadcast_to(i, kk.shape) > 0))
            s = jnp.where(valid, s, NEG_BIG)
            m_blk = jnp.max(s, axis=-1, keepdims=True)
            p = jnp.exp(s - m_blk)
            l_blk = jnp.sum(p, axis=-1, keepdims=True)
            pv = _dot(p.astype(BF16), vc)
            return rows_q, m_blk, l_blk, pv

        def emit(rows_q, m_blk, l_blk, pv, pat=pat):
            acc[pat, rows_q, :] = jnp.where(mask_a, pv[:b], pv[b:])
            m_s[pat, rows_q, :] = jnp.where(mask_a, m_blk[:b], m_blk[b:])
            l_s[pat, rows_q, :] = jnp.where(mask_a, l_blk[:b], l_blk[b:])

        def body(it, carry, block=block, emit=emit):
            parts = [block(it * DIL_UNROLL + u) for u in range(DIL_UNROLL)]
            for part in parts:
                emit(*part)
            return carry

        lax.fori_loop(0, (dil * nb) // DIL_UNROLL, body, 0)

    n_pat = len(DIL_PATTERNS)

    def merge_body(c, carry):
        rows = pl.ds(pl.multiple_of(c * DIL_MERGE_ROWS, DIL_MERGE_ROWS), DIL_MERGE_ROWS)
        ms = [m_s[pat, rows, :] for pat in range(n_pat)]
        m = functools.reduce(jnp.maximum, ms)
        num = jnp.zeros((DIL_MERGE_ROWS, LANES), F32)
        den = jnp.zeros((DIL_MERGE_ROWS, LANES), F32)
        for pat in range(n_pat):
            w = jnp.exp(ms[pat] - m)
            num = num + w * acc[pat, rows, :]
            den = den + w * l_s[pat, rows, :]
        o_ref[rows, :] = num / den
        return carry

    lax.fori_loop(0, seq // DIL_MERGE_ROWS, merge_body, 0)

    tchunk = 4 * LANES
    for j in range(wp // tchunk):
        rows = pl.ds(seq - wp + j * tchunk, tchunk)
        cols = slice(j * tchunk, (j + 1) * tchunk)
        kt = k_ref[rows, :].T
        vt = v_ref[rows, :].T
        for hh in range(HEADS_PER_TILE):
            wk_ref[0, hh, :, cols] = kt[hh * HEAD_DIM:(hh + 1) * HEAD_DIM]
            wv_ref[0, hh, :, cols] = vt[hh * HEAD_DIM:(hh + 1) * HEAD_DIM]


def _dil_prompt(aq, ak, av, n_batch, seq):
    t_all = aq.shape[0]
    n_pairs = ATT_WIDTH // LANES
    wp = min(MAX_WINDOW, seq)
    blk = pl.BlockSpec((seq, LANES), lambda n, p: (n, p))
    wblk = pl.BlockSpec((1, HEADS_PER_TILE, HEAD_DIM, wp), lambda n, p: (n, p, 0, 0))
    wshape = jax.ShapeDtypeStruct((n_batch, ATT_HEADS, HEAD_DIM, wp), F32)
    return pl.pallas_call(
        functools.partial(_dil_prompt_kernel, seq=seq, wp=wp),
        grid=(n_batch, n_pairs),
        in_specs=[blk, blk, blk],
        out_specs=[blk, wblk, wblk],
        out_shape=[jax.ShapeDtypeStruct((t_all, ATT_WIDTH), F32), wshape, wshape],
        scratch_shapes=[pltpu.VMEM((len(DIL_PATTERNS), seq, LANES), F32)] * 3,
        input_output_aliases={0: 0},
        compiler_params=_cparams(2),
        name="dil_prompt",
    )(aq, ak, av)


def _ret_sample_kernel(q_ref, k_ref, v_ref, g_ref, gn_ref, lg_ref, st_ref,
                       o_ref, nst_ref, qt, kt):
    qt[...] = q_ref[...].T
    kt[...] = k_ref[...].T
    vt = v_ref[...].T
    gt = g_ref[...].T
    lg = lg_ref[...]
    outs = []
    for hh in range(HEADS_PER_TILE):
        lo = hh * HEAD_DIM
        gdec = jnp.exp(lg[:, lo:lo + 1])
        vth = vt[lo:lo + HEAD_DIM, :]

        def body(d, o, hh=hh, lo=lo, gdec=gdec, vth=vth):
            new = gdec * st_ref[hh, d] + kt[pl.ds(lo + d, 1), :] * vth
            nst_ref[hh, d] = new
            return o + qt[pl.ds(lo + d, 1), :] * new

        o = lax.fori_loop(0, HEAD_DIM, body, jnp.zeros_like(vth), unroll=8)
        mu = jnp.mean(o, axis=0, keepdims=True)
        cen = o - mu
        var = jnp.mean(cen * cen, axis=0, keepdims=True)
        gate = gt[lo:lo + HEAD_DIM, :]
        outs.append(cen * lax.rsqrt(var + EPS) * gn_ref[lo:lo + HEAD_DIM, :] * (gate * _sigmoid(gate)))
    o_ref[...] = jnp.concatenate(outs, axis=0).T


def _ret_sample(rq, rk, rv, ret_n, gn_col, lg_row, state_t):
    t_all = rq.shape[0]
    n_s = state_t.shape[-1]
    n_pairs = RET_WIDTH // LANES
    last = t_all // n_s - 1
    blk = pl.BlockSpec((n_s, LANES), lambda p: (last, p))
    st_blk = pl.BlockSpec((HEADS_PER_TILE, HEAD_DIM, HEAD_DIM, n_s), lambda p: (p, 0, 0, 0))
    return pl.pallas_call(
        _ret_sample_kernel,
        grid=(n_pairs,),
        in_specs=[blk, blk, blk, blk,
                  pl.BlockSpec((LANES, 1), lambda p: (p, 0)),
                  pl.BlockSpec((1, LANES), lambda p: (0, p)),
                  st_blk],
        out_specs=[blk, st_blk],
        out_shape=[jax.ShapeDtypeStruct(ret_n.shape, F32),
                   jax.ShapeDtypeStruct(state_t.shape, F32)],
        scratch_shapes=[pltpu.VMEM((LANES, n_s), F32), pltpu.VMEM((LANES, n_s), F32)],
        input_output_aliases={3: 0},
        compiler_params=_cparams(1),
        name="ret_sample",
    )(rq, rk, rv, ret_n, gn_col, lg_row, state_t)


def _win_sample_kernel(aq_ref, akn_ref, avn_ref, kc_ref, vc_ref,
                       ko_ref, vo_ref, att_ref, qt, kt, vt, acct, *, win):
    n = pl.program_id(0)
    n_s = qt.shape[1]

    @pl.when(n == 0)
    def _():
        qt[...] = aq_ref[...].T
        kt[...] = akn_ref[...].T
        vt[...] = avn_ref[...].T
        acct[...] = jnp.zeros_like(acct)

    onehot = (lax.broadcasted_iota(I32, (n_s, LANES), 0) == n).astype(F32)
    hp = lax.Precision.HIGHEST
    qb = jnp.dot(qt[...], onehot, precision=hp, preferred_element_type=F32)
    kb = jnp.dot(kt[...], onehot, precision=hp, preferred_element_type=F32)
    vb = jnp.dot(vt[...], onehot, precision=hp, preferred_element_type=F32)

    w_pos = lax.broadcasted_iota(I32, (1, win), 1)
    back = win - w_pos
    mult = jnp.zeros((1, win), F32)
    for window, dil in DIL_PATTERNS:
        mult = mult + ((back <= window) & (back % dil == 0)).astype(F32)
    valid = mult > 0.0
    is_last = w_pos == win - 1
    reps = win // LANES
    scale = HEAD_DIM ** -0.5

    s_rows, s0_rows = [], []
    for h in range(ATT_HEADS):
        lo = h * HEAD_DIM
        k_t = kc_ref[0, h]
        qh = qb[lo:lo + HEAD_DIM, :]
        kh = kb[lo:lo + HEAD_DIM, :]
        s_rows.append(jnp.sum(k_t * jnp.concatenate([qh] * reps, axis=1), axis=0, keepdims=True))
        s0_rows.append(jnp.sum(qh * kh, axis=0, keepdims=True)[:, 0:1])
        ko_ref[0, h] = jnp.where(is_last, jnp.concatenate([kh] * reps, axis=1),
                                 pltpu.roll(k_t, win - 1, 1))
    s = jnp.concatenate(s_rows, axis=0) * scale
    s0 = jnp.concatenate(s0_rows, axis=0) * scale
    m = jnp.maximum(jnp.max(jnp.where(valid, s, NEG_BIG), axis=-1, keepdims=True), s0)
    e = jnp.where(valid, jnp.exp(s - m), 0.0) * mult
    e0 = len(DIL_PATTERNS) * jnp.exp(s0 - m)
    denom = jnp.sum(e, axis=-1, keepdims=True) + e0

    cols = []
    for h in range(ATT_HEADS):
        lo = h * HEAD_DIM
        v_t = vc_ref[0, h]
        vh = vb[lo:lo + HEAD_DIM, :]
        num = jnp.sum(v_t * e[h:h + 1, :], axis=1, keepdims=True) + e0[h:h + 1, :] * vh[:, 0:1]
        cols.append(num / denom[h:h + 1, :])
        vo_ref[0, h] = jnp.where(is_last, jnp.concatenate([vh] * reps, axis=1),
                                 pltpu.roll(v_t, win - 1, 1))
    o_col = jnp.concatenate(cols, axis=0)
    lane_n = lax.broadcasted_iota(I32, (1, n_s), 1) == n
    acct[...] = jnp.where(lane_n, o_col, acct[...])

    @pl.when(n == pl.num_programs(0) - 1)
    def _():
        att_ref[...] = acct[...].T


def _win_sample(att_o, ak, av, cache_k, cache_v):
    n_s, n_h, hd, win = cache_k.shape
    assert win >= max(w for w, _ in DIL_PATTERNS)
    t_all = att_o.shape[0]
    last = t_all // n_s - 1
    rows = pl.BlockSpec((n_s, ATT_WIDTH), lambda n: (last, 0))
    cblk = pl.BlockSpec((1, n_h, hd, win), lambda n: (n, 0, 0, 0))
    cshape = jax.ShapeDtypeStruct(cache_k.shape, F32)
    return pl.pallas_call(
        functools.partial(_win_sample_kernel, win=win),
        grid=(n_s,),
        in_specs=[rows, rows, rows, cblk, cblk],
        out_specs=[cblk, cblk, rows],
        out_shape=[cshape, cshape, jax.ShapeDtypeStruct(att_o.shape, F32)],
        scratch_shapes=[pltpu.VMEM((ATT_WIDTH, n_s), F32)] * 4,
        input_output_aliases={0: 2},
        compiler_params=_cparams(1),
        name="win_sample",
    )(att_o, ak, av, cache_k, cache_v)


def _lane_tile_norm(t, g, n_tiles):
    outs = []
    for h in range(n_tiles):
        outs.append(_rms(t[:, h * LANES:(h + 1) * LANES], g))
    return jnp.concatenate(outs, axis=-1)


def _mem_kv_kernel(mem_ref, gm_ref, w_ref, gk_ref, k_ref, v_ref):
    xn = _rms(mem_ref[0], gm_ref[...]).astype(BF16)
    kv = _dot(xn, w_ref[...])
    k_ref[0] = _lane_tile_norm(kv[:, :X_WIDTH], gk_ref[...], X_HEADS)
    v_ref[0] = kv[:, X_WIDTH:]


def _mem_kv(mem, g_mem, w_kv, g_xk):
    n, m, _ = mem.shape
    out = jax.ShapeDtypeStruct((n, m, X_WIDTH), F32)
    return pl.pallas_call(
        _mem_kv_kernel,
        grid=(n,),
        in_specs=[pl.BlockSpec((1, m, D_MODEL), lambda i: (i, 0, 0)),
                  pl.BlockSpec((1, D_MODEL), lambda i: (0, 0)),
                  pl.BlockSpec((D_MODEL, 2 * X_WIDTH), lambda i: (0, 0)),
                  pl.BlockSpec((1, X_HEAD_DIM), lambda i: (0, 0))],
        out_specs=[pl.BlockSpec((1, m, X_WIDTH), lambda i: (i, 0, 0))] * 2,
        out_shape=[out, out],
        compiler_params=_cparams(1),
        name="mem_kv",
    )(mem, g_mem, w_kv, g_xk)


def _mix_out_kernel(ret_ref, att_ref, x_ref, wo_ref, g2_ref, wq_ref, gq_ref, *rest):
    h_ref, q_ref = rest[-2:]
    mixed = jnp.concatenate([ret_ref[...], att_ref[...]], axis=-1).astype(BF16)
    h = x_ref[...] + _dot(mixed, wo_ref[...])
    h_ref[...] = h
    q = _dot(_rms(h, g2_ref[...]).astype(BF16), wq_ref[...])
    q_ref[...] = _lane_tile_norm(q, gq_ref[...], X_HEADS)


def _mix_out(ret_n, att_o, x, w_out, g2, w_qx, g_xq, tm, row_start, prev=None):
    return _group_rows_call(_mix_out_kernel, "mix_out", [ret_n, att_o], [x],
                            [w_out, g2, w_qx, g_xq], [], [D_MODEL, X_WIDTH], ret_n.shape[0], tm,
                            row_start, prev)


def _xattn_prompt_kernel(q_ref, k_ref, v_ref, o_ref):
    q = q_ref[...]
    k = k_ref[0]
    v = v_ref[0]
    scale = X_HEAD_DIM ** -0.5
    outs = []
    for h in range(X_HEADS):
        sl = slice(h * LANES, (h + 1) * LANES)
        s = _dot_nt(q[:, sl].astype(BF16), k[:, sl].astype(BF16)) * scale
        p = jnp.exp(s - jnp.max(s, axis=-1, keepdims=True))
        o = _dot(p.astype(BF16), v[:, sl].astype(BF16))
        outs.append(o / jnp.sum(p, axis=-1, keepdims=True))
    o_ref[...] = jnp.concatenate(outs, axis=-1)


def _xattn_prompt(qx, mk, mv, n_batch, seq, tq=512):
    t_all = qx.shape[0]
    per = seq // tq
    m = mk.shape[1]
    rows = pl.BlockSpec((tq, X_WIDTH), lambda n, i: (n * per + i, 0))
    mem = pl.BlockSpec((1, m, X_WIDTH), lambda n, i: (n, 0, 0))
    return pl.pallas_call(
        _xattn_prompt_kernel,
        grid=(n_batch, per),
        in_specs=[rows, mem, mem],
        out_specs=rows,
        out_shape=jax.ShapeDtypeStruct((t_all, X_WIDTH), F32),
        input_output_aliases={0: 0},
        compiler_params=_cparams(2),
        name="xattn_prompt",
    )(qx, mk, mv)


def _xattn_sample_kernel(q_ref, k_ref, v_ref, o_ref, *, group, n_mem):
    i = pl.program_id(0)
    scale = X_HEAD_DIM ** -0.5
    for j in range(group):
        n = i * group + j
        q = q_ref[pl.ds(n, 1), :]
        outs = []
        for h in range(X_HEADS):
            rows = pl.ds(h, n_mem, stride=X_HEADS)
            qh = q[:, h * LANES:(h + 1) * LANES]
            s = jnp.sum(k_ref[j, rows, :] * qh, axis=-1, keepdims=True) * scale
            p = jnp.exp(s - jnp.max(s, axis=0, keepdims=True))
            o = jnp.sum(p * v_ref[j, rows, :], axis=0, keepdims=True)
            outs.append(o / jnp.sum(p, axis=0, keepdims=True))
        o_ref[pl.ds(n, 1), :] = jnp.concatenate(outs, axis=-1)


XATTN_SAMPLE_GROUP = 8


def _xattn_sample(o_all, mk, mv):
    n_s, mh, _ = mk.shape
    t_all = o_all.shape[0]
    last = t_all // n_s - 1
    group = XATTN_SAMPLE_GROUP
    rows = pl.BlockSpec((n_s, X_WIDTH), lambda n: (last, 0))
    mem = pl.BlockSpec((group, mh, X_HEAD_DIM), lambda n: (n, 0, 0))
    return pl.pallas_call(
        functools.partial(_xattn_sample_kernel, group=group, n_mem=mh // X_HEADS),
        grid=(n_s // group,),
        in_specs=[rows, mem, mem],
        out_specs=rows,
        out_shape=jax.ShapeDtypeStruct(o_all.shape, F32),
        input_output_aliases={0: 0},
        compiler_params=_cparams(1),
        name="xattn_sample",
    )(o_all, mk, mv)


ROW_TILE_ROWS = D_MODEL // LANES


def _store_row_tiles(ref, x):
    rows = x.shape[0]
    for c in range(ROW_TILE_ROWS):
        ref[pl.ds(c, rows, stride=ROW_TILE_ROWS), :] = x[:, c * LANES:(c + 1) * LANES]


def _load_row_tiles(ref, rows):
    return jnp.concatenate([ref[pl.ds(c, rows, stride=ROW_TILE_ROWS), :]
                            for c in range(ROW_TILE_ROWS)], axis=1)


def _xout_router_kernel(o_ref, h_ref, wo_ref, g3_ref, wr_ref, br_ref,
                        h2_ref, xn_ref, idx_ref, gate_ref):
    h2 = h_ref[...] + _dot(o_ref[...].astype(BF16), wo_ref[...])
    h2_ref[...] = h2
    xn = _rms(h2, g3_ref[...])
    _store_row_tiles(xn_ref, xn)
    logits = lax.dot_general(wr_ref[...], xn, (((1,), (1,)), ((), ())),
                             precision=lax.Precision.HIGHEST,
                             preferred_element_type=F32) + br_ref[...]
    eid = lax.broadcasted_iota(I32, logits.shape, 0)
    work = logits
    vals, idxs = [], []
    for _ in range(TOP_K):
        mx = jnp.max(work, axis=0, keepdims=True)
        ix = jnp.min(jnp.where(work == mx, eid, N_EXPERTS), axis=0, keepdims=True)
        vals.append(mx)
        idxs.append(ix)
        work = jnp.where(eid == ix, -jnp.inf, work)
    ex = [jnp.exp(v - vals[0]) for v in vals]
    tot = ex[0] + ex[1] + ex[2] + ex[3]
    idx_ref[...] = jnp.concatenate(idxs, axis=0)
    gate_ref[...] = jnp.concatenate([e / tot for e in ex], axis=0)


def _xout_router(o_all, h_all, w_ox, g3, w_rt, b_r):
    t_all = h_all.shape[0]
    tm = ROUTER_TILE
    row = lambda i: (i, 0)
    fixed = lambda i: (0, 0)
    colblk = lambda i: (0, i)
    return pl.pallas_call(
        _xout_router_kernel,
        grid=(t_all // tm,),
        in_specs=[pl.BlockSpec((tm, X_WIDTH), row),
                  pl.BlockSpec((tm, D_MODEL), row),
                  pl.BlockSpec((X_WIDTH, D_MODEL), fixed),
                  pl.BlockSpec((1, D_MODEL), fixed),
                  pl.BlockSpec((N_EXPERTS, D_MODEL), fixed),
                  pl.BlockSpec((N_EXPERTS, 1), fixed)],
        out_specs=[pl.BlockSpec((tm, D_MODEL), row),
                   pl.BlockSpec((tm * ROW_TILE_ROWS, LANES), row),
                   pl.BlockSpec((TOP_K, tm), colblk), pl.BlockSpec((TOP_K, tm), colblk)],
        out_shape=[jax.ShapeDtypeStruct((t_all, D_MODEL), F32),
                   jax.ShapeDtypeStruct((t_all * ROW_TILE_ROWS, LANES), F32),
                   jax.ShapeDtypeStruct((TOP_K, t_all), I32),
                   jax.ShapeDtypeStruct((TOP_K, t_all), F32)],
        compiler_params=_cparams(1),
        name="xout_router",
    )(o_all, h_all, w_ox, g3, w_rt, b_r)


def _route_kernel(idx_ref, dest_ref, be_ref, nu_ref, pe_ref, seg_ref, nxt_ref, *, t_all,
                  n_blocks_pad):
    bm = MOE_BLOCK_ROWS
    nt = t_all // LANES
    e_col = lax.broadcasted_iota(I32, (N_EXPERTS, 1), 0)
    hp = lax.Precision.HIGHEST

    def multi_hot(j):
        blk = idx_ref[:, pl.ds(pl.multiple_of(j * LANES, LANES), LANES)]
        mh = jnp.zeros((N_EXPERTS, LANES), F32)
        for k in range(TOP_K):
            mh = mh + (e_col == blk[k:k + 1, :]).astype(F32)
        return blk, mh

    def count_body(j, c):
        _, mh = multi_hot(j)
        return c + jnp.sum(mh, axis=1, keepdims=True)

    counts = lax.fori_loop(0, nt, count_body, jnp.zeros((N_EXPERTS, 1), F32))
    padded = jnp.ceil(counts * (1.0 / bm)) * bm
    tri = (lax.broadcasted_iota(I32, (N_EXPERTS, N_EXPERTS), 1)
           <= lax.broadcasted_iota(I32, (N_EXPERTS, N_EXPERTS), 0)).astype(F32)
    pad_end = jnp.dot(tri, jnp.broadcast_to(padded, (N_EXPERTS, LANES)), precision=hp,
                      preferred_element_type=F32)
    pad_start = pad_end[:, 0:1] - padded
    upper = (lax.broadcasted_iota(I32, (LANES, LANES), 0)
             < lax.broadcasted_iota(I32, (LANES, LANES), 1)).astype(BF16)

    def dest_body(j, carry):
        blk, mh = multi_hot(j)
        rank = carry + _dot(mh.astype(BF16), upper)
        base = pad_start + rank
        for k in range(TOP_K):
            d = jnp.sum(jnp.where(e_col == blk[k:k + 1, :], base, 0.0), axis=0, keepdims=True)
            dest_ref[pl.ds(k, 1), pl.ds(pl.multiple_of(j * LANES, LANES), LANES)] = d.astype(I32)
        return carry + jnp.sum(mh, axis=1, keepdims=True)

    lax.fori_loop(0, nt, dest_body, jnp.zeros((N_EXPERTS, 1), F32))

    b_row = lax.broadcasted_iota(I32, (1, n_blocks_pad), 1).astype(F32) * bm
    be = jnp.sum((pad_end[:, 0:1] <= b_row).astype(F32), axis=0, keepdims=True)
    be = jnp.minimum(be, N_EXPERTS - 1.0)
    be_ref[...] = be.astype(I32)
    e_f = e_col.astype(F32)
    nonempty = counts > 0.0
    seg_ref[...] = jnp.sum(jnp.where((e_f < be) & nonempty, 1.0, 0.0), axis=0,
                           keepdims=True).astype(I32)
    nxt = jnp.min(jnp.where((e_f > be) & nonempty, e_f, float(N_EXPERTS)), axis=0, keepdims=True)
    nxt_ref[...] = jnp.where(nxt < N_EXPERTS, nxt, -1.0).astype(I32)
    nu_ref[...] = (pad_end[N_EXPERTS - 1:N_EXPERTS, :] * (1.0 / bm)).astype(I32)
    lane_row = lax.broadcasted_iota(I32, (1, LANES), 1)
    pe_ref[...] = jnp.sum(jnp.where(e_col == lane_row, pad_end, 0.0), axis=0,
                          keepdims=True).astype(I32)


def _route(idx_t, n_blocks_pad):
    t_all = idx_t.shape[1]
    return pl.pallas_call(
        functools.partial(_route_kernel, t_all=t_all, n_blocks_pad=n_blocks_pad),
        out_shape=[jax.ShapeDtypeStruct((TOP_K, t_all), I32),
                   jax.ShapeDtypeStruct((1, n_blocks_pad), I32),
                   jax.ShapeDtypeStruct((1, LANES), I32),
                   jax.ShapeDtypeStruct((1, LANES), I32),
                   jax.ShapeDtypeStruct((1, n_blocks_pad), I32),
                   jax.ShapeDtypeStruct((1, n_blocks_pad), I32)],
        compiler_params=pltpu.CompilerParams(vmem_limit_bytes=VMEM_LIMIT),
        name="moe_route",
    )(idx_t)


def _dispatch_kernel(dest_sm, pe_sm, nu_sm, x_ref, xs_ref, zbuf, sem, zsem, *, t_all, n_blocks):
    i = pl.program_id(0)
    tm = x_ref.shape[0]
    bm = MOE_BLOCK_ROWS

    @pl.when(i == 0)
    def _():
        zbuf[...] = jnp.zeros_like(zbuf)

        def zero_block(row0):
            return pltpu.make_async_copy(zbuf, xs_ref.at[pl.ds(row0, bm)], zsem)

        ends = [pe_sm[e] for e in range(N_EXPERTS)]
        used = [ends[e] > (ends[e - 1] if e else 0) for e in range(N_EXPERTS)]
        for e in range(N_EXPERTS):
            @pl.when(used[e])
            def _(e=e):
                zero_block(ends[e] - bm).start()

        def tail_start(b, c):
            zero_block(b * bm).start()
            return c

        lax.fori_loop(nu_sm[0], n_blocks, tail_start, 0)
        for e in range(N_EXPERTS):
            @pl.when(used[e])
            def _():
                zero_block(0).wait()

        def tail_wait(b, c):
            zero_block(0).wait()
            return c

        lax.fori_loop(nu_sm[0], n_blocks, tail_wait, 0)

    def start_body(j, c):
        for k in range(TOP_K):
            d = dest_sm[k * t_all + i * tm + j]
            pltpu.make_async_copy(x_ref.at[j], xs_ref.at[d], sem).start(priority=k % 2)
        return c

    lax.fori_loop(0, tm, start_body, 0, unroll=ROW_DMA_UNROLL)
    for k in range(TOP_K):
        pltpu.make_async_copy(x_ref, xs_ref.at[pl.ds(0, tm)], sem).wait()


ROW_DMA_UNROLL = 4


def _dispatch(dest_flat, pad_end, n_used, xn_all, n_blocks):
    t_all = xn_all.shape[0]
    tm = TOKEN_TILE
    bm = MOE_BLOCK_ROWS
    return pl.pallas_call(
        functools.partial(_dispatch_kernel, t_all=t_all, n_blocks=n_blocks),
        grid_spec=pltpu.PrefetchScalarGridSpec(
            num_scalar_prefetch=3,
            grid=(t_all // tm,),
            in_specs=[pl.BlockSpec((tm, ROW_TILE_ROWS, LANES), lambda i, d, pe, nu: (i, 0, 0))],
            out_specs=pl.BlockSpec(memory_space=pl.ANY),
            scratch_shapes=[pltpu.VMEM((bm, ROW_TILE_ROWS, LANES), F32),
                            pltpu.SemaphoreType.DMA, pltpu.SemaphoreType.DMA]),
        out_shape=jax.ShapeDtypeStruct((n_blocks * bm, ROW_TILE_ROWS, LANES), F32),
        compiler_params=_cparams(1),
        name="moe_dispatch",
    )(dest_flat, pad_end, n_used, xn_all)


def _expert_kernel(be_sm, nu_sm, seg_sm, nxt_sm, x_ref, wgu_hbm, bgu_ref, wd_hbm, bd_ref, y_ref,
                   wgu_bf, wd_bf, wgu_f, wd_f, wsems):
    b = pl.program_id(0)
    used = b < nu_sm[0]
    first = jnp.logical_and(
        used, jnp.logical_or(b == 0, be_sm[b] != be_sm[jnp.maximum(b - 1, 0)]))
    slot = seg_sm[b] % 2

    def weight_copies(e, s):
        return [pltpu.make_async_copy(wgu_hbm.at[e], wgu_f.at[s], wsems.at[s]),
                pltpu.make_async_copy(wd_hbm.at[e], wd_f.at[s], wsems.at[s])]

    @pl.when(b == 0)
    def _():
        for cp in weight_copies(be_sm[0], 0):
            cp.start()

    @pl.when(first)
    def _():
        for cp in weight_copies(be_sm[b], slot):
            cp.wait()
        wgu_bf[...] = wgu_f[slot].astype(BF16)
        wd_bf[...] = wd_f[slot].astype(BF16)
        nx = nxt_sm[b]

        @pl.when(nx >= 0)
        def _():
            for cp in weight_copies(nx, 1 - slot):
                cp.start()

    @pl.when(b < nu_sm[0])
    def _():
        x = _load_row_tiles(x_ref, MOE_BLOCK_ROWS).astype(BF16)
        h = _dot(x, wgu_bf[...]) + bgu_ref[0]
        glu = jnp.minimum(h[:, :D_FF], SWIGLU_LIMIT)
        lin = jnp.clip(h[:, D_FF:], -SWIGLU_LIMIT, SWIGLU_LIMIT)
        act = glu * _sigmoid(SWIGLU_ALPHA * glu) * (lin + 1.0)
        _store_row_tiles(y_ref, _dot(act.astype(BF16), wd_bf[...]) + bd_ref[0])

    @pl.when(b >= nu_sm[0])
    def _():
        y_ref[...] = jnp.zeros_like(y_ref)


def _experts(block_e, n_used, seg, nxt, xs, w_gu, b_gu, w_down, b_down):
    cap = xs.shape[0] // ROW_TILE_ROWS
    bm = MOE_BLOCK_ROWS
    bias = lambda b, be, nu, sg, nx: (be[b], 0, 0)
    rows = lambda b, be, nu, sg, nx: (b, 0)
    hbm = pl.BlockSpec(memory_space=pl.ANY)
    return pl.pallas_call(
        _expert_kernel,
        grid_spec=pltpu.PrefetchScalarGridSpec(
            num_scalar_prefetch=4,
            grid=(cap // bm,),
            in_specs=[pl.BlockSpec((bm * ROW_TILE_ROWS, LANES), rows),
                      hbm,
                      pl.BlockSpec((1, 1, 2 * D_FF), bias),
                      hbm,
                      pl.BlockSpec((1, 1, D_MODEL), bias)],
            out_specs=pl.BlockSpec((bm * ROW_TILE_ROWS, LANES), rows),
            scratch_shapes=[pltpu.VMEM((D_MODEL, 2 * D_FF), BF16),
                            pltpu.VMEM((D_FF, D_MODEL), BF16),
                            pltpu.VMEM((2, D_MODEL, 2 * D_FF), F32),
                            pltpu.VMEM((2, D_FF, D_MODEL), F32),
                            pltpu.SemaphoreType.DMA((2,))]),
        out_shape=jax.ShapeDtypeStruct((cap * ROW_TILE_ROWS, LANES), F32),
        compiler_params=_cparams(1, VMEM_LIMIT_EXPERTS),
        name="moe_experts",
    )(block_e, n_used, seg, nxt, xs, w_gu, b_gu, w_down, b_down)


def _combine_kernel(dest_sm, yb_ref, h_ref, g_ref, op_ref, os_ref, buf, sems, *, t_all):
    i = pl.program_id(0)
    last = pl.num_programs(0) - 1
    tm = h_ref.shape[0]

    def gather(step, slot):
        def start_body(j, c):
            for k in range(TOP_K):
                d = dest_sm[k * t_all + step * tm + j]
                tile = pl.ds(pl.multiple_of(j * ROW_TILE_ROWS, ROW_TILE_ROWS), ROW_TILE_ROWS)
                pltpu.make_async_copy(yb_ref.at[d], buf.at[slot, k, tile],
                                      sems.at[slot]).start(priority=k % 2)
            return c

        lax.fori_loop(0, tm, start_body, 0, unroll=ROW_DMA_UNROLL)

    @pl.when(i == 0)
    def _():
        gather(0, 0)

    @pl.when(i < last)
    def _():
        gather(i + 1, (i + 1) % 2)

    slot = i % 2
    for k in range(TOP_K):
        pltpu.make_async_copy(buf.at[slot, k], buf.at[slot, k], sems.at[slot]).wait()

    g = g_ref[...]
    y = h_ref[...]
    for k in range(TOP_K):
        y = y + g[:, k:k + 1] * _load_row_tiles(buf.at[slot, k], tm)

    @pl.when(i < last)
    def _():
        op_ref[...] = y

    @pl.when(i == last)
    def _():
        os_ref[...] = y


def _combine(dest_flat, yb, h2_all, gates, n_prompt_rows):
    t_all = h2_all.shape[0]
    tm = ROW_TILE
    n_p_tiles = n_prompt_rows // tm
    assert t_all == n_prompt_rows + tm
    return pl.pallas_call(
        functools.partial(_combine_kernel, t_all=t_all),
        grid_spec=pltpu.PrefetchScalarGridSpec(
            num_scalar_prefetch=1,
            grid=(t_all // tm,),
            in_specs=[pl.BlockSpec(memory_space=pl.ANY),
                      pl.BlockSpec((tm, D_MODEL), lambda i, d: (i, 0)),
                      pl.BlockSpec((tm, TOP_K), lambda i, d: (i, 0))],
            out_specs=[pl.BlockSpec((tm, D_MODEL), lambda i, d: (jnp.minimum(i, n_p_tiles - 1), 0)),
                       pl.BlockSpec((tm, D_MODEL), lambda i, d: (0, 0))],
            scratch_shapes=[pltpu.VMEM((2, TOP_K, tm * ROW_TILE_ROWS, LANES), F32),
                            pltpu.SemaphoreType.DMA((2,))]),
        out_shape=[jax.ShapeDtypeStruct((n_prompt_rows, D_MODEL), F32),
                   jax.ShapeDtypeStruct((tm, D_MODEL), F32)],
        compiler_params=_cparams(1),
        name="moe_combine",
    )(dest_flat, yb, h2_all, gates)


def _rope_tables(pos):
    half = HEAD_DIM // 2
    inv = jnp.exp(-math.log(ROPE_THETA) * jnp.arange(half, dtype=F32) / half)
    ang = pos.astype(F32)[:, None] * inv[None, :]
    cos, sin = jnp.cos(ang), jnp.sin(ang)
    cos_t = jnp.concatenate([cos, cos] * HEADS_PER_TILE, axis=-1)
    sin_t = jnp.concatenate([-sin, sin] * HEADS_PER_TILE, axis=-1)
    return cos_t, sin_t


def _block_diag_ones(n, blk):
    r = jnp.arange(n) // blk
    return (r[:, None] == r[None, :]).astype(BF16)


def _layer(x_prompt, x_sample, state_ret, cache_win_k, cache_win_v, cache_mem_k, cache_mem_v,
           mem_prompt, g_norm1, w_in, g_att_q, g_att_k, g_ret_gn, w_out, g_norm2, g_mem,
           w_q_x, w_kv_x, g_x_q, g_x_k, w_o_x, g_norm3, w_router, b_router, w_gu, b_gu,
           w_down, b_down):
    n_b, seq, _ = x_prompt.shape
    n_s = x_sample.shape[0]
    assert x_sample.shape[1] == 1 and n_s == ROW_TILE
    t_p = n_b * seq
    t_all = t_p + n_s
    assert t_all % TOKEN_TILE == 0 and t_all % ROUTER_TILE == 0 and t_p % ROW_TILE == 0

    assert seq % PROMPT_TILE == 0
    x_p = x_prompt.reshape(t_p, D_MODEL)
    x_s = x_sample.reshape(n_s, D_MODEL)
    cos_p, sin_p = _rope_tables(jnp.arange(seq, dtype=jnp.int32))
    cos_s, sin_s = _rope_tables(jnp.full((n_s,), PAST_LEN, jnp.int32))
    log_g = jnp.log1p(-jnp.exp2(-5.0 - jnp.arange(RET_HEADS, dtype=F32)))
    lg_row = jnp.repeat(log_g, HEAD_DIM)[None, :]
    gn_row = g_ret_gn.reshape(1, RET_WIDTH)
    gn_col = g_ret_gn.reshape(RET_WIDTH, 1)
    gq = jnp.tile(g_att_q.reshape(1, HEAD_DIM), (1, ATT_HEADS))
    gk = jnp.tile(g_att_k.reshape(1, HEAD_DIM), (1, ATT_HEADS))
    seg = _block_diag_ones(ATT_WIDTH, HEAD_DIM)

    g1 = g_norm1.reshape(1, D_MODEL)
    w_in_bf = w_in.astype(BF16)
    proj = _in_proj(x_p, g1, w_in_bf, cos_p, sin_p, gq, gk, seg, t_all, PROMPT_TILE, 0)
    rq, rk, rv, rg, aq, ak, av = _in_proj(x_s, g1, w_in_bf, cos_s, sin_s, gq, gk, seg, t_all, n_s,
                                          t_p, prev=proj)

    ret_n, st_p = _ret_prompt(rq, rk, rv, rg, gn_row, lg_row, n_b, seq)
    att_o, wk_p, wv_p = _dil_prompt(aq, ak, av, n_b, seq)
    state_t = jnp.transpose(state_ret, (1, 2, 3, 0))
    ret_n, st_s = _ret_sample(rq, rk, rv, ret_n, gn_col, lg_row, state_t)
    ck = jnp.transpose(cache_win_k, (0, 2, 3, 1))
    cv = jnp.transpose(cache_win_v, (0, 2, 3, 1))
    wk_s, wv_s, att_o = _win_sample(att_o, ak, av, ck, cv)

    mix_w = (w_out.astype(BF16), g_norm2.reshape(1, D_MODEL), w_q_x.astype(BF16),
             g_x_q.reshape(1, X_HEAD_DIM))
    mixed = _mix_out(ret_n, att_o, x_p, *mix_w, PROMPT_TILE, 0)
    h_all, qx = _mix_out(ret_n, att_o, x_s, *mix_w, n_s, t_p, prev=mixed)

    mk_p, mv_p = _mem_kv(mem_prompt, g_mem.reshape(1, D_MODEL), w_kv_x.astype(BF16),
                         g_x_k.reshape(1, X_HEAD_DIM))
    o_all = _xattn_prompt(qx, mk_p, mv_p, n_b, seq)
    n_mem = cache_mem_k.shape[1]
    o_all = _xattn_sample(o_all, cache_mem_k.reshape(n_s, n_mem * X_HEADS, X_HEAD_DIM),
                          cache_mem_v.reshape(n_s, n_mem * X_HEADS, X_HEAD_DIM))

    h2_all, xn_all, idx_t, gate_t = _xout_router(
        o_all, h_all, w_o_x.astype(BF16), g_norm3.reshape(1, D_MODEL),
        jnp.transpose(w_router), b_router.reshape(N_EXPERTS, 1))

    bm = MOE_BLOCK_ROWS
    n_blocks = -(-(t_all * TOP_K) // bm) + N_EXPERTS
    n_blocks_pad = -(-n_blocks // LANES) * LANES
    dest_t, be, nu, pe, seg_b, nxt_b = _route(idx_t, n_blocks_pad)
    dest_flat = dest_t.reshape(TOP_K * t_all)
    cap = n_blocks * bm
    xs = _dispatch(dest_flat, pe.reshape(LANES), nu.reshape(LANES),
                   xn_all.reshape(t_all, ROW_TILE_ROWS, LANES), n_blocks)
    yb = _experts(be.reshape(n_blocks_pad), nu.reshape(LANES), seg_b.reshape(n_blocks_pad),
                  nxt_b.reshape(n_blocks_pad), xs.reshape(cap * ROW_TILE_ROWS, LANES),
                  w_gu, b_gu.reshape(N_EXPERTS, 1, 2 * D_FF), w_down,
                  b_down.reshape(N_EXPERTS, 1, D_MODEL))
    y_p, y_s = _combine(dest_flat, yb.reshape(cap, ROW_TILE_ROWS, LANES), h2_all,
                        jnp.transpose(gate_t), t_p)

    st_p = jnp.stack([st_p[:, :, :HEAD_DIM, :HEAD_DIM], st_p[:, :, HEAD_DIM:, HEAD_DIM:]], axis=2)
    st_p = st_p.reshape(n_b, RET_HEADS, HEAD_DIM, HEAD_DIM)
    return (y_p.reshape(n_b, seq, D_MODEL),
            y_s.reshape(n_s, 1, D_MODEL),
            st_p,
            jnp.transpose(st_s, (3, 0, 1, 2)),
            jnp.transpose(wk_p, (0, 3, 1, 2)),
            jnp.transpose(wv_p, (0, 3, 1, 2)),
            jnp.transpose(wk_s, (0, 3, 1, 2)),
            jnp.transpose(wv_s, (0, 3, 1, 2)),
            mk_p.reshape(n_b, n_mem, X_HEADS, X_HEAD_DIM),
            mv_p.reshape(n_b, n_mem, X_HEADS, X_HEAD_DIM))


def kernel(x_prompt, x_sample, state_ret, cache_win_k, cache_win_v, cache_mem_k, cache_mem_v,
           mem_prompt, g_norm1, w_in, g_att_q, g_att_k, g_ret_gn, w_out, g_norm2, g_mem,
           w_q_x, w_kv_x, g_x_q, g_x_k, w_o_x, g_norm3, w_router, b_router, w_gu, b_gu,
           w_down, b_down):
    assert state_ret.shape[0] == 1, "single-layer trunk"
    outs = _layer(x_prompt, x_sample, state_ret[0], cache_win_k[0], cache_win_v[0],
                  cache_mem_k[0], cache_mem_v[0], mem_prompt, g_norm1[0], w_in[0], g_att_q[0],
                  g_att_k[0], g_ret_gn[0], w_out[0], g_norm2[0], g_mem[0], w_q_x[0], w_kv_x[0],
                  g_x_q[0], g_x_k[0], w_o_x[0], g_norm3[0], w_router[0], b_router[0], w_gu[0],
                  b_gu[0], w_down[0], b_down[0])
    y_p, y_s = outs[0], outs[1]
    return (y_p, y_s) + tuple(o[None] for o in outs[2:])
```

```python
import functools
import math

import jax
import jax.numpy as jnp
from jax import lax
from jax.experimental import pallas as pl
from jax.experimental.pallas import tpu as pltpu

F32 = jnp.float32
BF16 = jnp.bfloat16
I32 = jnp.int32

D_MODEL = 1024
HEAD_DIM = 64
RET_HEADS = 8
ATT_HEADS = 8
RET_WIDTH = RET_HEADS * HEAD_DIM
ATT_WIDTH = ATT_HEADS * HEAD_DIM
IN_COLS = 4 * RET_WIDTH + 3 * ATT_WIDTH
RET_CHUNK = 128
DIL_PATTERNS = ((128, 1), (512, 4), (2048, 16))
DIL_BLOCK = 128
MAX_WINDOW = 2048
X_HEADS = 4
X_HEAD_DIM = 128
X_WIDTH = X_HEADS * X_HEAD_DIM
N_EXPERTS = 32
TOP_K = 4
D_FF = D_MODEL
SWIGLU_LIMIT = 7.0
SWIGLU_ALPHA = 1.702
ROPE_THETA = 10000.0
EPS = 1e-6
PAST_LEN = 8192

LANES = 128
SUBLANES = 8
HEADS_PER_TILE = LANES // HEAD_DIM

TOKEN_TILE = 384
ROUTER_TILE = 384
PROMPT_TILE = 512
ROW_TILE = 128
MOE_BLOCK_ROWS = 512
NEG_BIG = -1e30
VMEM_LIMIT = 48 * 1024 * 1024
VMEM_LIMIT_EXPERTS = 58 * 1024 * 1024


def _cparams(n_axes, vmem_limit=VMEM_LIMIT):
    return pltpu.CompilerParams(
        dimension_semantics=("arbitrary",) * n_axes, vmem_limit_bytes=vmem_limit)


def _rms(x, g):
    return x * lax.rsqrt(jnp.mean(x * x, axis=-1, keepdims=True) + EPS) * g


def _dot(a, b):
    return jnp.dot(a, b, preferred_element_type=F32)


def _dot_nt(a, b):
    return lax.dot_general(a, b, (((1,), (1,)), ((), ())), preferred_element_type=F32)


def _dot_tn(a, b):
    return lax.dot_general(a, b, (((0,), (0,)), ((), ())), preferred_element_type=F32)


def _sigmoid(x):
    return 1.0 / (1.0 + jnp.exp(-x))


N_PROJ_OUT = 7


def _in_proj_kernel(x_ref, g1_ref, w_ref, gq_ref, gk_ref, seg_ref, cos_ref, sin_ref, *rest):
    rq_ref, rk_ref, rv_ref, rg_ref, aq_ref, ak_ref, av_ref = rest[-N_PROJ_OUT:]
    xn = _rms(x_ref[...], g1_ref[...]).astype(BF16)
    proj = _dot(xn, w_ref[...])
    reps = RET_WIDTH // LANES
    cos = jnp.concatenate([cos_ref[...]] * reps, axis=-1)
    sin = jnp.concatenate([sin_ref[...]] * reps, axis=-1)
    lane = lax.broadcasted_iota(I32, (1, RET_WIDTH), 1)
    first_half = (lane % HEAD_DIM) < (HEAD_DIM // 2)
    seg = seg_ref[...]

    def rope(t):
        partner = jnp.where(first_half,
                            pltpu.roll(t, RET_WIDTH - HEAD_DIM // 2, 1),
                            pltpu.roll(t, HEAD_DIM // 2, 1))
        return t * cos + partner * sin

    def head_norm(t, g):
        ssum = _dot((t * t).astype(BF16), seg)
        return t * lax.rsqrt(ssum * (1.0 / HEAD_DIM) + EPS) * g

    w = RET_WIDTH
    rq_ref[...] = rope(proj[:, 0:w])
    rk_ref[...] = rope(proj[:, w:2 * w]) * (HEAD_DIM ** -0.5)
    rv_ref[...] = proj[:, 2 * w:3 * w]
    rg_ref[...] = proj[:, 3 * w:4 * w]
    aq_ref[...] = rope(head_norm(proj[:, 4 * w:5 * w], gq_ref[...]))
    ak_ref[...] = rope(head_norm(proj[:, 5 * w:6 * w], gk_ref[...]))
    av_ref[...] = proj[:, 6 * w:7 * w]


def _group_rows_call(kernel, name, shared_ins, row_ins, fixed_ins, tables, out_widths, t_all, tm,
                     row_start, prev):
    rows = row_ins[0].shape[0]
    off = row_start // tm
    assert rows % tm == 0 and row_start % tm == 0
    n_steps = rows // tm
    extra = 1 if (prev is None and row_start + rows < t_all) else 0
    src = lambda i: jnp.minimum(i, n_steps - 1)
    fixed = lambda i: (0, 0)
    in_specs = [pl.BlockSpec((tm, a.shape[1]), lambda i: (src(i) + off, 0)) for a in shared_ins]
    in_specs += [pl.BlockSpec((tm, a.shape[1]), lambda i: (src(i), 0)) for a in row_ins]
    in_specs += [pl.BlockSpec(a.shape, fixed) for a in fixed_ins]
    for t in tables:
        period = t.shape[0] // tm
        in_specs.append(pl.BlockSpec((tm, t.shape[1]),
                                     lambda i, period=period: (src(i) % period, 0)))
    n_in = len(in_specs)
    aliases = {}
    if prev is not None:
        in_specs += [pl.BlockSpec(memory_space=pl.ANY)] * len(prev)
        aliases = {n_in + j: j for j in range(len(prev))}
    return pl.pallas_call(
        kernel,
        grid=(n_steps + extra,),
        in_specs=in_specs,
        out_specs=[pl.BlockSpec((tm, w), lambda i: (i + off, 0)) for w in out_widths],
        out_shape=[jax.ShapeDtypeStruct((t_all, w), F32) for w in out_widths],
        input_output_aliases=aliases,
        compiler_params=_cparams(1),
        name=name,
    )(*shared_ins, *row_ins, *fixed_ins, *tables, *(prev or ()))


def _in_proj(x, g1, w_in, cos_t, sin_t, gq, gk, seg, t_all, tm, row_start, prev=None):
    return _group_rows_call(_in_proj_kernel, "in_proj", [], [x], [g1, w_in, gq, gk, seg],
                            [cos_t, sin_t], [RET_WIDTH] * N_PROJ_OUT, t_all, tm, row_start, prev)


def _group_norm_gate(o, mask_a, gn, gate):
    inv = 1.0 / HEAD_DIM
    sa = jnp.sum(jnp.where(mask_a, o, 0.0), axis=-1, keepdims=True)
    sb = jnp.sum(jnp.where(mask_a, 0.0, o), axis=-1, keepdims=True)
    cen = o - jnp.where(mask_a, sa, sb) * inv
    c2 = cen * cen
    va = jnp.sum(jnp.where(mask_a, c2, 0.0), axis=-1, keepdims=True)
    vb = jnp.sum(jnp.where(mask_a, 0.0, c2), axis=-1, keepdims=True)
    var = jnp.where(mask_a, va, vb) * inv
    return cen * lax.rsqrt(var + EPS) * gn * (gate * _sigmoid(gate))


RET_CHUNKS_PER_STEP = 4


def _ret_prompt_kernel(q_ref, k_ref, v_ref, g_ref, gn_ref, lg_ref, o_ref, st_ref,
                       state, dec, qdec, kdec):
    n = pl.program_id(0)
    c = pl.program_id(1)
    ch = RET_CHUNK
    n_pairs = RET_WIDTH // LANES
    lane = lax.broadcasted_iota(I32, (1, LANES), 1)
    mask_a = lane < HEAD_DIM
    row_i = lax.broadcasted_iota(I32, (ch, 1), 0)

    @pl.when(jnp.logical_and(n == 0, c == 0))
    def _():
        row = row_i.astype(F32)
        col = lax.broadcasted_iota(I32, (1, ch), 1).astype(F32)
        diff = row - col
        causal = diff >= 0.0
        dpos = jnp.maximum(diff, 0.0)
        lg = lg_ref[...]
        for h in range(RET_HEADS):
            dec[h] = jnp.where(causal, jnp.exp(dpos * lg[:, h * HEAD_DIM:h * HEAD_DIM + 1]), 0.0)
        qdec[...] = jnp.exp((row + 1.0) * lg)
        kdec[...] = jnp.exp((ch - 1.0 - row) * lg)

    @pl.when(c == 0)
    def _():
        state[...] = jnp.zeros_like(state)

    same_head = (row_i // HEAD_DIM) == (lane // HEAD_DIM)
    cdec = jnp.exp(ch * lg_ref[...])
    for p in range(n_pairs):
        sl = slice(p * LANES, (p + 1) * LANES)
        st = state[p]
        for cc in range(RET_CHUNKS_PER_STEP):
            rows = slice(cc * ch, (cc + 1) * ch)
            q = q_ref[rows, sl]
            k = k_ref[rows, sl]
            kb = k.astype(BF16)
            vb = v_ref[rows, sl].astype(BF16)
            lhs = jnp.concatenate([jnp.where(mask_a, q, 0.0), jnp.where(mask_a, 0.0, q)],
                                  axis=0).astype(BF16)
            s = _dot_nt(lhs, kb)
            pr = (s * jnp.concatenate([dec[2 * p], dec[2 * p + 1]], axis=0)).astype(BF16)
            o2 = _dot(pr, vb)
            o_intra = jnp.where(mask_a, o2[:ch], o2[ch:])
            o = o_intra + _dot(q.astype(BF16), st.astype(BF16)) * qdec[:, sl]
            upd = _dot_tn((k * kdec[:, sl]).astype(BF16), vb)
            st = cdec[:, sl] * st + jnp.where(same_head, upd, 0.0)
            o_ref[rows, sl] = _group_norm_gate(o, mask_a, gn_ref[:, sl], g_ref[rows, sl])
        state[p] = st

    @pl.when(c == pl.num_programs(1) - 1)
    def _():
        st_ref[0] = state[...]


def _ret_prompt(rq, rk, rv, rg, gn_row, lg_row, n_batch, seq):
    t_all = rq.shape[0]
    n_pairs = RET_WIDTH // LANES
    step_rows = RET_CHUNK * RET_CHUNKS_PER_STEP
    assert seq % step_rows == 0
    n_steps = seq // step_rows
    blk = pl.BlockSpec((step_rows, RET_WIDTH), lambda n, c: (n * n_steps + c, 0))
    lane_row = pl.BlockSpec((1, RET_WIDTH), lambda n, c: (0, 0))
    return pl.pallas_call(
        _ret_prompt_kernel,
        grid=(n_batch, n_steps),
        in_specs=[blk, blk, blk, blk, lane_row, lane_row],
        out_specs=[blk, pl.BlockSpec((1, n_pairs, LANES, LANES), lambda n, c: (n, 0, 0, 0))],
        out_shape=[jax.ShapeDtypeStruct((t_all, RET_WIDTH), F32),
                   jax.ShapeDtypeStruct((n_batch, n_pairs, LANES, LANES), F32)],
        scratch_shapes=[pltpu.VMEM((n_pairs, LANES, LANES), F32),
                        pltpu.VMEM((RET_HEADS, RET_CHUNK, RET_CHUNK), F32),
                        pltpu.VMEM((RET_CHUNK, RET_WIDTH), F32),
                        pltpu.VMEM((RET_CHUNK, RET_WIDTH), F32)],
        input_output_aliases={3: 0},
        compiler_params=_cparams(2),
        name="ret_prompt",
    )(rq, rk, rv, rg, gn_row, lg_row)


DIL_UNROLL = 16
DIL_MERGE_ROWS = 256


def _dil_prompt_kernel(q_ref, k_ref, v_ref, o_ref, wk_ref, wv_ref, acc, m_s, l_s, *, seq, wp):
    b = DIL_BLOCK
    lane = lax.broadcasted_iota(I32, (1, LANES), 1)
    mask_a = lane < HEAD_DIM
    qi = lax.broadcasted_iota(I32, (2 * b, 1), 0) % b
    kk = lax.broadcasted_iota(I32, (1, 2 * b), 1)
    dist = qi + b - kk
    scale = HEAD_DIM ** -0.5

    for pat, (window, dil) in enumerate(DIL_PATTERNS):
        steps = window // dil
        band = (dist >= 0) & (dist <= steps)
        nb = seq // (b * dil)

        def block(idx, dil=dil, nb=nb, band=band):
            r = idx // nb
            i = idx % nb
            rows_q = pl.ds(r + i * (b * dil), b, stride=dil)
            rows_p = pl.ds(r + jnp.maximum(i - 1, 0) * (b * dil), b, stride=dil)
            q = q_ref[rows_q, :]
            kc = jnp.concatenate([k_ref[rows_p, :], k_ref[rows_q, :]], axis=0).astype(BF16)
            vc = jnp.concatenate([v_ref[rows_p, :], v_ref[rows_q, :]], axis=0).astype(BF16)
            lhs = jnp.concatenate([jnp.where(mask_a, q, 0.0), jnp.where(mask_a, 0.0, q)],
                                  axis=0).astype(BF16)
            s = _dot_nt(lhs, kc) * scale
            valid = band & ((kk >= b) | (jnp.broadcast_to(i, kk.shape) > 0))
            s = jnp.where(valid, s, NEG_BIG)
            m_blk = jnp.max(s, axis=-1, keepdims=True)
            p = jnp.exp(s - m_blk)
            l_blk = jnp.sum(p, axis=-1, keepdims=True)
            pv = _dot(p.astype(BF16), vc)
            return rows_q, m_blk, l_blk, pv

        def emit(rows_q, m_blk, l_blk, pv, pat=pat):
            acc[pat, rows_q, :] = jnp.where(mask_a, pv[:b], pv[b:])
            m_s[pat, rows_q, :] = jnp.where(mask_a, m_blk[:b], m_blk[b:])
            l_s[pat, rows_q, :] = jnp.where(mask_a, l_blk[:b], l_blk[b:])

        def body(it, carry, block=block, emit=emit):
            parts = [block(it * DIL_UNROLL + u) for u in range(DIL_UNROLL)]
            for part in parts:
                emit(*part)
            return carry

        lax.fori_loop(0, (dil * nb) // DIL_UNROLL, body, 0)

    n_pat = len(DIL_PATTERNS)

    def merge_body(c, carry):
        rows = pl.ds(pl.multiple_of(c * DIL_MERGE_ROWS, DIL_MERGE_ROWS), DIL_MERGE_ROWS)
        ms = [m_s[pat, rows, :] for pat in range(n_pat)]
        m = functools.reduce(jnp.maximum, ms)
        num = jnp.zeros((DIL_MERGE_ROWS, LANES), F32)
        den = jnp.zeros((DIL_MERGE_ROWS, LANES), F32)
        for pat in range(n_pat):
            w = jnp.exp(ms[pat] - m)
            num = num + w * acc[pat, rows, :]
            den = den + w * l_s[pat, rows, :]
        o_ref[rows, :] = num / den
        return carry

    lax.fori_loop(0, seq // DIL_MERGE_ROWS, merge_body, 0)

    tchunk = 4 * LANES
    for j in range(wp // tchunk):
        rows = pl.ds(seq - wp + j * tchunk, tchunk)
        cols = slice(j * tchunk, (j + 1) * tchunk)
        kt = k_ref[rows, :].T
        vt = v_ref[rows, :].T
        for hh in range(HEADS_PER_TILE):
            wk_ref[0, hh, :, cols] = kt[hh * HEAD_DIM:(hh + 1) * HEAD_DIM]
            wv_ref[0, hh, :, cols] = vt[hh * HEAD_DIM:(hh + 1) * HEAD_DIM]


def _dil_prompt(aq, ak, av, n_batch, seq):
    t_all = aq.shape[0]
    n_pairs = ATT_WIDTH // LANES
    wp = min(MAX_WINDOW, seq)
    blk = pl.BlockSpec((seq, LANES), lambda n, p: (n, p))
    wblk = pl.BlockSpec((1, HEADS_PER_TILE, HEAD_DIM, wp), lambda n, p: (n, p, 0, 0))
    wshape = jax.ShapeDtypeStruct((n_batch, ATT_HEADS, HEAD_DIM, wp), F32)
    return pl.pallas_call(
        functools.partial(_dil_prompt_kernel, seq=seq, wp=wp),
        grid=(n_batch, n_pairs),
        in_specs=[blk, blk, blk],
        out_specs=[blk, wblk, wblk],
        out_shape=[jax.ShapeDtypeStruct((t_all, ATT_WIDTH), F32), wshape, wshape],
        scratch_shapes=[pltpu.VMEM((len(DIL_PATTERNS), seq, LANES), F32)] * 3,
        input_output_aliases={0: 0},
        compiler_params=_cparams(2),
        name="dil_prompt",
    )(aq, ak, av)


def _ret_sample_kernel(q_ref, k_ref, v_ref, g_ref, gn_ref, lg_ref, st_ref,
                       o_ref, nst_ref, qt, kt):
    qt[...] = q_ref[...].T
    kt[...] = k_ref[...].T
    vt = v_ref[...].T
    gt = g_ref[...].T
    lg = lg_ref[...]
    outs = []
    for hh in range(HEADS_PER_TILE):
        lo = hh * HEAD_DIM
        gdec = jnp.exp(lg[:, lo:lo + 1])
        vth = vt[lo:lo + HEAD_DIM, :]

        def body(d, o, hh=hh, lo=lo, gdec=gdec, vth=vth):
            new = gdec * st_ref[hh, d] + kt[pl.ds(lo + d, 1), :] * vth
            nst_ref[hh, d] = new
            return o + qt[pl.ds(lo + d, 1), :] * new

        o = lax.fori_loop(0, HEAD_DIM, body, jnp.zeros_like(vth), unroll=8)
        mu = jnp.mean(o, axis=0, keepdims=True)
        cen = o - mu
        var = jnp.mean(cen * cen, axis=0, keepdims=True)
        gate = gt[lo:lo + HEAD_DIM, :]
        outs.append(cen * lax.rsqrt(var + EPS) * gn_ref[lo:lo + HEAD_DIM, :] * (gate * _sigmoid(gate)))
    o_ref[...] = jnp.concatenate(outs, axis=0).T


def _ret_sample(rq, rk, rv, ret_n, gn_col, lg_row, state_t):
    t_all = rq.shape[0]
    n_s = state_t.shape[-1]
    n_pairs = RET_WIDTH // LANES
    last = t_all // n_s - 1
    blk = pl.BlockSpec((n_s, LANES), lambda p: (last, p))
    st_blk = pl.BlockSpec((HEADS_PER_TILE, HEAD_DIM, HEAD_DIM, n_s), lambda p: (p, 0, 0, 0))
    return pl.pallas_call(
        _ret_sample_kernel,
        grid=(n_pairs,),
        in_specs=[blk, blk, blk, blk,
                  pl.BlockSpec((LANES, 1), lambda p: (p, 0)),
                  pl.BlockSpec((1, LANES), lambda p: (0, p)),
                  st_blk],
        out_specs=[blk, st_blk],
        out_shape=[jax.ShapeDtypeStruct(ret_n.shape, F32),
                   jax.ShapeDtypeStruct(state_t.shape, F32)],
        scratch_shapes=[pltpu.VMEM((LANES, n_s), F32), pltpu.VMEM((LANES, n_s), F32)],
        input_output_aliases={3: 0},
        compiler_params=_cparams(1),
        name="ret_sample",
    )(rq, rk, rv, ret_n, gn_col, lg_row, state_t)


def _win_sample_kernel(aq_ref, akn_ref, avn_ref, kc_ref, vc_ref,
                       ko_ref, vo_ref, att_ref, qt, kt, vt, acct, *, win):
    n = pl.program_id(0)
    n_s = qt.shape[1]

    @pl.when(n == 0)
    def _():
        qt[...] = aq_ref[...].T
        kt[...] = akn_ref[...].T
        vt[...] = avn_ref[...].T
        acct[...] = jnp.zeros_like(acct)

    onehot = (lax.broadcasted_iota(I32, (n_s, LANES), 0) == n).astype(F32)
    hp = lax.Precision.HIGHEST
    qb = jnp.dot(qt[...], onehot, precision=hp, preferred_element_type=F32)
    kb = jnp.dot(kt[...], onehot, precision=hp, preferred_element_type=F32)
    vb = jnp.dot(vt[...], onehot, precision=hp, preferred_element_type=F32)

    w_pos = lax.broadcasted_iota(I32, (1, win), 1)
    back = win - w_pos
    mult = jnp.zeros((1, win), F32)
    for window, dil in DIL_PATTERNS:
        mult = mult + ((back <= window) & (back % dil == 0)).astype(F32)
    valid = mult > 0.0
    is_last = w_pos == win - 1
    reps = win // LANES
    scale = HEAD_DIM ** -0.5

    s_rows, s0_rows = [], []
    for h in range(ATT_HEADS):
        lo = h * HEAD_DIM
        k_t = kc_ref[0, h]
        qh = qb[lo:lo + HEAD_DIM, :]
        kh = kb[lo:lo + HEAD_DIM, :]
        s_rows.append(jnp.sum(k_t * jnp.concatenate([qh] * reps, axis=1), axis=0, keepdims=True))
        s0_rows.append(jnp.sum(qh * kh, axis=0, keepdims=True)[:, 0:1])
        ko_ref[0, h] = jnp.where(is_last, jnp.concatenate([kh] * reps, axis=1),
                                 pltpu.roll(k_t, win - 1, 1))
    s = jnp.concatenate(s_rows, axis=0) * scale
    s0 = jnp.concatenate(s0_rows, axis=0) * scale
    m = jnp.maximum(jnp.max(jnp.where(valid, s, NEG_BIG), axis=-1, keepdims=True), s0)
    e = jnp.where(valid, jnp.exp(s - m), 0.0) * mult
    e0 = len(DIL_PATTERNS) * jnp.exp(s0 - m)
    denom = jnp.sum(e, axis=-1, keepdims=True) + e0

    cols = []
    for h in range(ATT_HEADS):
        lo = h * HEAD_DIM
        v_t = vc_ref[0, h]
        vh = vb[lo:lo + HEAD_DIM, :]
        num = jnp.sum(v_t * e[h:h + 1, :], axis=1, keepdims=True) + e0[h:h + 1, :] * vh[:, 0:1]
        cols.append(num / denom[h:h + 1, :])
        vo_ref[0, h] = jnp.where(is_last, jnp.concatenate([vh] * reps, axis=1),
                                 pltpu.roll(v_t, win - 1, 1))
    o_col = jnp.concatenate(cols, axis=0)
    lane_n = lax.broadcasted_iota(I32, (1, n_s), 1) == n
    acct[...] = jnp.where(lane_n, o_col, acct[...])

    @pl.when(n == pl.num_programs(0) - 1)
    def _():
        att_ref[...] = acct[...].T


def _win_sample(att_o, ak, av, cache_k, cache_v):
    n_s, n_h, hd, win = cache_k.shape
    assert win >= max(w for w, _ in DIL_PATTERNS)
    t_all = att_o.shape[0]
    last = t_all // n_s - 1
    rows = pl.BlockSpec((n_s, ATT_WIDTH), lambda n: (last, 0))
    cblk = pl.BlockSpec((1, n_h, hd, win), lambda n: (n, 0, 0, 0))
    cshape = jax.ShapeDtypeStruct(cache_k.shape, F32)
    return pl.pallas_call(
        functools.partial(_win_sample_kernel, win=win),
        grid=(n_s,),
        in_specs=[rows, rows, rows, cblk, cblk],
        out_specs=[cblk, cblk, rows],
        out_shape=[cshape, cshape, jax.ShapeDtypeStruct(att_o.shape, F32)],
        scratch_shapes=[pltpu.VMEM((ATT_WIDTH, n_s), F32)] * 4,
        input_output_aliases={0: 2},
        compiler_params=_cparams(1),
        name="win_sample",
    )(att_o, ak, av, cache_k, cache_v)


def _lane_tile_norm(t, g, n_tiles):
    outs = []
    for h in range(n_tiles):
        outs.append(_rms(t[:, h * LANES:(h + 1) * LANES], g))
    return jnp.concatenate(outs, axis=-1)


def _mem_kv_kernel(mem_ref, gm_ref, w_ref, gk_ref, k_ref, v_ref):
    xn = _rms(mem_ref[0], gm_ref[...]).astype(BF16)
    kv = _dot(xn, w_ref[...])
    k_ref[0] = _lane_tile_norm(kv[:, :X_WIDTH], gk_ref[...], X_HEADS)
    v_ref[0] = kv[:, X_WIDTH:]


def _mem_kv(mem, g_mem, w_kv, g_xk):
    n, m, _ = mem.shape
    out = jax.ShapeDtypeStruct((n, m, X_WIDTH), F32)
    return pl.pallas_call(
        _mem_kv_kernel,
        grid=(n,),
        in_specs=[pl.BlockSpec((1, m, D_MODEL), lambda i: (i, 0, 0)),
                  pl.BlockSpec((1, D_MODEL), lambda i: (0, 0)),
                  pl.BlockSpec((D_MODEL, 2 * X_WIDTH), lambda i: (0, 0)),
                  pl.BlockSpec((1, X_HEAD_DIM), lambda i: (0, 0))],
        out_specs=[pl.BlockSpec((1, m, X_WIDTH), lambda i: (i, 0, 0))] * 2,
        out_shape=[out, out],
        compiler_params=_cparams(1),
        name="mem_kv",
    )(mem, g_mem, w_kv, g_xk)


def _mix_out_kernel(ret_ref, att_ref, x_ref, wo_ref, g2_ref, wq_ref, gq_ref, *rest):
    h_ref, q_ref = rest[-2:]
    mixed = jnp.concatenate([ret_ref[...], att_ref[...]], axis=-1).astype(BF16)
    h = x_ref[...] + _dot(mixed, wo_ref[...])
    h_ref[...] = h
    q = _dot(_rms(h, g2_ref[...]).astype(BF16), wq_ref[...])
    q_ref[...] = _lane_tile_norm(q, gq_ref[...], X_HEADS)


def _mix_out(ret_n, att_o, x, w_out, g2, w_qx, g_xq, tm, row_start, prev=None):
    return _group_rows_call(_mix_out_kernel, "mix_out", [ret_n, att_o], [x],
                            [w_out, g2, w_qx, g_xq], [], [D_MODEL, X_WIDTH], ret_n.shape[0], tm,
                            row_start, prev)


def _xattn_prompt_kernel(q_ref, k_ref, v_ref, o_ref):
    q = q_ref[...]
    k = k_ref[0]
    v = v_ref[0]
    scale = X_HEAD_DIM ** -0.5
    outs = []
    for h in range(X_HEADS):
        sl = slice(h * LANES, (h + 1) * LANES)
        s = _dot_nt(q[:, sl].astype(BF16), k[:, sl].astype(BF16)) * scale
        p = jnp.exp(s - jnp.max(s, axis=-1, keepdims=True))
        o = _dot(p.astype(BF16), v[:, sl].astype(BF16))
        outs.append(o / jnp.sum(p, axis=-1, keepdims=True))
    o_ref[...] = jnp.concatenate(outs, axis=-1)


def _xattn_prompt(qx, mk, mv, n_batch, seq, tq=512):
    t_all = qx.shape[0]
    per = seq // tq
    m = mk.shape[1]
    rows = pl.BlockSpec((tq, X_WIDTH), lambda n, i: (n * per + i, 0))
    mem = pl.BlockSpec((1, m, X_WIDTH), lambda n, i: (n, 0, 0))
    return pl.pallas_call(
        _xattn_prompt_kernel,
        grid=(n_batch, per),
        in_specs=[rows, mem, mem],
        out_specs=rows,
        out_shape=jax.ShapeDtypeStruct((t_all, X_WIDTH), F32),
        input_output_aliases={0: 0},
        compiler_params=_cparams(2),
        name="xattn_prompt",
    )(qx, mk, mv)


def _xattn_sample_kernel(q_ref, k_ref, v_ref, o_ref, *, group, n_mem):
    i = pl.program_id(0)
    scale = X_HEAD_DIM ** -0.5
    for j in range(group):
        n = i * group + j
        q = q_ref[pl.ds(n, 1), :]
        outs = []
        for h in range(X_HEADS):
            rows = pl.ds(h, n_mem, stride=X_HEADS)
            qh = q[:, h * LANES:(h + 1) * LANES]
            s = jnp.sum(k_ref[j, rows, :] * qh, axis=-1, keepdims=True) * scale
            p = jnp.exp(s - jnp.max(s, axis=0, keepdims=True))
            o = jnp.sum(p * v_ref[j, rows, :], axis=0, keepdims=True)
            outs.append(o / jnp.sum(p, axis=0, keepdims=True))
        o_ref[pl.ds(n, 1), :] = jnp.concatenate(outs, axis=-1)


XATTN_SAMPLE_GROUP = 8


def _xattn_sample(o_all, mk, mv):
    n_s, mh, _ = mk.shape
    t_all = o_all.shape[0]
    last = t_all // n_s - 1
    group = XATTN_SAMPLE_GROUP
    rows = pl.BlockSpec((n_s, X_WIDTH), lambda n: (last, 0))
    mem = pl.BlockSpec((group, mh, X_HEAD_DIM), lambda n: (n, 0, 0))
    return pl.pallas_call(
        functools.partial(_xattn_sample_kernel, group=group, n_mem=mh // X_HEADS),
        grid=(n_s // group,),
        in_specs=[rows, mem, mem],
        out_specs=rows,
        out_shape=jax.ShapeDtypeStruct(o_all.shape, F32),
        input_output_aliases={0: 0},
        compiler_params=_cparams(1),
        name="xattn_sample",
    )(o_all, mk, mv)


ROW_TILE_ROWS = D_MODEL // LANES


def _store_row_tiles(ref, x):
    rows = x.shape[0]
    for c in range(ROW_TILE_ROWS):
        ref[pl.ds(c, rows, stride=ROW_TILE_ROWS), :] = x[:, c * LANES:(c + 1) * LANES]


def _load_row_tiles(ref, rows):
    return jnp.concatenate([ref[pl.ds(c, rows, stride=ROW_TILE_ROWS), :]
                            for c in range(ROW_TILE_ROWS)], axis=1)


def _xout_router_kernel(o_ref, h_ref, wo_ref, g3_ref, wr_ref, br_ref,
                        h2_ref, xn_ref, idx_ref, gate_ref):
    h2 = h_ref[...] + _dot(o_ref[...].astype(BF16), wo_ref[...])
    h2_ref[...] = h2
    xn = _rms(h2, g3_ref[...])
    _store_row_tiles(xn_ref, xn)
    wr = wr_ref[...]
    w_hi = wr.astype(BF16)
    w_lo = (wr - w_hi.astype(F32)).astype(BF16)
    x_hi = xn.astype(BF16)
    x_lo = (xn - x_hi.astype(F32)).astype(BF16)
    logits = _dot_nt(w_hi, x_hi) + _dot_nt(w_hi, x_lo) + _dot_nt(w_lo, x_hi) + br_ref[...]
    eid = lax.broadcasted_iota(I32, logits.shape, 0)
    work = logits
    vals, idxs = [], []
    for _ in range(TOP_K):
        mx = jnp.max(work, axis=0, keepdims=True)
        ix = jnp.min(jnp.where(work == mx, eid, N_EXPERTS), axis=0, keepdims=True)
        vals.append(mx)
        idxs.append(ix)
        work = jnp.where(eid == ix, -jnp.inf, work)
    ex = [jnp.exp(v - vals[0]) for v in vals]
    tot = ex[0] + ex[1] + ex[2] + ex[3]
    idx_ref[...] = jnp.concatenate(idxs, axis=0)
    gate_ref[...] = jnp.concatenate([e / tot for e in ex], axis=0)


def _xout_router(o_all, h_all, w_ox, g3, w_rt, b_r):
    t_all = h_all.shape[0]
    tm = ROUTER_TILE
    row = lambda i: (i, 0)
    fixed = lambda i: (0, 0)
    colblk = lambda i: (0, i)
    return pl.pallas_call(
        _xout_router_kernel,
        grid=(t_all // tm,),
        in_specs=[pl.BlockSpec((tm, X_WIDTH), row),
                  pl.BlockSpec((tm, D_MODEL), row),
                  pl.BlockSpec((X_WIDTH, D_MODEL), fixed),
                  pl.BlockSpec((1, D_MODEL), fixed),
                  pl.BlockSpec((N_EXPERTS, D_MODEL), fixed),
                  pl.BlockSpec((N_EXPERTS, 1), fixed)],
        out_specs=[pl.BlockSpec((tm, D_MODEL), row),
                   pl.BlockSpec((tm * ROW_TILE_ROWS, LANES), row),
                   pl.BlockSpec((TOP_K, tm), colblk), pl.BlockSpec((TOP_K, tm), colblk)],
        out_shape=[jax.ShapeDtypeStruct((t_all, D_MODEL), F32),
                   jax.ShapeDtypeStruct((t_all * ROW_TILE_ROWS, LANES), F32),
                   jax.ShapeDtypeStruct((TOP_K, t_all), I32),
                   jax.ShapeDtypeStruct((TOP_K, t_all), F32)],
        compiler_params=_cparams(1),
        name="xout_router",
    )(o_all, h_all, w_ox, g3, w_rt, b_r)


def _route_kernel(idx_ref, dest_ref, be_ref, nu_ref, pe_ref, seg_ref, nxt_ref, *, t_all,
                  n_blocks_pad):
    bm = MOE_BLOCK_ROWS
    nt = t_all // LANES
    e_col = lax.broadcasted_iota(I32, (N_EXPERTS, 1), 0)
    hp = lax.Precision.HIGHEST

    def multi_hot(j):
        blk = idx_ref[:, pl.ds(pl.multiple_of(j * LANES, LANES), LANES)]
        mh = jnp.zeros((N_EXPERTS, LANES), F32)
        for k in range(TOP_K):
            mh = mh + (e_col == blk[k:k + 1, :]).astype(F32)
        return blk, mh

    def count_body(j, c):
        _, mh = multi_hot(j)
        return c + jnp.sum(mh, axis=1, keepdims=True)

    counts = lax.fori_loop(0, nt, count_body, jnp.zeros((N_EXPERTS, 1), F32))
    padded = jnp.ceil(counts * (1.0 / bm)) * bm
    tri = (lax.broadcasted_iota(I32, (N_EXPERTS, N_EXPERTS), 1)
           <= lax.broadcasted_iota(I32, (N_EXPERTS, N_EXPERTS), 0)).astype(F32)
    pad_end = jnp.dot(tri, jnp.broadcast_to(padded, (N_EXPERTS, LANES)), precision=hp,
                      preferred_element_type=F32)
    pad_start = pad_end[:, 0:1] - padded
    upper = (lax.broadcasted_iota(I32, (LANES, LANES), 0)
             < lax.broadcasted_iota(I32, (LANES, LANES), 1)).astype(BF16)

    def dest_body(j, carry):
        blk, mh = multi_hot(j)
        rank = carry + _dot(mh.astype(BF16), upper)
        base = pad_start + rank
        for k in range(TOP_K):
            d = jnp.sum(jnp.where(e_col == blk[k:k + 1, :], base, 0.0), axis=0, keepdims=True)
            dest_ref[pl.ds(k, 1), pl.ds(pl.multiple_of(j * LANES, LANES), LANES)] = d.astype(I32)
        return carry + jnp.sum(mh, axis=1, keepdims=True)

    lax.fori_loop(0, nt, dest_body, jnp.zeros((N_EXPERTS, 1), F32))

    b_row = lax.broadcasted_iota(I32, (1, n_blocks_pad), 1).astype(F32) * bm
    be = jnp.sum((pad_end[:, 0:1] <= b_row).astype(F32), axis=0, keepdims=True)
    be = jnp.minimum(be, N_EXPERTS - 1.0)
    be_ref[...] = be.astype(I32)
    e_f = e_col.astype(F32)
    nonempty = counts > 0.0
    seg_ref[...] = jnp.sum(jnp.where((e_f < be) & nonempty, 1.0, 0.0), axis=0,
                           keepdims=True).astype(I32)
    nxt = jnp.min(jnp.where((e_f > be) & nonempty, e_f, float(N_EXPERTS)), axis=0, keepdims=True)
    nxt_ref[...] = jnp.where(nxt < N_EXPERTS, nxt, -1.0).astype(I32)
    nu_ref[...] = (pad_end[N_EXPERTS - 1:N_EXPERTS, :] * (1.0 / bm)).astype(I32)
    lane_row = lax.broadcasted_iota(I32, (1, LANES), 1)
    pe_ref[...] = jnp.sum(jnp.where(e_col == lane_row, pad_end, 0.0), axis=0,
                          keepdims=True).astype(I32)


def _route(idx_t, n_blocks_pad):
    t_all = idx_t.shape[1]
    return pl.pallas_call(
        functools.partial(_route_kernel, t_all=t_all, n_blocks_pad=n_blocks_pad),
        out_shape=[jax.ShapeDtypeStruct((TOP_K, t_all), I32),
                   jax.ShapeDtypeStruct((1, n_blocks_pad), I32),
                   jax.ShapeDtypeStruct((1, LANES), I32),
                   jax.ShapeDtypeStruct((1, LANES), I32),
                   jax.ShapeDtypeStruct((1, n_blocks_pad), I32),
                   jax.ShapeDtypeStruct((1, n_blocks_pad), I32)],
        compiler_params=pltpu.CompilerParams(vmem_limit_bytes=VMEM_LIMIT),
        name="moe_route",
    )(idx_t)


def _dispatch_kernel(dest_sm, pe_sm, nu_sm, x_ref, xs_ref, zbuf, sem, zsem, *, t_all, n_blocks):
    i = pl.program_id(0)
    tm = x_ref.shape[0]
    bm = MOE_BLOCK_ROWS

    @pl.when(i == 0)
    def _():
        zbuf[...] = jnp.zeros_like(zbuf)

        def zero_block(row0):
            return pltpu.make_async_copy(zbuf, xs_ref.at[pl.ds(row0, bm)], zsem)

        ends = [pe_sm[e] for e in range(N_EXPERTS)]
        used = [ends[e] > (ends[e - 1] if e else 0) for e in range(N_EXPERTS)]
        for e in range(N_EXPERTS):
            @pl.when(used[e])
            def _(e=e):
                zero_block(ends[e] - bm).start()

        def tail_start(b, c):
            zero_block(b * bm).start()
            return c

        lax.fori_loop(nu_sm[0], n_blocks, tail_start, 0)
        for e in range(N_EXPERTS):
            @pl.when(used[e])
            def _():
                zero_block(0).wait()

        def tail_wait(b, c):
            zero_block(0).wait()
            return c

        lax.fori_loop(nu_sm[0], n_blocks, tail_wait, 0)

    def start_body(j, c):
        for k in range(TOP_K):
            d = dest_sm[k * t_all + i * tm + j]
            pltpu.make_async_copy(x_ref.at[j], xs_ref.at[d], sem).start(priority=k % 2)
        return c

    lax.fori_loop(0, tm, start_body, 0, unroll=ROW_DMA_UNROLL)
    for k in range(TOP_K):
        pltpu.make_async_copy(x_ref, xs_ref.at[pl.ds(0, tm)], sem).wait()


ROW_DMA_UNROLL = 4


def _dispatch(dest_flat, pad_end, n_used, xn_all, n_blocks):
    t_all = xn_all.shape[0]
    tm = TOKEN_TILE
    bm = MOE_BLOCK_ROWS
    return pl.pallas_call(
        functools.partial(_dispatch_kernel, t_all=t_all, n_blocks=n_blocks),
        grid_spec=pltpu.PrefetchScalarGridSpec(
            num_scalar_prefetch=3,
            grid=(t_all // tm,),
            in_specs=[pl.BlockSpec((tm, ROW_TILE_ROWS, LANES), lambda i, d, pe, nu: (i, 0, 0))],
            out_specs=pl.BlockSpec(memory_space=pl.ANY),
            scratch_shapes=[pltpu.VMEM((bm, ROW_TILE_ROWS, LANES), F32),
                            pltpu.SemaphoreType.DMA, pltpu.SemaphoreType.DMA]),
        out_shape=jax.ShapeDtypeStruct((n_blocks * bm, ROW_TILE_ROWS, LANES), F32),
        compiler_params=_cparams(1),
        name="moe_dispatch",
    )(dest_flat, pad_end, n_used, xn_all)


def _expert_kernel(be_sm, nu_sm, seg_sm, nxt_sm, x_ref, wgu_hbm, bgu_ref, wd_hbm, bd_ref, y_ref,
                   wgu_bf, wd_bf, wgu_f, wd_f, wsems):
    b = pl.program_id(0)
    used = b < nu_sm[0]
    first = jnp.logical_and(
        used, jnp.logical_or(b == 0, be_sm[b] != be_sm[jnp.maximum(b - 1, 0)]))
    slot = seg_sm[b] % 2

    def weight_copies(e, s):
        return [pltpu.make_async_copy(wgu_hbm.at[e], wgu_f.at[s], wsems.at[s]),
                pltpu.make_async_copy(wd_hbm.at[e], wd_f.at[s], wsems.at[s])]

    @pl.when(b == 0)
    def _():
        for cp in weight_copies(be_sm[0], 0):
            cp.start()

    @pl.when(first)
    def _():
        for cp in weight_copies(be_sm[b], slot):
            cp.wait()
        wgu_bf[...] = wgu_f[slot].astype(BF16)
        wd_bf[...] = wd_f[slot].astype(BF16)
        nx = nxt_sm[b]

        @pl.when(nx >= 0)
        def _():
            for cp in weight_copies(nx, 1 - slot):
                cp.start()

    @pl.when(b < nu_sm[0])
    def _():
        x = _load_row_tiles(x_ref, MOE_BLOCK_ROWS).astype(BF16)
        h = _dot(x, wgu_bf[...]) + bgu_ref[0]
        glu = jnp.minimum(h[:, :D_FF], SWIGLU_LIMIT)
        lin = jnp.clip(h[:, D_FF:], -SWIGLU_LIMIT, SWIGLU_LIMIT)
        act = glu * _sigmoid(SWIGLU_ALPHA * glu) * (lin + 1.0)
        _store_row_tiles(y_ref, _dot(act.astype(BF16), wd_bf[...]) + bd_ref[0])

    @pl.when(b >= nu_sm[0])
    def _():
        y_ref[...] = jnp.zeros_like(y_ref)


def _experts(block_e, n_used, seg, nxt, xs, w_gu, b_gu, w_down, b_down):
    cap = xs.shape[0] // ROW_TILE_ROWS
    bm = MOE_BLOCK_ROWS
    bias = lambda b, be, nu, sg, nx: (be[b], 0, 0)
    rows = lambda b, be, nu, sg, nx: (b, 0)
    hbm = pl.BlockSpec(memory_space=pl.ANY)
    return pl.pallas_call(
        _expert_kernel,
        grid_spec=pltpu.PrefetchScalarGridSpec(
            num_scalar_prefetch=4,
            grid=(cap // bm,),
            in_specs=[pl.BlockSpec((bm * ROW_TILE_ROWS, LANES), rows),
                      hbm,
                      pl.BlockSpec((1, 1, 2 * D_FF), bias),
                      hbm,
                      pl.BlockSpec((1, 1, D_MODEL), bias)],
            out_specs=pl.BlockSpec((bm * ROW_TILE_ROWS, LANES), rows),
            scratch_shapes=[pltpu.VMEM((D_MODEL, 2 * D_FF), BF16),
                            pltpu.VMEM((D_FF, D_MODEL), BF16),
                            pltpu.VMEM((2, D_MODEL, 2 * D_FF), F32),
                            pltpu.VMEM((2, D_FF, D_MODEL), F32),
                            pltpu.SemaphoreType.DMA((2,))]),
        out_shape=jax.ShapeDtypeStruct((cap * ROW_TILE_ROWS, LANES), F32),
        compiler_params=_cparams(1, VMEM_LIMIT_EXPERTS),
        name="moe_experts",
    )(block_e, n_used, seg, nxt, xs, w_gu, b_gu, w_down, b_down)


def _combine_kernel(dest_sm, yb_ref, h_ref, g_ref, op_ref, os_ref, buf, sems, *, t_all):
    i = pl.program_id(0)
    last = pl.num_programs(0) - 1
    tm = h_ref.shape[0]

    def gather(step, slot):
        def start_body(j, c):
            for k in range(TOP_K):
                d = dest_sm[k * t_all + step * tm + j]
                tile = pl.ds(pl.multiple_of(j * ROW_TILE_ROWS, ROW_TILE_ROWS), ROW_TILE_ROWS)
                pltpu.make_async_copy(yb_ref.at[d], buf.at[slot, k, tile],
                                      sems.at[slot]).start(priority=k % 2)
            return c

        lax.fori_loop(0, tm, start_body, 0, unroll=ROW_DMA_UNROLL)

    @pl.when(i == 0)
    def _():
        gather(0, 0)

    @pl.when(i < last)
    def _():
        gather(i + 1, (i + 1) % 2)

    slot = i % 2
    for k in range(TOP_K):
        pltpu.make_async_copy(buf.at[slot, k], buf.at[slot, k], sems.at[slot]).wait()

    g = g_ref[...]
    y = h_ref[...]
    for k in range(TOP_K):
        y = y + g[:, k:k + 1] * _load_row_tiles(buf.at[slot, k], tm)

    @pl.when(i < last)
    def _():
        op_ref[...] = y

    @pl.when(i == last)
    def _():
        os_ref[...] = y


def _combine(dest_flat, yb, h2_all, gates, n_prompt_rows):
    t_all = h2_all.shape[0]
    tm = ROW_TILE
    n_p_tiles = n_prompt_rows // tm
    assert t_all == n_prompt_rows + tm
    return pl.pallas_call(
        functools.partial(_combine_kernel, t_all=t_all),
        grid_spec=pltpu.PrefetchScalarGridSpec(
            num_scalar_prefetch=1,
            grid=(t_all // tm,),
            in_specs=[pl.BlockSpec(memory_space=pl.ANY),
                      pl.BlockSpec((tm, D_MODEL), lambda i, d: (i, 0)),
                      pl.BlockSpec((tm, TOP_K), lambda i, d: (i, 0))],
            out_specs=[pl.BlockSpec((tm, D_MODEL), lambda i, d: (jnp.minimum(i, n_p_tiles - 1), 0)),
                       pl.BlockSpec((tm, D_MODEL), lambda i, d: (0, 0))],
            scratch_shapes=[pltpu.VMEM((2, TOP_K, tm * ROW_TILE_ROWS, LANES), F32),
                            pltpu.SemaphoreType.DMA((2,))]),
        out_shape=[jax.ShapeDtypeStruct((n_prompt_rows, D_MODEL), F32),
                   jax.ShapeDtypeStruct((tm, D_MODEL), F32)],
        compiler_params=_cparams(1),
        name="moe_combine",
    )(dest_flat, yb, h2_all, gates)


def _rope_tables(pos):
    half = HEAD_DIM // 2
    inv = jnp.exp(-math.log(ROPE_THETA) * jnp.arange(half, dtype=F32) / half)
    ang = pos.astype(F32)[:, None] * inv[None, :]
    cos, sin = jnp.cos(ang), jnp.sin(ang)
    cos_t = jnp.concatenate([cos, cos] * HEADS_PER_TILE, axis=-1)
    sin_t = jnp.concatenate([-sin, sin] * HEADS_PER_TILE, axis=-1)
    return cos_t, sin_t


def _block_diag_ones(n, blk):
    r = jnp.arange(n) // blk
    return (r[:, None] == r[None, :]).astype(BF16)


def _layer(x_prompt, x_sample, state_ret, cache_win_k, cache_win_v, cache_mem_k, cache_mem_v,
           mem_prompt, g_norm1, w_in, g_att_q, g_att_k, g_ret_gn, w_out, g_norm2, g_mem,
           w_q_x, w_kv_x, g_x_q, g_x_k, w_o_x, g_norm3, w_router, b_router, w_gu, b_gu,
           w_down, b_down):
    n_b, seq, _ = x_prompt.shape
    n_s = x_sample.shape[0]
    assert x_sample.shape[1] == 1 and n_s == ROW_TILE
    t_p = n_b * seq
    t_all = t_p + n_s
    assert t_all % TOKEN_TILE == 0 and t_all % ROUTER_TILE == 0 and t_p % ROW_TILE == 0

    assert seq % PROMPT_TILE == 0
    x_p = x_prompt.reshape(t_p, D_MODEL)
    x_s = x_sample.reshape(n_s, D_MODEL)
    cos_p, sin_p = _rope_tables(jnp.arange(seq, dtype=jnp.int32))
    cos_s, sin_s = _rope_tables(jnp.full((n_s,), PAST_LEN, jnp.int32))
    log_g = jnp.log1p(-jnp.exp2(-5.0 - jnp.arange(RET_HEADS, dtype=F32)))
    lg_row = jnp.repeat(log_g, HEAD_DIM)[None, :]
    gn_row = g_ret_gn.reshape(1, RET_WIDTH)
    gn_col = g_ret_gn.reshape(RET_WIDTH, 1)
    gq = jnp.tile(g_att_q.reshape(1, HEAD_DIM), (1, ATT_HEADS))
    gk = jnp.tile(g_att_k.reshape(1, HEAD_DIM), (1, ATT_HEADS))
    seg = _block_diag_ones(ATT_WIDTH, HEAD_DIM)

    g1 = g_norm1.reshape(1, D_MODEL)
    w_in_bf = w_in.astype(BF16)
    proj = _in_proj(x_p, g1, w_in_bf, cos_p, sin_p, gq, gk, seg, t_all, PROMPT_TILE, 0)
    rq, rk, rv, rg, aq, ak, av = _in_proj(x_s, g1, w_in_bf, cos_s, sin_s, gq, gk, seg, t_all, n_s,
                                          t_p, prev=proj)

    ret_n, st_p = _ret_prompt(rq, rk, rv, rg, gn_row, lg_row, n_b, seq)
    att_o, wk_p, wv_p = _dil_prompt(aq, ak, av, n_b, seq)
    state_t = jnp.transpose(state_ret, (1, 2, 3, 0))
    ret_n, st_s = _ret_sample(rq, rk, rv, ret_n, gn_col, lg_row, state_t)
    ck = jnp.transpose(cache_win_k, (0, 2, 3, 1))
    cv = jnp.transpose(cache_win_v, (0, 2, 3, 1))
    wk_s, wv_s, att_o = _win_sample(att_o, ak, av, ck, cv)

    mix_w = (w_out.astype(BF16), g_norm2.reshape(1, D_MODEL), w_q_x.astype(BF16),
             g_x_q.reshape(1, X_HEAD_DIM))
    mixed = _mix_out(ret_n, att_o, x_p, *mix_w, PROMPT_TILE, 0)
    h_all, qx = _mix_out(ret_n, att_o, x_s, *mix_w, n_s, t_p, prev=mixed)

    mk_p, mv_p = _mem_kv(mem_prompt, g_mem.reshape(1, D_MODEL), w_kv_x.astype(BF16),
                         g_x_k.reshape(1, X_HEAD_DIM))
    o_all = _xattn_prompt(qx, mk_p, mv_p, n_b, seq)
    n_mem = cache_mem_k.shape[1]
    o_all = _xattn_sample(o_all, cache_mem_k.reshape(n_s, n_mem * X_HEADS, X_HEAD_DIM),
                          cache_mem_v.reshape(n_s, n_mem * X_HEADS, X_HEAD_DIM))

    h2_all, xn_all, idx_t, gate_t = _xout_router(
        o_all, h_all, w_o_x.astype(BF16), g_norm3.reshape(1, D_MODEL),
        jnp.transpose(w_router), b_router.reshape(N_EXPERTS, 1))

    bm = MOE_BLOCK_ROWS
    n_blocks = -(-(t_all * TOP_K) // bm) + N_EXPERTS
    n_blocks_pad = -(-n_blocks // LANES) * LANES
    dest_t, be, nu, pe, seg_b, nxt_b = _route(idx_t, n_blocks_pad)
    dest_flat = dest_t.reshape(TOP_K * t_all)
    cap = n_blocks * bm
    xs = _dispatch(dest_flat, pe.reshape(LANES), nu.reshape(LANES),
                   xn_all.reshape(t_all, ROW_TILE_ROWS, LANES), n_blocks)
    yb = _experts(be.reshape(n_blocks_pad), nu.reshape(LANES), seg_b.reshape(n_blocks_pad),
                  nxt_b.reshape(n_blocks_pad), xs.reshape(cap * ROW_TILE_ROWS, LANES),
                  w_gu, b_gu.reshape(N_EXPERTS, 1, 2 * D_FF), w_down,
                  b_down.reshape(N_EXPERTS, 1, D_MODEL))
    y_p, y_s = _combine(dest_flat, yb.reshape(cap, ROW_TILE_ROWS, LANES), h2_all,
                        jnp.transpose(gate_t), t_p)

    st_p = jnp.stack([st_p[:, :, :HEAD_DIM, :HEAD_DIM], st_p[:, :, HEAD_DIM:, HEAD_DIM:]], axis=2)
    st_p = st_p.reshape(n_b, RET_HEADS, HEAD_DIM, HEAD_DIM)
    return (y_p.reshape(n_b, seq, D_MODEL),
            y_s.reshape(n_s, 1, D_MODEL),
            st_p,
            jnp.transpose(st_s, (3, 0, 1, 2)),
            jnp.transpose(wk_p, (0, 3, 1, 2)),
            jnp.transpose(wv_p, (0, 3, 1, 2)),
            jnp.transpose(wk_s, (0, 3, 1, 2)),
            jnp.transpose(wv_s, (0, 3, 1, 2)),
            mk_p.reshape(n_b, n_mem, X_HEADS, X_HEAD_DIM),
            mv_p.reshape(n_b, n_mem, X_HEADS, X_HEAD_DIM))


def kernel(x_prompt, x_sample, state_ret, cache_win_k, cache_win_v, cache_mem_k, cache_mem_v,
           mem_prompt, g_norm1, w_in, g_att_q, g_att_k, g_ret_gn, w_out, g_norm2, g_mem,
           w_q_x, w_kv_x, g_x_q, g_x_k, w_o_x, g_norm3, w_router, b_router, w_gu, b_gu,
           w_down, b_down):
    assert state_ret.shape[0] == 1, "single-layer trunk"
    outs = _layer(x_prompt, x_sample, state_ret[0], cache_win_k[0], cache_win_v[0],
                  cache_mem_k[0], cache_mem_v[0], mem_prompt, g_norm1[0], w_in[0], g_att_q[0],
                  g_att_k[0], g_ret_gn[0], w_out[0], g_norm2[0], g_mem[0], w_q_x[0], w_kv_x[0],
                  g_x_q[0], g_x_k[0], w_o_x[0], g_norm3[0], w_router[0], b_router[0], w_gu[0],
                  b_gu[0], w_down[0], b_down[0])
    y_p, y_s = outs[0], outs[1]
    return (y_p, y_s) + tuple(o[None] for o in outs[2:])
```
